```python
import jax, jax.numpy as jnp
from jax import lax
import numpy as np

D_MODEL = 1024
BATCH = 4
SEQ = 8192
DEPTH = 4

GRID_W = 64
CTX_LEN = 256
N_MIXERS = 3
EPS = 1e-6
CONV_WIDTH = 31
FNET_GROUPS = 4
FNET_GROUP_DIM = D_MODEL // FNET_GROUPS
HEAD_DIM = 128
N_Q_HEADS = D_MODEL // HEAD_DIM
N_KV_HEADS = 2
Q_PER_KV = N_Q_HEADS // N_KV_HEADS
Q_DIM = N_Q_HEADS * HEAD_DIM
KV_DIM = N_KV_HEADS * HEAD_DIM
ROPE_THETA = 10000.0
Q_BLOCK = 128
N_EXPERT_GROUPS = 4
EXPERTS_PER_GROUP = 8
N_EXPERTS = N_EXPERT_GROUPS * EXPERTS_PER_GROUP
TOP_K_INNER = 2
D_EXPERT = D_MODEL // 2
MOE_BLOCK = 256

kernel_name = 'hybrid_conv_fourier_gqa_hmoe_dit'


def _rmsnorm(x, g):
    xf = x.astype(jnp.float32)
    y = xf * lax.rsqrt(jnp.mean(xf * xf, axis=-1, keepdims=True) + EPS)
    return (y * g.astype(jnp.float32)).astype(x.dtype)


def _modulate(h, shift, scale):
    return h * (1 + scale) + shift


def _conformer_conv(h, w_in, b_in, w_dw, b_dw, g_norm, w_out, b_out):
    a, b = jnp.split(h @ w_in + b_in, 2, axis=-1)
    v = a * jax.nn.sigmoid(b)
    pad = CONV_WIDTH // 2
    v = lax.conv_general_dilated(v, w_dw[:, None, :], window_strides=(1,), padding=[(pad, pad)],
                                 dimension_numbers=('NWC', 'WIO', 'NWC'),
                                 feature_group_count=D_MODEL) + b_dw
    v = jax.nn.silu(_rmsnorm(v, g_norm))
    return v @ w_out + b_out


def _fourier_mix(h, w_out, b_out):
    B_, L, D = h.shape
    hg = h.astype(jnp.float32).reshape(B_, L, FNET_GROUPS, FNET_GROUP_DIM)
    f = jnp.fft.fft2(hg, axes=(1, 3), norm='ortho').real
    return f.reshape(B_, L, D).astype(h.dtype) @ w_out + b_out


def _axial_rope_tables(rows):
    row = jnp.repeat(jnp.arange(rows, dtype=jnp.float32), GRID_W)
    col = jnp.tile(jnp.arange(GRID_W, dtype=jnp.float32), rows)
    n_pairs_axis = HEAD_DIM // 4
    inv = ROPE_THETA ** (-jnp.arange(n_pairs_axis, dtype=jnp.float32) / n_pairs_axis)
    ang = jnp.concatenate([row[:, None] * inv, col[:, None] * inv], axis=-1)
    return jnp.cos(ang), jnp.sin(ang)


def _rope(x, cos, sin):
    xf = x.astype(jnp.float32).reshape(x.shape[:-1] + (HEAD_DIM // 2, 2))
    x0, x1 = xf[..., 0], xf[..., 1]
    shp = (cos.shape[0],) + (1,) * (x.ndim - 3) + (cos.shape[1],)
    c_, s_ = cos.reshape(shp), sin.reshape(shp)
    out = jnp.stack([x0 * c_ - x1 * s_, x0 * s_ + x1 * c_], axis=-1).reshape(x.shape)
    return out.astype(x.dtype)


def _attn_q(h, w_q, gq):
    B_, L, _ = h.shape
    return _rmsnorm((h @ w_q).reshape(B_, L, N_KV_HEADS, Q_PER_KV, HEAD_DIM), gq)


def _attn_kv(h, w_kv, gk):
    B_, L, _ = h.shape
    k, v = jnp.split(h @ w_kv, 2, axis=-1)
    k = _rmsnorm(k.reshape(B_, L, N_KV_HEADS, HEAD_DIM), gk)
    return k, v.reshape(B_, L, N_KV_HEADS, HEAD_DIM)


def _sdpa(q, k, v):
    s = jnp.einsum('bqkgd,bskd->bkgqs', q, k, preferred_element_type=jnp.float32) * (HEAD_DIM ** -0.5)
    p = jax.nn.softmax(s, axis=-1)
    return jnp.einsum('bkgqs,bskd->bqkgd', p.astype(v.dtype), v)


def _hier_moe(t, w_group, b_group, w_expert, b_expert, w_gate, w_up, w_down):
    T_, D = t.shape
    tf = t.astype(jnp.float32)
    g_prob = jax.nn.softmax(tf @ w_group.astype(jnp.float32) + b_group.astype(jnp.float32), axis=-1)
    g_top, g_idx = lax.top_k(g_prob, 1)
    e_logits = (tf @ w_expert.astype(jnp.float32) + b_expert.astype(jnp.float32)).reshape(T_, N_EXPERT_GROUPS, EXPERTS_PER_GROUP)
    e_logits = e_logits[jnp.arange(T_), g_idx[:, 0]]
    e_top, e_idx = lax.top_k(jax.nn.softmax(e_logits, axis=-1), TOP_K_INNER)
    gate = g_top * e_top / jnp.sum(e_top, axis=-1, keepdims=True)
    expert = g_idx * EXPERTS_PER_GROUP + e_idx
    A = T_ * TOP_K_INNER
    e_flat = expert.reshape(A)
    order = jnp.argsort(e_flat)
    e_sorted = e_flat[order]
    counts = jnp.zeros((N_EXPERTS,), jnp.int32).at[e_flat].add(1)
    padded = (counts + MOE_BLOCK - 1) // MOE_BLOCK * MOE_BLOCK
    start = jnp.cumsum(counts) - counts
    pad_end = jnp.cumsum(padded)
    pad_start = pad_end - padded
    dest = pad_start[e_sorted] + jnp.arange(A, dtype=jnp.int32) - start[e_sorted]
    n_blocks = -(-A // MOE_BLOCK) + N_EXPERTS
    P = n_blocks * MOE_BLOCK
    tok = order // TOP_K_INNER
    buf = jnp.zeros((P, D), t.dtype).at[dest].set(t[tok])
    blk_expert = jnp.minimum(jnp.searchsorted(pad_end, jnp.arange(n_blocks, dtype=jnp.int32) * MOE_BLOCK, side='right'), N_EXPERTS - 1)

    def run(args):
        xb, e = args
        return (jax.nn.silu(xb @ w_gate[e]) * (xb @ w_up[e])) @ w_down[e]

    yb = lax.map(run, (buf.reshape(n_blocks, MOE_BLOCK, D), blk_expert)).reshape(P, D)
    y_sorted = yb[dest] * gate.reshape(A)[order][:, None].astype(t.dtype)
    return jnp.zeros((T_, D), t.dtype).at[tok].add(y_sorted)


def setup_inputs(seed: int = 0) -> dict:
    key = jax.random.key(seed)
    ks = iter(jax.random.split(key, 32))
    nrm = lambda shape, scale: jax.random.normal(next(ks), shape, jnp.float32) * scale
    D = D_MODEL
    n_conv = len(range(0, DEPTH, N_MIXERS))
    n_four = len(range(1, DEPTH, N_MIXERS))
    n_attn = len(range(2, DEPTH, N_MIXERS))
    return {
        'x': nrm((BATCH, SEQ, D), 1.0),
        'c': nrm((BATCH, D), 1.0),
        'ctx': nrm((BATCH, CTX_LEN, D), 1.0),
        'c_ctx': nrm((D,), 1.0),
        'norm_g': 1.0 + nrm((DEPTH, 2, D), 0.05),
        'w_mod': nrm((DEPTH, D, 6 * D), 0.5 * D ** -0.5),
        'b_mod': nrm((DEPTH, 6 * D), 0.02),
        'conv_w_in': nrm((n_conv, D, 2 * D), D ** -0.5),
        'conv_b_in': nrm((n_conv, 2 * D), 0.02),
        'conv_w_dw': nrm((n_conv, CONV_WIDTH, D), CONV_WIDTH ** -0.5),
        'conv_b_dw': nrm((n_conv, D), 0.02),
        'conv_norm_g': 1.0 + nrm((n_conv, D), 0.05),
        'conv_w_out': nrm((n_conv, D, D), D ** -0.5),
        'conv_b_out': nrm((n_conv, D), 0.02),
        'fnet_w_out': nrm((n_four, D, D), D ** -0.5),
        'fnet_b_out': nrm((n_four, D), 0.02),
        'attn_w_qkv': nrm((n_attn, D, Q_DIM + 2 * KV_DIM), D ** -0.5),
        'attn_q_norm_g': 1.0 + nrm((n_attn, HEAD_DIM), 0.05),
        'attn_k_norm_g': 1.0 + nrm((n_attn, HEAD_DIM), 0.05),
        'attn_w_out': nrm((n_attn, Q_DIM, D), Q_DIM ** -0.5),
        'moe_w_group': nrm((DEPTH, D, N_EXPERT_GROUPS), D ** -0.5),
        'moe_b_group': nrm((DEPTH, N_EXPERT_GROUPS), 0.01),
        'moe_w_expert': nrm((DEPTH, D, N_EXPERTS), D ** -0.5),
        'moe_b_expert': nrm((DEPTH, N_EXPERTS), 0.01),
        'moe_w_gate': nrm((DEPTH, N_EXPERTS, D, D_EXPERT), D ** -0.5),
        'moe_w_up': nrm((DEPTH, N_EXPERTS, D, D_EXPERT), D ** -0.5),
        'moe_w_down': nrm((DEPTH, N_EXPERTS, D_EXPERT, D), D_EXPERT ** -0.5),
    }


def reference(x, c, ctx, c_ctx, norm_g, w_mod, b_mod,
              conv_w_in, conv_b_in, conv_w_dw, conv_b_dw, conv_norm_g, conv_w_out, conv_b_out,
              fnet_w_out, fnet_b_out,
              attn_w_qkv, attn_q_norm_g, attn_k_norm_g, attn_w_out,
              moe_w_group, moe_b_group, moe_w_expert, moe_b_expert, moe_w_gate, moe_w_up, moe_w_down):
    B_, L, D = x.shape
    C = ctx.shape[1]
    ROWS = L // GRID_W
    cos, sin = _axial_rope_tables(ROWS)
    last_reader = max([i for i in range(DEPTH) if i % N_MIXERS == 2], default=-1)
    silu_c = jax.nn.silu(c)
    silu_cc = jax.nn.silu(c_ctx)
    xc = ctx
    for i in range(DEPTH):
        m, j = i % N_MIXERS, i // N_MIXERS
        ctx_on = i <= last_reader
        ctx_full = i < last_reader
        sh1, sc1, g1, sh2, sc2, g2 = jnp.split((silu_c @ w_mod[i] + b_mod[i])[:, None, :], 6, axis=-1)
        hl = _modulate(_rmsnorm(x, norm_g[i, 0]), sh1, sc1)
        if ctx_on:
            nc = 6 if ctx_full else 2
            cmod = jnp.split(silu_cc @ w_mod[i][:, :nc * D] + b_mod[i][:nc * D], nc, axis=-1)
            hc = _modulate(_rmsnorm(xc, norm_g[i, 0]), cmod[0], cmod[1])
        if m == 0:
            cp = (conv_w_in[j], conv_b_in[j], conv_w_dw[j], conv_b_dw[j], conv_norm_g[j], conv_w_out[j], conv_b_out[j])
            x = x + g1 * _conformer_conv(hl, *cp)
            if ctx_full:
                xc = xc + cmod[2] * _conformer_conv(hc, *cp)
        elif m == 1:
            x = x + g1 * _fourier_mix(hl, fnet_w_out[j], fnet_b_out[j])
            if ctx_full:
                xc = xc + cmod[2] * _fourier_mix(hc, fnet_w_out[j], fnet_b_out[j])
        else:
            w_q, w_kv = attn_w_qkv[j][:, :Q_DIM], attn_w_qkv[j][:, Q_DIM:]
            q = _rope(_attn_q(hl, w_q, attn_q_norm_g[j]), cos, sin)
            k, v = _attn_kv(hl, w_kv, attn_k_norm_g[j])
            k = _rope(k, cos, sin)
            kc, vc = _attn_kv(hc, w_kv, attn_k_norm_g[j])
            k_all = jnp.concatenate([kc, k], axis=1)
            v_all = jnp.concatenate([vc, v], axis=1)
            n_blk = L // Q_BLOCK
            qb = q.reshape(B_, n_blk, Q_BLOCK, N_KV_HEADS, Q_PER_KV, HEAD_DIM).swapaxes(0, 1)
            o = lax.map(lambda qq: _sdpa(qq, k_all, v_all), qb)
            o = o.swapaxes(0, 1).reshape(B_, L, Q_DIM)
            x = x + g1 * (o @ attn_w_out[j])
            if ctx_full:
                qc = _attn_q(hc, w_q, attn_q_norm_g[j])
                oc = _sdpa(qc, kc, vc).reshape(B_, C, Q_DIM)
                xc = xc + cmod[2] * (oc @ attn_w_out[j])
        mp = (moe_w_group[i], moe_b_group[i], moe_w_expert[i], moe_b_expert[i], moe_w_gate[i], moe_w_up[i], moe_w_down[i])
        h2 = _modulate(_rmsnorm(x, norm_g[i, 1]), sh2, sc2).reshape(B_ * L, D)
        if ctx_full:
            h2c = _modulate(_rmsnorm(xc, norm_g[i, 1]), cmod[3], cmod[4]).reshape(B_ * C, D)
            y = _hier_moe(jnp.concatenate([h2, h2c], axis=0), *mp)
            x = x + g2 * y[:B_ * L].reshape(B_, L, D)
            xc = xc + cmod[5] * y[B_ * L:].reshape(B_, C, D)
        else:
            x = x + g2 * _hier_moe(h2, *mp).reshape(B_, L, D)
    return x
```

```python
import functools
import math

import jax
import jax.numpy as jnp
from jax import lax
from jax.experimental import pallas as pl
from jax.experimental.pallas import tpu as pltpu

F32 = jnp.float32
BF16 = jnp.bfloat16
HIGHEST = lax.Precision.HIGHEST

EPS = 1e-6
GRID_W = 64
N_MIXERS = 3
CONV_WIDTH = 31
CONV_HALO = 16
FNET_GROUPS = 4
HEAD_DIM = 128
N_KV_HEADS = 2
Q_PER_KV = 4
ROPE_THETA = 10000.0
N_EXPERT_GROUPS = 4
EXPERTS_PER_GROUP = 8
N_EXPERTS = N_EXPERT_GROUPS * EXPERTS_PER_GROUP
MOE_BLOCK = 256
LANES = 128
DFT_NA = 128
MOD_ROWS = 8
VMEM_LIMIT = 56 * 1024 * 1024


def _cparams(*sem):
    return pltpu.CompilerParams(dimension_semantics=sem, vmem_limit_bytes=VMEM_LIMIT)


def _rms(x, g):
    return x * lax.rsqrt(jnp.mean(x * x, axis=-1, keepdims=True) + EPS) * g


def _sigmoid(x):
    return 1.0 / (1.0 + jnp.exp(-x))


def _mod_row(mod_ref, b, k, d):
    return mod_ref[pl.ds(b, 1), pl.ds(k * d, d)]


def _norm_mod(x, g, mod_ref, b, k_shift, d):
    return _rms(x, g) * (1.0 + _mod_row(mod_ref, b, k_shift + 1, d)) + _mod_row(mod_ref, b, k_shift, d)


def _batch_row(bpb, n_lat_blocks):
    i = pl.program_id(0)
    return jnp.where(i < n_lat_blocks, i // bpb, MOD_ROWS // 2)


def _mod_kernel(c_ref, w_ref, b_ref, o_ref):
    c = c_ref[...]
    s = c * _sigmoid(c)
    o_ref[0] = jnp.dot(s, w_ref[0], precision=HIGHEST, preferred_element_type=F32) + b_ref[0]


def _modulation(cvec, w_mod, b_mod):
    depth, d, n = w_mod.shape
    tn = 1024
    return pl.pallas_call(
        _mod_kernel,
        grid=(depth, n // tn),
        in_specs=[
            pl.BlockSpec((MOD_ROWS, d), lambda l, j: (0, 0)),
            pl.BlockSpec((1, d, tn), lambda l, j: (l, 0, j)),
            pl.BlockSpec((1, 1, tn), lambda l, j: (l, 0, j)),
        ],
        out_specs=pl.BlockSpec((1, MOD_ROWS, tn), lambda l, j: (l, 0, j)),
        out_shape=jax.ShapeDtypeStruct((depth, MOD_ROWS, n), F32),
        compiler_params=_cparams("parallel", "parallel"),
        name="adaln_mod",
    )(cvec, w_mod, b_mod.reshape(depth, 1, n))


def _conv_in_kernel(x_ref, g_ref, mod_ref, w_ref, b_ref, v_ref, *, bpb, n_lat):
    d = x_ref.shape[1]
    b = _batch_row(bpb, n_lat)
    h = _norm_mod(x_ref[...], g_ref[...], mod_ref, b, 0, d)
    u = jnp.dot(h.astype(BF16), w_ref[...], preferred_element_type=F32) + b_ref[...]
    v_ref[...] = u[:, :d] * _sigmoid(u[:, d:])


def _conv_in(x, g, mod, w_in, b_in, *, tm, bpb, n_lat):
    t, d = x.shape
    return pl.pallas_call(
        functools.partial(_conv_in_kernel, bpb=bpb, n_lat=n_lat),
        grid=(t // tm,),
        in_specs=[
            pl.BlockSpec((tm, d), lambda i: (i, 0)),
            pl.BlockSpec((1, d), lambda i: (0, 0)),
            pl.BlockSpec(mod.shape, lambda i: (0, 0)),
            pl.BlockSpec((d, 2 * d), lambda i: (0, 0)),
            pl.BlockSpec((1, 2 * d), lambda i: (0, 0)),
        ],
        out_specs=pl.BlockSpec((tm, d), lambda i: (i, 0)),
        out_shape=jax.ShapeDtypeStruct((t, d), F32),
        compiler_params=_cparams("parallel"),
        name="conv_in",
    )(x, g.reshape(1, d), mod, w_in.astype(BF16), b_in.reshape(1, 2 * d))


def _conv_out_kernel(vp_ref, vc_ref, vn_ref, x_ref, wdw_ref, bdw_ref, gn_ref, wo_ref, bo_ref, mod_ref,
                     o_ref, buf, z, *, bpb, n_lat, tm):
    d = x_ref.shape[1]
    i = pl.program_id(0)
    j = i % bpb
    b = _batch_row(bpb, n_lat)
    h = CONV_HALO
    buf[0:h, :] = jnp.where(j == 0, 0.0, vp_ref[...])
    buf[h:h + tm, :] = vc_ref[...]
    buf[h + tm:, :] = jnp.where(j == bpb - 1, 0.0, vn_ref[...])
    rc, cw = 32, 256
    off = h - CONV_WIDTH // 2

    def col_chunk(ci, carry):
        c0 = pl.multiple_of(ci * cw, cw)
        for r in range(tm // rc):
            acc = jnp.zeros((rc, cw), F32)
            for t in range(CONV_WIDTH):
                acc = acc + buf[r * rc + off + t:r * rc + off + t + rc, pl.ds(c0, cw)] * wdw_ref[t:t + 1, pl.ds(c0, cw)]
            z[r * rc:(r + 1) * rc, pl.ds(c0, cw)] = acc
        return carry

    lax.fori_loop(0, d // cw, col_chunk, 0)
    zz = _rms(z[...] + bdw_ref[...], gn_ref[...])
    zz = zz * _sigmoid(zz)
    y = jnp.dot(zz.astype(BF16), wo_ref[...], preferred_element_type=F32) + bo_ref[...]
    o_ref[...] = x_ref[...] + _mod_row(mod_ref, b, 2, d) * y


def _conv_out(v, x, mod, w_dw, b_dw, g_norm, w_out, b_out, *, tm, bpb, n_lat):
    t, d = x.shape
    hb = tm // CONV_HALO
    n_halo = t // CONV_HALO
    wdw = jnp.zeros((32, d), F32).at[:CONV_WIDTH].set(w_dw)
    return pl.pallas_call(
        functools.partial(_conv_out_kernel, bpb=bpb, n_lat=n_lat, tm=tm),
        grid=(t // tm,),
        in_specs=[
            pl.BlockSpec((CONV_HALO, d), lambda i: (jnp.maximum(i * hb - 1, 0), 0)),
            pl.BlockSpec((tm, d), lambda i: (i, 0)),
            pl.BlockSpec((CONV_HALO, d), lambda i: (jnp.minimum((i + 1) * hb, n_halo - 1), 0)),
            pl.BlockSpec((tm, d), lambda i: (i, 0)),
            pl.BlockSpec((32, d), lambda i: (0, 0)),
            pl.BlockSpec((1, d), lambda i: (0, 0)),
            pl.BlockSpec((1, d), lambda i: (0, 0)),
            pl.BlockSpec((d, d), lambda i: (0, 0)),
            pl.BlockSpec((1, d), lambda i: (0, 0)),
            pl.BlockSpec(mod.shape, lambda i: (0, 0)),
        ],
        out_specs=pl.BlockSpec((tm, d), lambda i: (i, 0)),
        out_shape=jax.ShapeDtypeStruct((t, d), F32),
        scratch_shapes=[pltpu.VMEM((tm + 2 * CONV_HALO, d), F32), pltpu.VMEM((tm, d), F32)],
        compiler_params=_cparams("parallel"),
        name="conv_out",
    )(v, v, v, x, wdw, b_dw.reshape(1, d), g_norm.reshape(1, d), w_out.astype(BF16), b_out.reshape(1, d), mod)


def _prenorm_kernel(x_ref, g_ref, mod_ref, h_ref, *, bpb, n_lat):
    d = x_ref.shape[1]
    b = _batch_row(bpb, n_lat)
    h_ref[...] = _norm_mod(x_ref[...], g_ref[...], mod_ref, b, 0, d).astype(h_ref.dtype)


def _prenorm(x, g, mod, *, tm, bpb, n_lat, dtype):
    t, d = x.shape
    return pl.pallas_call(
        functools.partial(_prenorm_kernel, bpb=bpb, n_lat=n_lat),
        grid=(t // tm,),
        in_specs=[
            pl.BlockSpec((tm, d), lambda i: (i, 0)),
            pl.BlockSpec((1, d), lambda i: (0, 0)),
            pl.BlockSpec(mod.shape, lambda i: (0, 0)),
        ],
        out_specs=pl.BlockSpec((tm, d), lambda i: (i, 0)),
        out_shape=jax.ShapeDtypeStruct((t, d), dtype),
        compiler_params=_cparams("parallel"),
        name="prenorm",
    )(x, g.reshape(1, d), mod)


def _dft1_kernel(h_ref, m_ref, o_ref, *, nb_chunk, nb, na):
    bc = pl.program_id(2)
    for bi in range(nb_chunk):
        b = bc * nb_chunk + bi
        xs = h_ref[0, pl.ds(b, na, stride=nb), :]
        t = jnp.dot(m_ref[bi], xs.astype(BF16), preferred_element_type=F32)
        o_ref[0, 0, bi] = t[:na].astype(BF16)
        o_ref[0, 1, bi] = t[na:].astype(BF16)


def _dft_tables(seq):
    na, nb = DFT_NA, seq // DFT_NA
    ka = jnp.arange(na, dtype=jnp.int32)
    a = jnp.arange(na, dtype=jnp.int32)
    b = jnp.arange(nb, dtype=jnp.int32)
    n = a[None, None, :] * nb + b[:, None, None]
    ang = ((ka[None, :, None] * n) % seq).astype(F32) * (2.0 * math.pi / seq)
    m1 = jnp.concatenate([jnp.cos(ang), -jnp.sin(ang)], axis=1).astype(BF16)
    kb = jnp.arange(nb, dtype=jnp.int32)
    ang2 = ((kb[:, None] * b[None, :]) % nb).astype(F32) * (2.0 * math.pi / nb)
    c2, s2 = jnp.cos(ang2), jnp.sin(ang2)
    m2 = jnp.concatenate([jnp.concatenate([c2, s2], axis=1),
                          jnp.concatenate([s2, -c2], axis=1)], axis=0).astype(BF16)
    return m1, m2


def _const_lhs_matmul_kernel(a_ref, x_ref, o_ref):
    o_ref[0] = jnp.dot(a_ref[...], x_ref[0], preferred_element_type=F32).astype(o_ref.dtype)


def _const_lhs_matmul(a, x, *, cn):
    m, k = a.shape
    bsz, _, n = x.shape
    return pl.pallas_call(
        _const_lhs_matmul_kernel,
        grid=(bsz, n // cn),
        in_specs=[
            pl.BlockSpec((m, k), lambda b, j: (0, 0)),
            pl.BlockSpec((1, k, cn), lambda b, j: (b, 0, j)),
        ],
        out_specs=pl.BlockSpec((1, m, cn), lambda b, j: (b, 0, j)),
        out_shape=jax.ShapeDtypeStruct((bsz, m, n), BF16),
        compiler_params=_cparams("parallel", "parallel"),
        name="const_lhs_matmul",
    )(a, x)


def _seq_dft_two_stage(h, bsz, seq):
    d = h.shape[1]
    na, nb = DFT_NA, seq // DFT_NA
    gw = LANES
    nb_chunk = min(nb, 8)
    m1, m2 = _dft_tables(seq)
    t1 = pl.pallas_call(
        functools.partial(_dft1_kernel, nb_chunk=nb_chunk, nb=nb, na=na),
        grid=(bsz, d // gw, nb // nb_chunk),
        in_specs=[
            pl.BlockSpec((1, seq, gw), lambda b, g, c: (b, 0, g)),
            pl.BlockSpec((nb_chunk, 2 * na, na), lambda b, g, c: (c, 0, 0)),
        ],
        out_specs=pl.BlockSpec((1, 2, nb_chunk, na, gw), lambda b, g, c: (b, 0, c, 0, g)),
        out_shape=jax.ShapeDtypeStruct((bsz, 2, nb, na, d), BF16),
        compiler_params=_cparams("parallel", "parallel", "arbitrary"),
        name="dft_stage1",
    )(h.reshape(bsz, seq, d), m1)
    pq = _const_lhs_matmul(m2, t1.reshape(bsz, 2 * nb, na * d), cn=min(na * d, 8192))
    return pq.reshape(bsz, 2, seq, d)


def _seq_dft_dense(h, bsz, seq):
    d = h.shape[1]
    k = jnp.arange(seq, dtype=jnp.int32)
    ang = ((k[:, None] * k[None, :]) % seq).astype(F32) * (2.0 * math.pi / seq)
    a = jnp.concatenate([jnp.cos(ang), jnp.sin(ang)], axis=0).astype(BF16)
    pq = _const_lhs_matmul(a, h.reshape(bsz, seq, d), cn=d)
    return pq.reshape(bsz, 2, seq, d)


def _fourier_out_kernel(p_ref, q_ref, x_ref, cm_ref, wo_ref, bo_ref, mod_ref, o_ref, *, bpb, n_lat):
    d = x_ref.shape[1]
    gw = d // FNET_GROUPS
    b = _batch_row(bpb, n_lat)
    p, q = p_ref[0, 0], q_ref[0, 0]
    f = [jnp.dot(p[:, g * gw:(g + 1) * gw], cm_ref[:gw], preferred_element_type=F32)
         + jnp.dot(q[:, g * gw:(g + 1) * gw], cm_ref[gw:], preferred_element_type=F32)
         for g in range(FNET_GROUPS)]
    f = jnp.concatenate(f, axis=1).astype(BF16)
    y = jnp.dot(f, wo_ref[...], preferred_element_type=F32) + bo_ref[...]
    o_ref[...] = x_ref[...] + _mod_row(mod_ref, b, 2, d) * y


def _fourier_out(pq, x, mod, w_out, b_out, *, seq, tm, bpb, n_lat):
    t, d = x.shape
    gw = d // FNET_GROUPS
    k = jnp.arange(gw, dtype=jnp.int32)
    ang = ((k[:, None] * k[None, :]) % gw).astype(F32) * (2.0 * math.pi / gw)
    scale = 1.0 / math.sqrt(seq * gw)
    cm = (jnp.concatenate([jnp.cos(ang), -jnp.sin(ang)], axis=0) * scale).astype(BF16)
    return pl.pallas_call(
        functools.partial(_fourier_out_kernel, bpb=bpb, n_lat=n_lat),
        grid=(t // tm,),
        in_specs=[
            pl.BlockSpec((1, 1, tm, d), lambda i: (i // bpb, 0, i % bpb, 0)),
            pl.BlockSpec((1, 1, tm, d), lambda i: (i // bpb, 1, i % bpb, 0)),
            pl.BlockSpec((tm, d), lambda i: (i, 0)),
            pl.BlockSpec((2 * gw, gw), lambda i: (0, 0)),
            pl.BlockSpec((d, d), lambda i: (0, 0)),
            pl.BlockSpec((1, d), lambda i: (0, 0)),
            pl.BlockSpec(mod.shape, lambda i: (0, 0)),
        ],
        out_specs=pl.BlockSpec((tm, d), lambda i: (i, 0)),
        out_shape=jax.ShapeDtypeStruct((t, d), F32),
        compiler_params=_cparams("parallel"),
        name="fourier_out",
    )(pq, pq, x, cm, w_out.astype(BF16), b_out.reshape(1, d), mod)


def _head_perm():
    return jnp.concatenate([jnp.arange(0, HEAD_DIM, 2), jnp.arange(1, HEAD_DIM, 2)])


def _rope_tables(seq):
    rows = seq // GRID_W
    row = jnp.repeat(jnp.arange(rows, dtype=F32), GRID_W)
    col = jnp.tile(jnp.arange(GRID_W, dtype=F32), rows)
    n_pairs_axis = HEAD_DIM // 4
    inv = ROPE_THETA ** (-jnp.arange(n_pairs_axis, dtype=F32) / n_pairs_axis)
    ang = jnp.concatenate([row[:, None] * inv, col[:, None] * inv], axis=-1)
    cos, sin = jnp.cos(ang), jnp.sin(ang)
    return jnp.concatenate([cos, cos], axis=-1), jnp.concatenate([-sin, sin], axis=-1)


def _qkv_kernel(x_ref, g_ref, mod_ref, w_ref, gq_ref, gk_ref, cos_ref, sin_ref, q_ref, k_ref, v_ref, *, bpb, n_lat):
    d = x_ref.shape[1]
    b = _batch_row(bpb, n_lat)
    h = _norm_mod(x_ref[...], g_ref[...], mod_ref, b, 0, d)
    u = jnp.dot(h.astype(BF16), w_ref[...], preferred_element_type=F32)
    cos, sin = cos_ref[...], sin_ref[...]
    q_dim = q_ref.shape[1]
    kv_dim = k_ref.shape[1]

    def norm_rope(xh, gain):
        xh = _rms(xh, gain)
        return xh * cos + pltpu.roll(xh, HEAD_DIM // 2, 1) * sin

    q_scale = HEAD_DIM ** -0.5
    for hh in range(q_dim // HEAD_DIM):
        sl = slice(hh * HEAD_DIM, (hh + 1) * HEAD_DIM)
        q_ref[:, sl] = (norm_rope(u[:, sl], gq_ref[...]) * q_scale).astype(BF16)
    for hh in range(kv_dim // HEAD_DIM):
        sl = slice(hh * HEAD_DIM, (hh + 1) * HEAD_DIM)
        k_ref[:, sl] = norm_rope(u[:, q_dim + hh * HEAD_DIM:q_dim + (hh + 1) * HEAD_DIM], gk_ref[...]).astype(BF16)
    v_ref[...] = u[:, q_dim + kv_dim:].astype(BF16)


def _qkv(x, g, mod, w_qkv, gq, gk, cos, sin, *, tm, bpb, n_lat):
    t, d = x.shape
    q_dim = N_KV_HEADS * Q_PER_KV * HEAD_DIM
    kv_dim = N_KV_HEADS * HEAD_DIM
    perm = _head_perm()
    n_heads_qk = (q_dim + kv_dim) // HEAD_DIM
    cols = (jnp.arange(n_heads_qk)[:, None] * HEAD_DIM + perm[None, :]).reshape(-1)
    cols = jnp.concatenate([cols, jnp.arange(q_dim + kv_dim, q_dim + 2 * kv_dim)])
    w = w_qkv[:, cols].astype(BF16)
    n_pos = cos.shape[0] // tm
    return pl.pallas_call(
        functools.partial(_qkv_kernel, bpb=bpb, n_lat=n_lat),
        grid=(t // tm,),
        in_specs=[
            pl.BlockSpec((tm, d), lambda i: (i, 0)),
            pl.BlockSpec((1, d), lambda i: (0, 0)),
            pl.BlockSpec(mod.shape, lambda i: (0, 0)),
            pl.BlockSpec((d, q_dim + 2 * kv_dim), lambda i: (0, 0)),
            pl.BlockSpec((1, HEAD_DIM), lambda i: (0, 0)),
            pl.BlockSpec((1, HEAD_DIM), lambda i: (0, 0)),
            pl.BlockSpec((tm, HEAD_DIM), lambda i: (i % n_pos, 0)),
            pl.BlockSpec((tm, HEAD_DIM), lambda i: (i % n_pos, 0)),
        ],
        out_specs=[
            pl.BlockSpec((tm, q_dim), lambda i: (i, 0)),
            pl.BlockSpec((tm, kv_dim), lambda i: (i, 0)),
            pl.BlockSpec((tm, kv_dim), lambda i: (i, 0)),
        ],
        out_shape=[
            jax.ShapeDtypeStruct((t, q_dim), BF16),
            jax.ShapeDtypeStruct((t, kv_dim), BF16),
            jax.ShapeDtypeStruct((t, kv_dim), BF16),
        ],
        compiler_params=_cparams("parallel"),
        name="qkv_proj",
    )(x, g.reshape(1, d), mod, w, gq[perm].reshape(1, HEAD_DIM), gk[perm].reshape(1, HEAD_DIM), cos, sin)


def _attn_kernel(q_ref, k_ref, v_ref, o_ref, qs, acc, m, l, *, tq, tk, nk):
    for g in range(Q_PER_KV):
        qs[g * tq:(g + 1) * tq, :] = q_ref[0, :, g * HEAD_DIM:(g + 1) * HEAD_DIM]
    m[...] = jnp.full(m.shape, -jnp.inf, F32)
    l[...] = jnp.zeros(l.shape, F32)
    acc[...] = jnp.zeros(acc.shape, F32)

    def body(j, carry):
        k0 = pl.multiple_of(j * tk, tk)
        kk = k_ref[0, pl.ds(k0, tk), :]
        vv = v_ref[0, pl.ds(k0, tk), :]
        s = lax.dot_general(qs[...], kk, (((1,), (1,)), ((), ())), preferred_element_type=F32)
        m_prev = m[...]
        m_new = jnp.maximum(m_prev, jnp.max(s, axis=-1, keepdims=True))
        p = jnp.exp(s - m_new)
        alpha = jnp.exp(m_prev - m_new)
        l[...] = alpha * l[...] + jnp.sum(p, axis=-1, keepdims=True)
        acc[...] = alpha * acc[...] + jnp.dot(p.astype(BF16), vv, preferred_element_type=F32)
        m[...] = m_new
        return carry

    lax.fori_loop(0, nk, body, 0)
    out = acc[...] / l[...]
    for g in range(Q_PER_KV):
        o_ref[0, :, g * HEAD_DIM:(g + 1) * HEAD_DIM] = out[g * tq:(g + 1) * tq].astype(o_ref.dtype)


def _attention(q, k_all, v_all, *, tq, tk):
    bsz, seq, q_dim = q.shape
    lk = k_all.shape[1]
    gq = Q_PER_KV * HEAD_DIM
    return pl.pallas_call(
        functools.partial(_attn_kernel, tq=tq, tk=tk, nk=lk // tk),
        grid=(bsz, N_KV_HEADS, seq // tq),
        in_specs=[
            pl.BlockSpec((1, tq, gq), lambda b, h, i: (b, i, h)),
            pl.BlockSpec((1, lk, HEAD_DIM), lambda b, h, i: (b, 0, h)),
            pl.BlockSpec((1, lk, HEAD_DIM), lambda b, h, i: (b, 0, h)),
        ],
        out_specs=pl.BlockSpec((1, tq, gq), lambda b, h, i: (b, i, h)),
        out_shape=jax.ShapeDtypeStruct((bsz, seq, q_dim), BF16),
        scratch_shapes=[
            pltpu.VMEM((Q_PER_KV * tq, HEAD_DIM), BF16),
            pltpu.VMEM((Q_PER_KV * tq, HEAD_DIM), F32),
            pltpu.VMEM((Q_PER_KV * tq, 1), F32),
            pltpu.VMEM((Q_PER_KV * tq, 1), F32),
        ],
        compiler_params=_cparams("parallel", "parallel", "parallel"),
        name="flash_attention",
    )(q, k_all, v_all)


def _proj_residual_kernel(a_ref, x_ref, w_ref, mod_ref, o_ref, *, bpb, n_lat):
    d = x_ref.shape[1]
    b = _batch_row(bpb, n_lat)
    y = jnp.dot(a_ref[...], w_ref[...], preferred_element_type=F32)
    o_ref[...] = x_ref[...] + _mod_row(mod_ref, b, 2, d) * y


def _proj_residual(a, x, mod, w, *, tm, bpb, n_lat):
    t, d = x.shape
    ka = a.shape[1]
    return pl.pallas_call(
        functools.partial(_proj_residual_kernel, bpb=bpb, n_lat=n_lat),
        grid=(t // tm,),
        in_specs=[
            pl.BlockSpec((tm, ka), lambda i: (i, 0)),
            pl.BlockSpec((tm, d), lambda i: (i, 0)),
            pl.BlockSpec((ka, d), lambda i: (0, 0)),
            pl.BlockSpec(mod.shape, lambda i: (0, 0)),
        ],
        out_specs=pl.BlockSpec((tm, d), lambda i: (i, 0)),
        out_shape=jax.ShapeDtypeStruct((t, d), F32),
        compiler_params=_cparams("parallel"),
        name="proj_residual",
    )(a, x, w.astype(BF16), mod)


ROUTE_E, ROUTE_GATE, ROUTE_RANK = 0, 2, 4
ROUTER_LOGIT0 = N_EXPERT_GROUPS


def _router_kernel(x_ref, g_ref, mod_ref, wr_ref, br_ref, tri_ref, h_ref, r_ref, cnt_ref, carry, *, bpb, n_lat):
    d = x_ref.shape[1]
    tm = x_ref.shape[0]
    i = pl.program_id(0)
    b = _batch_row(bpb, n_lat)

    @pl.when(i == 0)
    def _():
        carry[...] = jnp.zeros(carry.shape, F32)

    h = _norm_mod(x_ref[...], g_ref[...], mod_ref, b, 3, d)
    h_ref[...] = h
    logits = jnp.dot(h, wr_ref[...], precision=HIGHEST, preferred_element_type=F32) + br_ref[...]
    lane = lax.broadcasted_iota(jnp.int32, (tm, LANES), 1).astype(F32)
    big = float(LANES)
    neg = -jnp.inf
    gmask = lane < N_EXPERT_GROUPS
    gl = jnp.where(gmask, logits, neg)
    ge = jnp.exp(gl - jnp.max(gl, axis=-1, keepdims=True))
    gp = ge / jnp.sum(ge, axis=-1, keepdims=True)
    g_top = jnp.max(gp, axis=-1, keepdims=True)
    g_idx = jnp.min(jnp.where(gmask & (gp == g_top), lane, big), axis=-1, keepdims=True)
    lo = ROUTER_LOGIT0 + g_idx * EXPERTS_PER_GROUP
    emask = (lane >= lo) & (lane < lo + EXPERTS_PER_GROUP)
    el = jnp.where(emask, logits, neg)
    ee = jnp.exp(el - jnp.max(el, axis=-1, keepdims=True))
    ep = ee / jnp.sum(ee, axis=-1, keepdims=True)
    p1 = jnp.max(ep, axis=-1, keepdims=True)
    i1 = jnp.min(jnp.where(emask & (ep == p1), lane, big), axis=-1, keepdims=True)
    rest = emask & (lane != i1)
    p2 = jnp.max(jnp.where(rest, ep, -1.0), axis=-1, keepdims=True)
    i2 = jnp.min(jnp.where(rest & (ep == p2), lane, big), axis=-1, keepdims=True)
    denom = p1 + p2
    gate1 = g_top * p1 / denom
    gate2 = g_top * p2 / denom
    hit1 = lane == i1
    hit2 = lane == i2
    onehot = jnp.where(hit1 | hit2, 1.0, 0.0)
    cum = jnp.dot(tri_ref[...], onehot.astype(BF16), preferred_element_type=F32) + carry[...]
    rank1 = jnp.sum(jnp.where(hit1, cum, 0.0), axis=-1, keepdims=True)
    rank2 = jnp.sum(jnp.where(hit2, cum, 0.0), axis=-1, keepdims=True)
    carry[...] = carry[...] + jnp.sum(onehot, axis=0, keepdims=True)
    cnt_ref[...] = carry[...]
    rec = jnp.zeros((tm, LANES), F32)
    for ln, val in ((ROUTE_E, i1 - ROUTER_LOGIT0), (ROUTE_E + 1, i2 - ROUTER_LOGIT0), (ROUTE_GATE, gate1),
                    (ROUTE_GATE + 1, gate2), (ROUTE_RANK, rank1), (ROUTE_RANK + 1, rank2)):
        rec = jnp.where(lane == float(ln), val, rec)
    r_ref[...] = rec


def _router(x, g, mod, w_group, b_group, w_expert, b_expert, *, tm, bpb, n_lat):
    t, d = x.shape
    n_log = N_EXPERT_GROUPS + N_EXPERTS
    wr = jnp.zeros((d, LANES), F32).at[:, :N_EXPERT_GROUPS].set(w_group).at[:, N_EXPERT_GROUPS:n_log].set(w_expert)
    br = jnp.zeros((1, LANES), F32).at[0, :N_EXPERT_GROUPS].set(b_group).at[0, N_EXPERT_GROUPS:n_log].set(b_expert)
    tri = (jnp.arange(tm)[:, None] > jnp.arange(tm)[None, :]).astype(BF16)
    return pl.pallas_call(
        functools.partial(_router_kernel, bpb=bpb, n_lat=n_lat),
        grid=(t // tm,),
        in_specs=[
            pl.BlockSpec((tm, d), lambda i: (i, 0)),
            pl.BlockSpec((1, d), lambda i: (0, 0)),
            pl.BlockSpec(mod.shape, lambda i: (0, 0)),
            pl.BlockSpec((d, LANES), lambda i: (0, 0)),
            pl.BlockSpec((1, LANES), lambda i: (0, 0)),
            pl.BlockSpec((tm, tm), lambda i: (0, 0)),
        ],
        out_specs=[
            pl.BlockSpec((tm, d), lambda i: (i, 0)),
            pl.BlockSpec((tm, LANES), lambda i: (i, 0)),
            pl.BlockSpec((1, LANES), lambda i: (0, 0)),
        ],
        out_shape=[
            jax.ShapeDtypeStruct((t, d), F32),
            jax.ShapeDtypeStruct((t, LANES), F32),
            jax.ShapeDtypeStruct((1, LANES), F32),
        ],
        scratch_shapes=[pltpu.VMEM((1, LANES), F32)],
        compiler_params=_cparams("arbitrary"),
        name="moe_router",
    )(x, g.reshape(1, d), mod, wr, br, tri)


def _moe_ffn_kernel(be_ref, bv_ref, x_ref, wg_ref, wu_ref, wd_ref, o_ref, wgb, wub, wdb):
    i = pl.program_id(0)
    e = be_ref[i]
    e_prev = be_ref[jnp.maximum(i - 1, 0)]

    @pl.when((i == 0) | (e != e_prev))
    def _():
        wgb[...] = wg_ref[0, 0].astype(BF16)
        wub[...] = wu_ref[0, 0].astype(BF16)
        wdb[...] = wd_ref[0, 0].astype(BF16)

    @pl.when(bv_ref[i] > 0)
    def _():
        xb = x_ref[...].astype(BF16)
        gt = jnp.dot(xb, wgb[...], preferred_element_type=F32)
        up = jnp.dot(xb, wub[...], preferred_element_type=F32)
        a = gt * _sigmoid(gt) * up
        o_ref[...] = jnp.dot(a.astype(BF16), wdb[...], preferred_element_type=F32)

    @pl.when(bv_ref[i] == 0)
    def _():
        o_ref[...] = jnp.zeros(o_ref.shape, F32)


def _moe_ffn(buf, blk_expert, blk_valid, w_gate, w_up, w_down, layer):
    p, d = buf.shape
    de = w_gate.shape[-1]
    n_blocks = p // MOE_BLOCK
    grid_spec = pltpu.PrefetchScalarGridSpec(
        num_scalar_prefetch=2,
        grid=(n_blocks,),
        in_specs=[
            pl.BlockSpec((MOE_BLOCK, d), lambda i, be, bv: (i, 0)),
            pl.BlockSpec((1, 1, d, de), lambda i, be, bv: (layer, be[i], 0, 0)),
            pl.BlockSpec((1, 1, d, de), lambda i, be, bv: (layer, be[i], 0, 0)),
            pl.BlockSpec((1, 1, de, d), lambda i, be, bv: (layer, be[i], 0, 0)),
        ],
        out_specs=pl.BlockSpec((MOE_BLOCK, d), lambda i, be, bv: (i, 0)),
        scratch_shapes=[pltpu.VMEM((d, de), BF16), pltpu.VMEM((d, de), BF16), pltpu.VMEM((de, d), BF16)],
    )
    return pl.pallas_call(
        _moe_ffn_kernel,
        grid_spec=grid_spec,
        out_shape=jax.ShapeDtypeStruct((p, d), F32),
        compiler_params=_cparams("arbitrary"),
        name="moe_ffn",
    )(blk_expert, blk_valid, buf, w_gate, w_up, w_down)


def _moe_combine_kernel(x_ref, y0_ref, y1_ref, r_ref, mod_ref, o_ref, *, bpb, n_lat):
    d = x_ref.shape[1]
    b = _batch_row(bpb, n_lat)
    r = r_ref[...]
    y = y0_ref[...] * r[:, ROUTE_GATE:ROUTE_GATE + 1] + y1_ref[...] * r[:, ROUTE_GATE + 1:ROUTE_GATE + 2]
    o_ref[...] = x_ref[...] + _mod_row(mod_ref, b, 5, d) * y


def _moe_combine(x, y0, y1, route, mod, *, tm, bpb, n_lat):
    t, d = x.shape
    row = pl.BlockSpec((tm, d), lambda i: (i, 0))
    return pl.pallas_call(
        functools.partial(_moe_combine_kernel, bpb=bpb, n_lat=n_lat),
        grid=(t // tm,),
        in_specs=[row, row, row, pl.BlockSpec((tm, LANES), lambda i: (i, 0)), pl.BlockSpec(mod.shape, lambda i: (0, 0))],
        out_specs=row,
        out_shape=jax.ShapeDtypeStruct((t, d), F32),
        compiler_params=_cparams("parallel"),
        name="moe_combine",
    )(x, y0, y1, route, mod)


def _hier_moe_residual(x, g, mod, w_group, b_group, w_expert, b_expert, w_gate, w_up, w_down, layer, *, tm, bpb, n_lat):
    t, d = x.shape
    h, route, counts = _router(x, g, mod, w_group, b_group, w_expert, b_expert, tm=tm, bpb=bpb, n_lat=n_lat)
    counts = counts[0, ROUTER_LOGIT0:ROUTER_LOGIT0 + N_EXPERTS].astype(jnp.int32)
    expert = route[:, ROUTE_E:ROUTE_E + 2].astype(jnp.int32)
    rank = route[:, ROUTE_RANK:ROUTE_RANK + 2].astype(jnp.int32)
    padded = (counts + MOE_BLOCK - 1) // MOE_BLOCK * MOE_BLOCK
    pad_end = jnp.cumsum(padded)
    pad_start = pad_end - padded
    dest = pad_start[expert] + rank
    n_blocks = -(-(2 * t) // MOE_BLOCK) + N_EXPERTS
    blk_first = jnp.arange(n_blocks, dtype=jnp.int32) * MOE_BLOCK
    blk_expert = jnp.minimum(jnp.searchsorted(pad_end, blk_first, side='right'), N_EXPERTS - 1).astype(jnp.int32)
    blk_valid = (blk_first < pad_end[-1]).astype(jnp.int32)
    buf = jnp.zeros((n_blocks * MOE_BLOCK, d), F32).at[dest[:, 0]].set(h).at[dest[:, 1]].set(h)
    yb = _moe_ffn(buf, blk_expert, blk_valid, w_gate, w_up, w_down, layer)
    return _moe_combine(x, yb[dest[:, 0]], yb[dest[:, 1]], route, mod, tm=tm, bpb=bpb, n_lat=n_lat)


def kernel(x, c, ctx, c_ctx, norm_g, w_mod, b_mod, conv_w_in, conv_b_in, conv_w_dw, conv_b_dw, conv_norm_g, conv_w_out, conv_b_out, fnet_w_out, fnet_b_out, attn_w_qkv, attn_q_norm_g, attn_k_norm_g, attn_w_out, moe_w_group, moe_b_group, moe_w_expert, moe_b_expert, moe_w_gate, moe_w_up, moe_w_down):
    bsz, seq, d = x.shape
    n_ctx = ctx.shape[1]
    depth = norm_g.shape[0]
    assert bsz <= MOD_ROWS // 2 and seq % DFT_NA == 0 and seq % GRID_W == 0
    t_lat, t_ctx = bsz * seq, bsz * n_ctx
    tm = min(512, seq)
    tmc = min(tm, n_ctx)
    assert seq % tm == 0 and n_ctx % tmc == 0 and tm % tmc == 0
    lat = dict(tm=tm, bpb=seq // tm, n_lat=t_lat // tm)
    cx = dict(tm=tmc, bpb=1, n_lat=0)
    both = dict(tm=tmc, bpb=seq // tmc, n_lat=t_lat // tmc)
    last_reader = max([i for i in range(depth) if i % N_MIXERS == 2], default=-1)

    cvec = jnp.zeros((MOD_ROWS, d), F32).at[:bsz].set(c).at[MOD_ROWS // 2].set(c_ctx)
    mod_all = _modulation(cvec, w_mod, b_mod)
    cos, sin = _rope_tables(seq)

    xl = x.reshape(t_lat, d)
    xc = ctx.reshape(t_ctx, d)
    for i in range(depth):
        m, j = i % N_MIXERS, i // N_MIXERS
        ctx_on = i <= last_reader
        ctx_full = i < last_reader
        mod = mod_all[i]
        g1, g2 = norm_g[i, 0], norm_g[i, 1]
        if m == 0:
            cp = (conv_w_dw[j], conv_b_dw[j], conv_norm_g[j], conv_w_out[j], conv_b_out[j])
            v = _conv_in(xl, g1, mod, conv_w_in[j], conv_b_in[j], **lat)
            xl = _conv_out(v, xl, mod, *cp, **lat)
            if ctx_full:
                vc = _conv_in(xc, g1, mod, conv_w_in[j], conv_b_in[j], **cx)
                xc = _conv_out(vc, xc, mod, *cp, tm=n_ctx, bpb=1, n_lat=0)
        elif m == 1:
            hl = _prenorm(xl, g1, mod, dtype=F32, **lat)
            pq = _seq_dft_two_stage(hl, bsz, seq)
            xl = _fourier_out(pq, xl, mod, fnet_w_out[j], fnet_b_out[j], seq=seq, **lat)
            if ctx_full:
                hc = _prenorm(xc, g1, mod, dtype=BF16, **cx)
                pqc = _seq_dft_dense(hc, bsz, n_ctx)
                xc = _fourier_out(pqc, xc, mod, fnet_w_out[j], fnet_b_out[j], seq=n_ctx, tm=n_ctx, bpb=1, n_lat=0)
        else:
            gq, gk = attn_q_norm_g[j], attn_k_norm_g[j]
            q, k, vv = _qkv(xl, g1, mod, attn_w_qkv[j], gq, gk, cos, sin, **lat)
            ones = jnp.ones((tmc, HEAD_DIM), F32)
            _, kc, vc = _qkv(xc, g1, mod, attn_w_qkv[j], gq, gk, ones, jnp.zeros_like(ones), **cx)
            kv_dim = k.shape[1]
            k_all = jnp.concatenate([kc.reshape(bsz, n_ctx, kv_dim), k.reshape(bsz, seq, kv_dim)], axis=1)
            v_all = jnp.concatenate([vc.reshape(bsz, n_ctx, kv_dim), vv.reshape(bsz, seq, kv_dim)], axis=1)
            lk = seq + n_ctx
            tk = max(t for t in (768, 512, 384, 256, 128) if lk % t == 0)
            o = _attention(q.reshape(bsz, seq, -1), k_all, v_all, tq=min(256, seq), tk=tk)
            xl = _proj_residual(o.reshape(t_lat, -1), xl, mod, attn_w_out[j], **lat)
            if ctx_full:
                raise NotImplementedError("context-stream attention output is only needed before the last reader layer")
        mp = (moe_w_group[i], moe_b_group[i], moe_w_expert[i], moe_b_expert[i], moe_w_gate, moe_w_up, moe_w_down, i)
        if ctx_full:
            xa = _hier_moe_residual(jnp.concatenate([xl, xc], axis=0), g2, mod, *mp, **both)
            xl, xc = xa[:t_lat], xa[t_lat:]
        else:
            xl = _hier_moe_residual(xl, g2, mod, *mp, **lat)
    return xl.reshape(bsz, seq, d)
```

```python
import functools
import math

import jax
import jax.numpy as jnp
from jax import lax
from jax.experimental import pallas as pl
from jax.experimental.pallas import tpu as pltpu

F32 = jnp.float32
BF16 = jnp.bfloat16
HIGHEST = lax.Precision.HIGHEST

EPS = 1e-6
GRID_W = 64
N_MIXERS = 3
CONV_WIDTH = 31
CONV_HALO = 16
CONV_ROW_CHUNK = 32
CONV_COL_CHUNK = 256
SUBLANES = 8
FNET_GROUPS = 4
HEAD_DIM = 128
N_KV_HEADS = 2
Q_PER_KV = 4
ROPE_THETA = 10000.0
N_EXPERT_GROUPS = 4
EXPERTS_PER_GROUP = 8
N_EXPERTS = N_EXPERT_GROUPS * EXPERTS_PER_GROUP
MOE_BLOCK = 256
LANES = 128
DFT_NA = 128
MOD_ROWS = 8
VMEM_LIMIT = 56 * 1024 * 1024


def _cparams(*sem):
    return pltpu.CompilerParams(dimension_semantics=sem, vmem_limit_bytes=VMEM_LIMIT)


def _rms(x, g):
    return x * lax.rsqrt(jnp.mean(x * x, axis=-1, keepdims=True) + EPS) * g


def _sigmoid(x):
    return 1.0 / (1.0 + jnp.exp(-x))


def _mod_row(mod_ref, b, k, d):
    return mod_ref[pl.ds(b, 1), pl.ds(k * d, d)]


def _norm_mod(x, g, mod_ref, b, k_shift, d):
    return _rms(x, g) * (1.0 + _mod_row(mod_ref, b, k_shift + 1, d)) + _mod_row(mod_ref, b, k_shift, d)


def _batch_row(bpb, n_lat_blocks):
    i = pl.program_id(0)
    return jnp.where(i < n_lat_blocks, i // bpb, MOD_ROWS // 2)


def _mod_kernel(c_ref, w_ref, b_ref, o_ref):
    c = c_ref[...]
    s = c * _sigmoid(c)
    o_ref[0] = jnp.dot(s, w_ref[0], precision=HIGHEST, preferred_element_type=F32) + b_ref[0]


def _modulation(cvec, w_mod, b_mod):
    depth, d, n = w_mod.shape
    tn = 1024
    return pl.pallas_call(
        _mod_kernel,
        grid=(depth, n // tn),
        in_specs=[
            pl.BlockSpec((MOD_ROWS, d), lambda l, j: (0, 0)),
            pl.BlockSpec((1, d, tn), lambda l, j: (l, 0, j)),
            pl.BlockSpec((1, 1, tn), lambda l, j: (l, 0, j)),
        ],
        out_specs=pl.BlockSpec((1, MOD_ROWS, tn), lambda l, j: (l, 0, j)),
        out_shape=jax.ShapeDtypeStruct((depth, MOD_ROWS, n), F32),
        compiler_params=_cparams("parallel", "parallel"),
        name="adaln_mod",
    )(cvec, w_mod, b_mod.reshape(depth, 1, n))


def _conv_in_kernel(x_ref, g_ref, mod_ref, w_ref, b_ref, v_ref, *, bpb, n_lat):
    d = x_ref.shape[1]
    b = _batch_row(bpb, n_lat)
    h = _norm_mod(x_ref[...], g_ref[...], mod_ref, b, 0, d)
    u = jnp.dot(h.astype(BF16), w_ref[...], preferred_element_type=F32) + b_ref[...]
    v_ref[...] = u[:, :d] * _sigmoid(u[:, d:])


def _conv_in(x, g, mod, w_in, b_in, *, tm, bpb, n_lat):
    t, d = x.shape
    return pl.pallas_call(
        functools.partial(_conv_in_kernel, bpb=bpb, n_lat=n_lat),
        grid=(t // tm,),
        in_specs=[
            pl.BlockSpec((tm, d), lambda i: (i, 0)),
            pl.BlockSpec((1, d), lambda i: (0, 0)),
            pl.BlockSpec(mod.shape, lambda i: (0, 0)),
            pl.BlockSpec((d, 2 * d), lambda i: (0, 0)),
            pl.BlockSpec((1, 2 * d), lambda i: (0, 0)),
        ],
        out_specs=pl.BlockSpec((tm, d), lambda i: (i, 0)),
        out_shape=jax.ShapeDtypeStruct((t, d), F32),
        compiler_params=_cparams("parallel"),
        name="conv_in",
    )(x, g.reshape(1, d), mod, w_in.astype(BF16), b_in.reshape(1, 2 * d))


def _conv_out_kernel(vp_ref, vc_ref, vn_ref, x_ref, wdw_ref, bdw_ref, gn_ref, wo_ref, bo_ref, mod_ref,
                     o_ref, buf, z, sh, *, bpb, n_lat, tm):
    d = x_ref.shape[1]
    i = pl.program_id(0)
    j = i % bpb
    b = _batch_row(bpb, n_lat)
    h = CONV_HALO
    buf[0:h, :] = jnp.where(j == 0, 0.0, vp_ref[...])
    buf[h:h + tm, :] = vc_ref[...]
    buf[h + tm:, :] = jnp.where(j == bpb - 1, 0.0, vn_ref[...])
    rc, cw = CONV_ROW_CHUNK, CONV_COL_CHUNK
    off = h - CONV_WIDTH // 2
    n_sh = sh.shape[1]

    def col_chunk(ci, carry):
        c0 = pl.multiple_of(ci * cw, cw)
        for s in range(SUBLANES):
            sh[s] = buf[s:s + n_sh, pl.ds(c0, cw)]
        for r in range(tm // rc):
            acc = jnp.zeros((rc, cw), F32)
            for t in range(CONV_WIDTH):
                q, s = divmod(off + t, SUBLANES)
                row0 = r * rc + q * SUBLANES
                acc = acc + sh[s, row0:row0 + rc, :] * wdw_ref[t:t + 1, pl.ds(c0, cw)]
            z[r * rc:(r + 1) * rc, pl.ds(c0, cw)] = acc
        return carry

    lax.fori_loop(0, d // cw, col_chunk, 0)
    zz = _rms(z[...] + bdw_ref[...], gn_ref[...])
    zz = zz * _sigmoid(zz)
    y = jnp.dot(zz.astype(BF16), wo_ref[...], preferred_element_type=F32) + bo_ref[...]
    o_ref[...] = x_ref[...] + _mod_row(mod_ref, b, 2, d) * y


def _conv_out(v, x, mod, w_dw, b_dw, g_norm, w_out, b_out, *, tm, bpb, n_lat):
    t, d = x.shape
    hb = tm // CONV_HALO
    n_halo = t // CONV_HALO
    wdw = jnp.zeros((32, d), F32).at[:CONV_WIDTH].set(w_dw)
    n_tap_tiles = (CONV_HALO - CONV_WIDTH // 2 + CONV_WIDTH - 1) // SUBLANES
    return pl.pallas_call(
        functools.partial(_conv_out_kernel, bpb=bpb, n_lat=n_lat, tm=tm),
        grid=(t // tm,),
        in_specs=[
            pl.BlockSpec((CONV_HALO, d), lambda i: (jnp.maximum(i * hb - 1, 0), 0)),
            pl.BlockSpec((tm, d), lambda i: (i, 0)),
            pl.BlockSpec((CONV_HALO, d), lambda i: (jnp.minimum((i + 1) * hb, n_halo - 1), 0)),
            pl.BlockSpec((tm, d), lambda i: (i, 0)),
            pl.BlockSpec((32, d), lambda i: (0, 0)),
            pl.BlockSpec((1, d), lambda i: (0, 0)),
            pl.BlockSpec((1, d), lambda i: (0, 0)),
            pl.BlockSpec((d, d), lambda i: (0, 0)),
            pl.BlockSpec((1, d), lambda i: (0, 0)),
            pl.BlockSpec(mod.shape, lambda i: (0, 0)),
        ],
        out_specs=pl.BlockSpec((tm, d), lambda i: (i, 0)),
        out_shape=jax.ShapeDtypeStruct((t, d), F32),
        scratch_shapes=[
            pltpu.VMEM((tm + 2 * CONV_HALO, d), F32),
            pltpu.VMEM((tm, d), F32),
            pltpu.VMEM((SUBLANES, tm + n_tap_tiles * SUBLANES, CONV_COL_CHUNK), F32),
        ],
        compiler_params=_cparams("parallel"),
        name="conv_out",
    )(v, v, v, x, wdw, b_dw.reshape(1, d), g_norm.reshape(1, d), w_out.astype(BF16), b_out.reshape(1, d), mod)


def _prenorm_kernel(x_ref, g_ref, mod_ref, h_ref, *, bpb, n_lat):
    d = x_ref.shape[1]
    b = _batch_row(bpb, n_lat)
    h_ref[...] = _norm_mod(x_ref[...], g_ref[...], mod_ref, b, 0, d).astype(h_ref.dtype)


def _prenorm(x, g, mod, *, tm, bpb, n_lat, dtype):
    t, d = x.shape
    return pl.pallas_call(
        functools.partial(_prenorm_kernel, bpb=bpb, n_lat=n_lat),
        grid=(t // tm,),
        in_specs=[
            pl.BlockSpec((tm, d), lambda i: (i, 0)),
            pl.BlockSpec((1, d), lambda i: (0, 0)),
            pl.BlockSpec(mod.shape, lambda i: (0, 0)),
        ],
        out_specs=pl.BlockSpec((tm, d), lambda i: (i, 0)),
        out_shape=jax.ShapeDtypeStruct((t, d), dtype),
        compiler_params=_cparams("parallel"),
        name="prenorm",
    )(x, g.reshape(1, d), mod)


def _dft1_kernel(h_ref, m_ref, o_ref, *, nb_chunk, nb, na):
    bc = pl.program_id(2)
    for bi in range(nb_chunk):
        b = bc * nb_chunk + bi
        xs = h_ref[0, pl.ds(b, na, stride=nb), :]
        t = jnp.dot(m_ref[bi], xs.astype(BF16), preferred_element_type=F32)
        o_ref[0, 0, bi] = t[:na].astype(BF16)
        o_ref[0, 1, bi] = t[na:].astype(BF16)


def _dft_tables(seq):
    na, nb = DFT_NA, seq // DFT_NA
    ka = jnp.arange(na, dtype=jnp.int32)
    a = jnp.arange(na, dtype=jnp.int32)
    b = jnp.arange(nb, dtype=jnp.int32)
    n = a[None, None, :] * nb + b[:, None, None]
    ang = ((ka[None, :, None] * n) % seq).astype(F32) * (2.0 * math.pi / seq)
    m1 = jnp.concatenate([jnp.cos(ang), -jnp.sin(ang)], axis=1).astype(BF16)
    kb = jnp.arange(nb, dtype=jnp.int32)
    ang2 = ((kb[:, None] * b[None, :]) % nb).astype(F32) * (2.0 * math.pi / nb)
    c2, s2 = jnp.cos(ang2), jnp.sin(ang2)
    m2 = jnp.concatenate([jnp.concatenate([c2, s2], axis=1),
                          jnp.concatenate([s2, -c2], axis=1)], axis=0).astype(BF16)
    return m1, m2


def _const_lhs_matmul_kernel(a_ref, x_ref, o_ref):
    o_ref[0] = jnp.dot(a_ref[...], x_ref[0], preferred_element_type=F32).astype(o_ref.dtype)


def _const_lhs_matmul(a, x, *, cn):
    m, k = a.shape
    bsz, _, n = x.shape
    return pl.pallas_call(
        _const_lhs_matmul_kernel,
        grid=(bsz, n // cn),
        in_specs=[
            pl.BlockSpec((m, k), lambda b, j: (0, 0)),
            pl.BlockSpec((1, k, cn), lambda b, j: (b, 0, j)),
        ],
        out_specs=pl.BlockSpec((1, m, cn), lambda b, j: (b, 0, j)),
        out_shape=jax.ShapeDtypeStruct((bsz, m, n), BF16),
        compiler_params=_cparams("parallel", "parallel"),
        name="const_lhs_matmul",
    )(a, x)


def _seq_dft_two_stage(h, bsz, seq):
    d = h.shape[1]
    na, nb = DFT_NA, seq // DFT_NA
    gw = LANES
    nb_chunk = min(nb, 8)
    m1, m2 = _dft_tables(seq)
    t1 = pl.pallas_call(
        functools.partial(_dft1_kernel, nb_chunk=nb_chunk, nb=nb, na=na),
        grid=(bsz, d // gw, nb // nb_chunk),
        in_specs=[
            pl.BlockSpec((1, seq, gw), lambda b, g, c: (b, 0, g)),
            pl.BlockSpec((nb_chunk, 2 * na, na), lambda b, g, c: (c, 0, 0)),
        ],
        out_specs=pl.BlockSpec((1, 2, nb_chunk, na, gw), lambda b, g, c: (b, 0, c, 0, g)),
        out_shape=jax.ShapeDtypeStruct((bsz, 2, nb, na, d), BF16),
        compiler_params=_cparams("parallel", "parallel", "arbitrary"),
        name="dft_stage1",
    )(h.reshape(bsz, seq, d), m1)
    pq = _const_lhs_matmul(m2, t1.reshape(bsz, 2 * nb, na * d), cn=min(na * d, 8192))
    return pq.reshape(bsz, 2, seq, d)


def _seq_dft_dense(h, bsz, seq):
    d = h.shape[1]
    k = jnp.arange(seq, dtype=jnp.int32)
    ang = ((k[:, None] * k[None, :]) % seq).astype(F32) * (2.0 * math.pi / seq)
    a = jnp.concatenate([jnp.cos(ang), jnp.sin(ang)], axis=0).astype(BF16)
    pq = _const_lhs_matmul(a, h.reshape(bsz, seq, d), cn=d)
    return pq.reshape(bsz, 2, seq, d)


def _fourier_out_kernel(p_ref, q_ref, x_ref, cm_ref, wo_ref, bo_ref, mod_ref, o_ref, *, bpb, n_lat):
    d = x_ref.shape[1]
    gw = d // FNET_GROUPS
    b = _batch_row(bpb, n_lat)
    p, q = p_ref[0, 0], q_ref[0, 0]
    f = [jnp.dot(p[:, g * gw:(g + 1) * gw], cm_ref[:gw], preferred_element_type=F32)
         + jnp.dot(q[:, g * gw:(g + 1) * gw], cm_ref[gw:], preferred_element_type=F32)
         for g in range(FNET_GROUPS)]
    f = jnp.concatenate(f, axis=1).astype(BF16)
    y = jnp.dot(f, wo_ref[...], preferred_element_type=F32) + bo_ref[...]
    o_ref[...] = x_ref[...] + _mod_row(mod_ref, b, 2, d) * y


def _fourier_out(pq, x, mod, w_out, b_out, *, seq, tm, bpb, n_lat):
    t, d = x.shape
    gw = d // FNET_GROUPS
    k = jnp.arange(gw, dtype=jnp.int32)
    ang = ((k[:, None] * k[None, :]) % gw).astype(F32) * (2.0 * math.pi / gw)
    scale = 1.0 / math.sqrt(seq * gw)
    cm = (jnp.concatenate([jnp.cos(ang), -jnp.sin(ang)], axis=0) * scale).astype(BF16)
    return pl.pallas_call(
        functools.partial(_fourier_out_kernel, bpb=bpb, n_lat=n_lat),
        grid=(t // tm,),
        in_specs=[
            pl.BlockSpec((1, 1, tm, d), lambda i: (i // bpb, 0, i % bpb, 0)),
            pl.BlockSpec((1, 1, tm, d), lambda i: (i // bpb, 1, i % bpb, 0)),
            pl.BlockSpec((tm, d), lambda i: (i, 0)),
            pl.BlockSpec((2 * gw, gw), lambda i: (0, 0)),
            pl.BlockSpec((d, d), lambda i: (0, 0)),
            pl.BlockSpec((1, d), lambda i: (0, 0)),
            pl.BlockSpec(mod.shape, lambda i: (0, 0)),
        ],
        out_specs=pl.BlockSpec((tm, d), lambda i: (i, 0)),
        out_shape=jax.ShapeDtypeStruct((t, d), F32),
        compiler_params=_cparams("parallel"),
        name="fourier_out",
    )(pq, pq, x, cm, w_out.astype(BF16), b_out.reshape(1, d), mod)


def _head_perm():
    return jnp.concatenate([jnp.arange(0, HEAD_DIM, 2), jnp.arange(1, HEAD_DIM, 2)])


def _rope_tables(seq):
    rows = seq // GRID_W
    row = jnp.repeat(jnp.arange(rows, dtype=F32), GRID_W)
    col = jnp.tile(jnp.arange(GRID_W, dtype=F32), rows)
    n_pairs_axis = HEAD_DIM // 4
    inv = ROPE_THETA ** (-jnp.arange(n_pairs_axis, dtype=F32) / n_pairs_axis)
    ang = jnp.concatenate([row[:, None] * inv, col[:, None] * inv], axis=-1)
    cos, sin = jnp.cos(ang), jnp.sin(ang)
    return jnp.concatenate([cos, cos], axis=-1), jnp.concatenate([-sin, sin], axis=-1)


def _qkv_kernel(x_ref, g_ref, mod_ref, w_ref, gq_ref, gk_ref, cos_ref, sin_ref, q_ref, k_ref, v_ref, *, bpb, n_lat):
    d = x_ref.shape[1]
    b = _batch_row(bpb, n_lat)
    h = _norm_mod(x_ref[...], g_ref[...], mod_ref, b, 0, d)
    u = jnp.dot(h.astype(BF16), w_ref[...], preferred_element_type=F32)
    cos, sin = cos_ref[...], sin_ref[...]
    q_dim = q_ref.shape[1]
    kv_dim = k_ref.shape[1]

    def norm_rope(xh, gain):
        xh = _rms(xh, gain)
        return xh * cos + pltpu.roll(xh, HEAD_DIM // 2, 1) * sin

    q_scale = HEAD_DIM ** -0.5 * math.log2(math.e)
    for hh in range(q_dim // HEAD_DIM):
        sl = slice(hh * HEAD_DIM, (hh + 1) * HEAD_DIM)
        q_ref[:, sl] = (norm_rope(u[:, sl], gq_ref[...]) * q_scale).astype(BF16)
    for hh in range(kv_dim // HEAD_DIM):
        sl = slice(hh * HEAD_DIM, (hh + 1) * HEAD_DIM)
        k_ref[:, sl] = norm_rope(u[:, q_dim + hh * HEAD_DIM:q_dim + (hh + 1) * HEAD_DIM], gk_ref[...]).astype(BF16)
    v_ref[...] = u[:, q_dim + kv_dim:].astype(BF16)


def _qkv(x, g, mod, w_qkv, gq, gk, cos, sin, *, tm, bpb, n_lat):
    t, d = x.shape
    q_dim = N_KV_HEADS * Q_PER_KV * HEAD_DIM
    kv_dim = N_KV_HEADS * HEAD_DIM
    perm = _head_perm()
    n_heads_qk = (q_dim + kv_dim) // HEAD_DIM
    cols = (jnp.arange(n_heads_qk)[:, None] * HEAD_DIM + perm[None, :]).reshape(-1)
    cols = jnp.concatenate([cols, jnp.arange(q_dim + kv_dim, q_dim + 2 * kv_dim)])
    w = w_qkv[:, cols].astype(BF16)
    n_pos = cos.shape[0] // tm
    return pl.pallas_call(
        functools.partial(_qkv_kernel, bpb=bpb, n_lat=n_lat),
        grid=(t // tm,),
        in_specs=[
            pl.BlockSpec((tm, d), lambda i: (i, 0)),
            pl.BlockSpec((1, d), lambda i: (0, 0)),
            pl.BlockSpec(mod.shape, lambda i: (0, 0)),
            pl.BlockSpec((d, q_dim + 2 * kv_dim), lambda i: (0, 0)),
            pl.BlockSpec((1, HEAD_DIM), lambda i: (0, 0)),
            pl.BlockSpec((1, HEAD_DIM), lambda i: (0, 0)),
            pl.BlockSpec((tm, HEAD_DIM), lambda i: (i % n_pos, 0)),
            pl.BlockSpec((tm, HEAD_DIM), lambda i: (i % n_pos, 0)),
        ],
        out_specs=[
            pl.BlockSpec((tm, q_dim), lambda i: (i, 0)),
            pl.BlockSpec((tm, kv_dim), lambda i: (i, 0)),
            pl.BlockSpec((tm, kv_dim), lambda i: (i, 0)),
        ],
        out_shape=[
            jax.ShapeDtypeStruct((t, q_dim), BF16),
            jax.ShapeDtypeStruct((t, kv_dim), BF16),
            jax.ShapeDtypeStruct((t, kv_dim), BF16),
        ],
        compiler_params=_cparams("parallel"),
        name="qkv_proj",
    )(x, g.reshape(1, d), mod, w, gq[perm].reshape(1, HEAD_DIM), gk[perm].reshape(1, HEAD_DIM), cos, sin)


ATTN_STREAMS = 1
ATTN_MAX_TK = 1408


def _attn_kernel(qt_ref, k_ref, vt_ref, o_ref, *scratch, tq, tk, nk):
    ns = ATTN_STREAMS
    s_refs, p_refs, acc_refs = scratch[:ns], scratch[ns:2 * ns], scratch[2 * ns:]
    nq = Q_PER_KV * tq
    w = nq // ns
    for acc in acc_refs:
        acc[...] = jnp.zeros(acc.shape, F32)

    def kv_chunk(j, carry):
        m_all, l_all = carry
        kk = k_ref[0, pl.ds(pl.multiple_of(j * tk, tk), tk), :]
        vt = vt_ref[0, 0, j]
        for st in range(ns):
            s_refs[st][...] = jnp.dot(kk, qt_ref[0, 0, 0, :, st * w:(st + 1) * w], preferred_element_type=F32)
        m_parts, l_parts = [], []
        for st in range(ns):
            alpha_parts = []
            for c in range(w // LANES):
                cs = slice(c * LANES, (c + 1) * LANES)
                gs = slice(st * w + c * LANES, st * w + (c + 1) * LANES)
                m_prev = m_all[:, gs]
                m_new = jnp.maximum(m_prev, jnp.max(s_refs[st][:, cs], axis=0, keepdims=True))
                p = jnp.exp2(s_refs[st][:, cs] - m_new)
                alpha = jnp.exp2(m_prev - m_new)
                p_refs[st][:, cs] = p.astype(BF16)
                m_parts.append(m_new)
                l_parts.append(alpha * l_all[:, gs] + jnp.sum(p, axis=0, keepdims=True))
                alpha_parts.append(alpha)
            alpha_st = jnp.concatenate(alpha_parts, axis=1)
            acc_refs[st][...] = alpha_st * acc_refs[st][...] + jnp.dot(vt, p_refs[st][...], preferred_element_type=F32)
        return jnp.concatenate(m_parts, axis=1), jnp.concatenate(l_parts, axis=1)

    init = (jnp.full((1, nq), -jnp.inf, F32), jnp.zeros((1, nq), F32))
    _, l_all = lax.fori_loop(0, nk, kv_chunk, init)
    heads_per_stream = w // tq
    for st in range(ns):
        out = (acc_refs[st][...] / l_all[:, st * w:(st + 1) * w]).T
        for gl in range(heads_per_stream):
            g = st * heads_per_stream + gl
            o_ref[0, :, g * HEAD_DIM:(g + 1) * HEAD_DIM] = out[gl * tq:(gl + 1) * tq].astype(o_ref.dtype)


def _attention(q, k_all, v_all, *, tq, tk):
    bsz, seq, q_dim = q.shape
    lk = k_all.shape[1]
    gq = Q_PER_KV * HEAD_DIM
    nq = Q_PER_KV * tq
    nk = lk // tk
    w = nq // ATTN_STREAMS
    assert w % tq == 0 and w % LANES == 0
    qt = q.reshape(bsz, seq // tq, tq, N_KV_HEADS, Q_PER_KV, HEAD_DIM).transpose(0, 3, 1, 5, 4, 2)
    qt = qt.reshape(bsz, N_KV_HEADS, seq // tq, HEAD_DIM, nq)
    vt = v_all.reshape(bsz, nk, tk, N_KV_HEADS, HEAD_DIM).transpose(0, 3, 1, 4, 2)
    return pl.pallas_call(
        functools.partial(_attn_kernel, tq=tq, tk=tk, nk=nk),
        grid=(bsz, N_KV_HEADS, seq // tq),
        in_specs=[
            pl.BlockSpec((1, 1, 1, HEAD_DIM, nq), lambda b, h, i: (b, h, i, 0, 0)),
            pl.BlockSpec((1, lk, HEAD_DIM), lambda b, h, i: (b, 0, h)),
            pl.BlockSpec((1, 1, nk, HEAD_DIM, tk), lambda b, h, i: (b, h, 0, 0, 0)),
        ],
        out_specs=pl.BlockSpec((1, tq, gq), lambda b, h, i: (b, i, h)),
        out_shape=jax.ShapeDtypeStruct((bsz, seq, q_dim), BF16),
        scratch_shapes=(
            [pltpu.VMEM((tk, w), F32) for _ in range(ATTN_STREAMS)]
            + [pltpu.VMEM((tk, w), BF16) for _ in range(ATTN_STREAMS)]
            + [pltpu.VMEM((HEAD_DIM, w), F32) for _ in range(ATTN_STREAMS)]
        ),
        compiler_params=_cparams("parallel", "parallel", "parallel"),
        name="flash_attention",
    )(qt, k_all, vt)


def _proj_residual_kernel(a_ref, x_ref, w_ref, mod_ref, o_ref, *, bpb, n_lat):
    d = x_ref.shape[1]
    b = _batch_row(bpb, n_lat)
    y = jnp.dot(a_ref[...], w_ref[...], preferred_element_type=F32)
    o_ref[...] = x_ref[...] + _mod_row(mod_ref, b, 2, d) * y


def _proj_residual(a, x, mod, w, *, tm, bpb, n_lat):
    t, d = x.shape
    ka = a.shape[1]
    return pl.pallas_call(
        functools.partial(_proj_residual_kernel, bpb=bpb, n_lat=n_lat),
        grid=(t // tm,),
        in_specs=[
            pl.BlockSpec((tm, ka), lambda i: (i, 0)),
            pl.BlockSpec((tm, d), lambda i: (i, 0)),
            pl.BlockSpec((ka, d), lambda i: (0, 0)),
            pl.BlockSpec(mod.shape, lambda i: (0, 0)),
        ],
        out_specs=pl.BlockSpec((tm, d), lambda i: (i, 0)),
        out_shape=jax.ShapeDtypeStruct((t, d), F32),
        compiler_params=_cparams("parallel"),
        name="proj_residual",
    )(a, x, w.astype(BF16), mod)


ROUTE_E, ROUTE_GATE, ROUTE_RANK = 0, 2, 4
ROUTER_LOGIT0 = N_EXPERT_GROUPS


def _router_kernel(x_ref, g_ref, mod_ref, wr_ref, br_ref, tri_ref, h_ref, r_ref, cnt_ref, carry, *, bpb, n_lat):
    d = x_ref.shape[1]
    tm = x_ref.shape[0]
    i = pl.program_id(0)
    b = _batch_row(bpb, n_lat)

    @pl.when(i == 0)
    def _():
        carry[...] = jnp.zeros(carry.shape, F32)

    h = _norm_mod(x_ref[...], g_ref[...], mod_ref, b, 3, d)
    h_ref[...] = h
    logits = jnp.dot(h, wr_ref[...], precision=HIGHEST, preferred_element_type=F32) + br_ref[...]
    lane = lax.broadcasted_iota(jnp.int32, (tm, LANES), 1).astype(F32)
    big = float(LANES)
    neg = -jnp.inf
    gmask = lane < N_EXPERT_GROUPS
    gl = jnp.where(gmask, logits, neg)
    ge = jnp.exp(gl - jnp.max(gl, axis=-1, keepdims=True))
    gp = ge / jnp.sum(ge, axis=-1, keepdims=True)
    g_top = jnp.max(gp, axis=-1, keepdims=True)
    g_idx = jnp.min(jnp.where(gmask & (gp == g_top), lane, big), axis=-1, keepdims=True)
    lo = ROUTER_LOGIT0 + g_idx * EXPERTS_PER_GROUP
    emask = (lane >= lo) & (lane < lo + EXPERTS_PER_GROUP)
    el = jnp.where(emask, logits, neg)
    ee = jnp.exp(el - jnp.max(el, axis=-1, keepdims=True))
    ep = ee / jnp.sum(ee, axis=-1, keepdims=True)
    p1 = jnp.max(ep, axis=-1, keepdims=True)
    i1 = jnp.min(jnp.where(emask & (ep == p1), lane, big), axis=-1, keepdims=True)
    rest = emask & (lane != i1)
    p2 = jnp.max(jnp.where(rest, ep, -1.0), axis=-1, keepdims=True)
    i2 = jnp.min(jnp.where(rest & (ep == p2), lane, big), axis=-1, keepdims=True)
    denom = p1 + p2
    gate1 = g_top * p1 / denom
    gate2 = g_top * p2 / denom
    hit1 = lane == i1
    hit2 = lane == i2
    onehot = jnp.where(hit1 | hit2, 1.0, 0.0)
    cum = jnp.dot(tri_ref[...], onehot.astype(BF16), preferred_element_type=F32) + carry[...]
    rank1 = jnp.sum(jnp.where(hit1, cum, 0.0), axis=-1, keepdims=True)
    rank2 = jnp.sum(jnp.where(hit2, cum, 0.0), axis=-1, keepdims=True)
    carry[...] = carry[...] + jnp.sum(onehot, axis=0, keepdims=True)
    cnt_ref[...] = carry[...]
    rec = jnp.zeros((tm, LANES), F32)
    for ln, val in ((ROUTE_E, i1 - ROUTER_LOGIT0), (ROUTE_E + 1, i2 - ROUTER_LOGIT0), (ROUTE_GATE, gate1),
                    (ROUTE_GATE + 1, gate2), (ROUTE_RANK, rank1), (ROUTE_RANK + 1, rank2)):
        rec = jnp.where(lane == float(ln), val, rec)
    r_ref[...] = rec


def _router(x, g, mod, w_group, b_group, w_expert, b_expert, *, tm, bpb, n_lat):
    t, d = x.shape
    n_log = N_EXPERT_GROUPS + N_EXPERTS
    wr = jnp.zeros((d, LANES), F32).at[:, :N_EXPERT_GROUPS].set(w_group).at[:, N_EXPERT_GROUPS:n_log].set(w_expert)
    br = jnp.zeros((1, LANES), F32).at[0, :N_EXPERT_GROUPS].set(b_group).at[0, N_EXPERT_GROUPS:n_log].set(b_expert)
    tri = (jnp.arange(tm)[:, None] > jnp.arange(tm)[None, :]).astype(BF16)
    return pl.pallas_call(
        functools.partial(_router_kernel, bpb=bpb, n_lat=n_lat),
        grid=(t // tm,),
        in_specs=[
            pl.BlockSpec((tm, d), lambda i: (i, 0)),
            pl.BlockSpec((1, d), lambda i: (0, 0)),
            pl.BlockSpec(mod.shape, lambda i: (0, 0)),
            pl.BlockSpec((d, LANES), lambda i: (0, 0)),
            pl.BlockSpec((1, LANES), lambda i: (0, 0)),
            pl.BlockSpec((tm, tm), lambda i: (0, 0)),
        ],
        out_specs=[
            pl.BlockSpec((tm, d), lambda i: (i, 0)),
            pl.BlockSpec((tm, LANES), lambda i: (i, 0)),
            pl.BlockSpec((1, LANES), lambda i: (0, 0)),
        ],
        out_shape=[
            jax.ShapeDtypeStruct((t, d), F32),
            jax.ShapeDtypeStruct((t, LANES), F32),
            jax.ShapeDtypeStruct((1, LANES), F32),
        ],
        scratch_shapes=[pltpu.VMEM((1, LANES), F32)],
        compiler_params=_cparams("arbitrary"),
        name="moe_router",
    )(x, g.reshape(1, d), mod, wr, br, tri)


def _moe_ffn_kernel(be_ref, bv_ref, x_ref, wg_ref, wu_ref, wd_ref, o_ref, wgb, wub, wdb):
    i = pl.program_id(0)
    e = be_ref[i]
    e_prev = be_ref[jnp.maximum(i - 1, 0)]

    @pl.when((i == 0) | (e != e_prev))
    def _():
        wgb[...] = wg_ref[0, 0].astype(BF16)
        wub[...] = wu_ref[0, 0].astype(BF16)
        wdb[...] = wd_ref[0, 0].astype(BF16)

    n_valid = bv_ref[i]

    @pl.when(n_valid > 0)
    def _():
        xb = x_ref[...].astype(BF16)
        gt = jnp.dot(xb, wgb[...], preferred_element_type=F32)
        up = jnp.dot(xb, wub[...], preferred_element_type=F32)
        a = gt * _sigmoid(gt) * up
        o_ref[...] = jnp.dot(a.astype(BF16), wdb[...], preferred_element_type=F32)

    @pl.when(n_valid <= 0)
    def _():
        o_ref[...] = jnp.zeros(o_ref.shape, F32)


def _moe_ffn(buf, blk_expert, blk_valid, w_gate, w_up, w_down, layer):
    p, d = buf.shape
    de = w_gate.shape[-1]
    n_blocks = p // MOE_BLOCK
    grid_spec = pltpu.PrefetchScalarGridSpec(
        num_scalar_prefetch=2,
        grid=(n_blocks,),
        in_specs=[
            pl.BlockSpec((MOE_BLOCK, d), lambda i, be, bv: (i, 0)),
            pl.BlockSpec((1, 1, d, de), lambda i, be, bv: (layer, be[i], 0, 0)),
            pl.BlockSpec((1, 1, d, de), lambda i, be, bv: (layer, be[i], 0, 0)),
            pl.BlockSpec((1, 1, de, d), lambda i, be, bv: (layer, be[i], 0, 0)),
        ],
        out_specs=pl.BlockSpec((MOE_BLOCK, d), lambda i, be, bv: (i, 0)),
        scratch_shapes=[pltpu.VMEM((d, de), BF16), pltpu.VMEM((d, de), BF16), pltpu.VMEM((de, d), BF16)],
    )
    return pl.pallas_call(
        _moe_ffn_kernel,
        grid_spec=grid_spec,
        out_shape=jax.ShapeDtypeStruct((p, d), F32),
        compiler_params=_cparams("arbitrary"),
        name="moe_ffn",
    )(blk_expert, blk_valid, buf, w_gate, w_up, w_down)


DMA_ISSUE_UNROLL = 8


def _row_copy(src, src_row, dst, dst_row, sem):
    return pltpu.make_async_copy(src.at[pl.ds(src_row, 1)], dst.at[pl.ds(dst_row, 1)], sem)


def _moe_dispatch_kernel(pe_ref, dest_ref, h_ref, buf_ref, zero_ref, sems, zsem):
    tm = h_ref.shape[0]
    n_blocks = buf_ref.shape[0] // MOE_BLOCK

    def zero_block(first_row):
        return pltpu.make_async_copy(zero_ref, buf_ref.at[pl.ds(pl.multiple_of(first_row, MOE_BLOCK), MOE_BLOCK)], zsem)

    @pl.when(pl.program_id(0) == 0)
    def _():
        zero_ref[...] = jnp.zeros(zero_ref.shape, F32)

        def expert_has_rows(e):
            return pe_ref[e] > jnp.where(e > 0, pe_ref[jnp.maximum(e - 1, 0)], 0)

        def fill(start):
            def body(e, carry):
                @pl.when(expert_has_rows(e))
                def _():
                    cp = zero_block(pe_ref[e] - MOE_BLOCK)
                    cp.start() if start else cp.wait()
                return carry
            return body

        def tail(start):
            def body(b, carry):
                cp = zero_block(b * MOE_BLOCK)
                cp.start() if start else cp.wait()
                return carry
            return body

        first_unused = pe_ref[N_EXPERTS - 1] // MOE_BLOCK
        for start in (True, False):
            lax.fori_loop(0, N_EXPERTS, fill(start), 0)
            lax.fori_loop(first_unused, n_blocks, tail(start), 0)

    def issue(r, carry):
        for j in range(2):
            _row_copy(h_ref, r, buf_ref, dest_ref[0, 0, 2 * r + j], sems.at[j]).start()
        return carry

    lax.fori_loop(0, tm, issue, 0, unroll=DMA_ISSUE_UNROLL)
    for j in range(2):
        pltpu.make_async_copy(h_ref, buf_ref.at[pl.ds(0, tm)], sems.at[j]).wait()


def _moe_dispatch(h, dest, pad_end, n_slots, *, tm):
    t, d = h.shape
    grid_spec = pltpu.PrefetchScalarGridSpec(
        num_scalar_prefetch=1,
        grid=(t // tm,),
        in_specs=[
            pl.BlockSpec((1, 1, 2 * tm), lambda i, pe: (i, 0, 0), memory_space=pltpu.SMEM),
            pl.BlockSpec((tm, d), lambda i, pe: (i, 0)),
        ],
        out_specs=pl.BlockSpec(memory_space=pl.ANY),
        scratch_shapes=[pltpu.VMEM((MOE_BLOCK, d), F32), pltpu.SemaphoreType.DMA((2,)), pltpu.SemaphoreType.DMA(())],
    )
    return pl.pallas_call(
        _moe_dispatch_kernel,
        grid_spec=grid_spec,
        out_shape=jax.ShapeDtypeStruct((n_slots, d), F32),
        compiler_params=_cparams("arbitrary"),
        name="moe_dispatch",
    )(pad_end.astype(jnp.int32), dest.reshape(t // tm, 1, 2 * tm), h)


def _moe_combine_kernel(dest_ref, x_ref, r_ref, mod_ref, yb_ref, o_ref, y0, y1, sems, *, bpb, n_lat):
    tm, d = x_ref.shape
    b = _batch_row(bpb, n_lat)
    ys = (y0, y1)

    def issue(r, carry):
        for j in range(2):
            _row_copy(yb_ref, dest_ref[0, 0, 2 * r + j], ys[j], r, sems.at[j]).start()
        return carry

    lax.fori_loop(0, tm, issue, 0, unroll=DMA_ISSUE_UNROLL)
    for j in range(2):
        pltpu.make_async_copy(yb_ref.at[pl.ds(0, tm)], ys[j], sems.at[j]).wait()
    r = r_ref[...]
    y = y0[...] * r[:, ROUTE_GATE:ROUTE_GATE + 1] + y1[...] * r[:, ROUTE_GATE + 1:ROUTE_GATE + 2]
    o_ref[...] = x_ref[...] + _mod_row(mod_ref, b, 5, d) * y


def _moe_combine(x, yb, dest, route, mod, *, tm, bpb, n_lat):
    t, d = x.shape
    row = pl.BlockSpec((tm, d), lambda i: (i, 0))
    return pl.pallas_call(
        functools.partial(_moe_combine_kernel, bpb=bpb, n_lat=n_lat),
        grid=(t // tm,),
        in_specs=[
            pl.BlockSpec((1, 1, 2 * tm), lambda i: (i, 0, 0), memory_space=pltpu.SMEM),
            row,
            pl.BlockSpec((tm, LANES), lambda i: (i, 0)),
            pl.BlockSpec(mod.shape, lambda i: (0, 0)),
            pl.BlockSpec(memory_space=pl.ANY),
        ],
        out_specs=row,
        out_shape=jax.ShapeDtypeStruct((t, d), F32),
        scratch_shapes=[pltpu.VMEM((tm, d), F32), pltpu.VMEM((tm, d), F32), pltpu.SemaphoreType.DMA((2,))],
        compiler_params=_cparams("arbitrary"),
        name="moe_combine",
    )(dest.reshape(t // tm, 1, 2 * tm), x, route, mod, yb)


def _hier_moe_residual(x, g, mod, w_group, b_group, w_expert, b_expert, w_gate, w_up, w_down, layer, *, tm, bpb, n_lat):
    t, d = x.shape
    h, route, counts = _router(x, g, mod, w_group, b_group, w_expert, b_expert, tm=tm, bpb=bpb, n_lat=n_lat)
    counts = counts[0, ROUTER_LOGIT0:ROUTER_LOGIT0 + N_EXPERTS].astype(jnp.int32)
    expert = route[:, ROUTE_E:ROUTE_E + 2].astype(jnp.int32)
    rank = route[:, ROUTE_RANK:ROUTE_RANK + 2].astype(jnp.int32)
    padded = (counts + MOE_BLOCK - 1) // MOE_BLOCK * MOE_BLOCK
    pad_end = jnp.cumsum(padded)
    pad_start = pad_end - padded
    dest = jnp.sum(jnp.where(expert[:, :, None] == jnp.arange(N_EXPERTS), pad_start, 0), axis=-1) + rank
    n_blocks = -(-(2 * t) // MOE_BLOCK) + N_EXPERTS
    blk_first = jnp.arange(n_blocks, dtype=jnp.int32) * MOE_BLOCK
    blk_expert = jnp.minimum(jnp.sum(blk_first[:, None] >= pad_end[None, :], axis=1), N_EXPERTS - 1).astype(jnp.int32)
    blk_rows = jnp.clip((pad_start + counts)[blk_expert] - blk_first, 0, MOE_BLOCK).astype(jnp.int32)
    buf = _moe_dispatch(h, dest, pad_end, n_blocks * MOE_BLOCK, tm=tm)
    yb = _moe_ffn(buf, blk_expert, blk_rows, w_gate, w_up, w_down, layer)
    return _moe_combine(x, yb, dest, route, mod, tm=tm, bpb=bpb, n_lat=n_lat)


def kernel(x, c, ctx, c_ctx, norm_g, w_mod, b_mod, conv_w_in, conv_b_in, conv_w_dw, conv_b_dw, conv_norm_g, conv_w_out, conv_b_out, fnet_w_out, fnet_b_out, attn_w_qkv, attn_q_norm_g, attn_k_norm_g, attn_w_out, moe_w_group, moe_b_group, moe_w_expert, moe_b_expert, moe_w_gate, moe_w_up, moe_w_down):
    bsz, seq, d = x.shape
    n_ctx = ctx.shape[1]
    depth = norm_g.shape[0]
    assert bsz <= MOD_ROWS // 2 and seq % DFT_NA == 0 and seq % GRID_W == 0
    t_lat, t_ctx = bsz * seq, bsz * n_ctx
    tm = min(512, seq)
    tmc = min(tm, n_ctx)
    assert seq % tm == 0 and n_ctx % tmc == 0 and tm % tmc == 0
    lat = dict(tm=tm, bpb=seq // tm, n_lat=t_lat // tm)
    cx = dict(tm=tmc, bpb=1, n_lat=0)
    tmb = tm if t_ctx % tm == 0 else tmc
    both = dict(tm=tmb, bpb=seq // tmb, n_lat=t_lat // tmb)
    last_reader = max([i for i in range(depth) if i % N_MIXERS == 2], default=-1)

    cvec = jnp.zeros((MOD_ROWS, d), F32).at[:bsz].set(c).at[MOD_ROWS // 2].set(c_ctx)
    mod_all = _modulation(cvec, w_mod, b_mod)
    cos, sin = _rope_tables(seq)

    xl = x.reshape(t_lat, d)
    xc = ctx.reshape(t_ctx, d)
    for i in range(depth):
        m, j = i % N_MIXERS, i // N_MIXERS
        ctx_on = i <= last_reader
        ctx_full = i < last_reader
        mod = mod_all[i]
        g1, g2 = norm_g[i, 0], norm_g[i, 1]
        if m == 0:
            cp = (conv_w_dw[j], conv_b_dw[j], conv_norm_g[j], conv_w_out[j], conv_b_out[j])
            v = _conv_in(xl, g1, mod, conv_w_in[j], conv_b_in[j], **lat)
            xl = _conv_out(v, xl, mod, *cp, **lat)
            if ctx_full:
                vc = _conv_in(xc, g1, mod, conv_w_in[j], conv_b_in[j], **cx)
                xc = _conv_out(vc, xc, mod, *cp, tm=n_ctx, bpb=1, n_lat=0)
        elif m == 1:
            hl = _prenorm(xl, g1, mod, dtype=F32, **lat)
            pq = _seq_dft_two_stage(hl, bsz, seq)
            xl = _fourier_out(pq, xl, mod, fnet_w_out[j], fnet_b_out[j], seq=seq, **lat)
            if ctx_full:
                hc = _prenorm(xc, g1, mod, dtype=BF16, **cx)
                pqc = _seq_dft_dense(hc, bsz, n_ctx)
                xc = _fourier_out(pqc, xc, mod, fnet_w_out[j], fnet_b_out[j], seq=n_ctx, tm=n_ctx, bpb=1, n_lat=0)
        else:
            gq, gk = attn_q_norm_g[j], attn_k_norm_g[j]
            q, k, vv = _qkv(xl, g1, mod, attn_w_qkv[j], gq, gk, cos, sin, **lat)
            ones = jnp.ones((tmc, HEAD_DIM), F32)
            _, kc, vc = _qkv(xc, g1, mod, attn_w_qkv[j], gq, gk, ones, jnp.zeros_like(ones), **cx)
            kv_dim = k.shape[1]
            k_all = jnp.concatenate([kc.reshape(bsz, n_ctx, kv_dim), k.reshape(bsz, seq, kv_dim)], axis=1)
            v_all = jnp.concatenate([vc.reshape(bsz, n_ctx, kv_dim), vv.reshape(bsz, seq, kv_dim)], axis=1)
            lk = seq + n_ctx
            tk = max(t for t in range(LANES, ATTN_MAX_TK + 1, LANES) if lk % t == 0)
            o = _attention(q.reshape(bsz, seq, -1), k_all, v_all, tq=min(256, seq), tk=tk)
            xl = _proj_residual(o.reshape(t_lat, -1), xl, mod, attn_w_out[j], **lat)
            if ctx_full:
                raise NotImplementedError("context-stream attention output is only needed before the last reader layer")
        mp = (moe_w_group[i], moe_b_group[i], moe_w_expert[i], moe_b_expert[i], moe_w_gate, moe_w_up, moe_w_down, i)
        if ctx_full:
            xa = _hier_moe_residual(jnp.concatenate([xl, xc], axis=0), g2, mod, *mp, **both)
            xl, xc = xa[:t_lat], xa[t_lat:]
        else:
            xl = _hier_moe_residual(xl, g2, mod, *mp, **lat)
    return xl.reshape(bsz, seq, d)
```

```python
import functools
import math

import jax
import jax.numpy as jnp
from jax import lax
from jax.experimental import pallas as pl
from jax.experimental.pallas import tpu as pltpu

F32 = jnp.float32
BF16 = jnp.bfloat16
HIGHEST = lax.Precision.HIGHEST

EPS = 1e-6
GRID_W = 64
N_MIXERS = 3
CONV_WIDTH = 31
CONV_HALO = 16
CONV_ROW_CHUNK = 32
CONV_COL_CHUNK = 256
SUBLANES = 8
FNET_GROUPS = 4
HEAD_DIM = 128
N_KV_HEADS = 2
Q_PER_KV = 4
ROPE_THETA = 10000.0
N_EXPERT_GROUPS = 4
EXPERTS_PER_GROUP = 8
N_EXPERTS = N_EXPERT_GROUPS * EXPERTS_PER_GROUP
MOE_BLOCK = 256
LANES = 128
DFT_NA = 128
MOD_ROWS = 8
VMEM_LIMIT = 56 * 1024 * 1024


def _cparams(*sem):
    return pltpu.CompilerParams(dimension_semantics=sem, vmem_limit_bytes=VMEM_LIMIT)


def _rms(x, g):
    return x * lax.rsqrt(jnp.mean(x * x, axis=-1, keepdims=True) + EPS) * g


def _sigmoid(x):
    return 1.0 / (1.0 + jnp.exp(-x))


def _mod_row(mod_ref, b, k, d):
    return mod_ref[pl.ds(b, 1), pl.ds(k * d, d)]


def _norm_mod(x, g, mod_ref, b, k_shift, d):
    return _rms(x, g) * (1.0 + _mod_row(mod_ref, b, k_shift + 1, d)) + _mod_row(mod_ref, b, k_shift, d)


def _batch_row(bpb, n_lat_blocks):
    i = pl.program_id(0)
    return jnp.where(i < n_lat_blocks, i // bpb, MOD_ROWS // 2)


def _mod_kernel(c_ref, w_ref, b_ref, o_ref):
    c = c_ref[...]
    s = c * _sigmoid(c)
    o_ref[0] = jnp.dot(s, w_ref[0], precision=HIGHEST, preferred_element_type=F32) + b_ref[0]


def _modulation(cvec, w_mod, b_mod):
    depth, d, n = w_mod.shape
    tn = 1024
    return pl.pallas_call(
        _mod_kernel,
        grid=(depth, n // tn),
        in_specs=[
            pl.BlockSpec((MOD_ROWS, d), lambda l, j: (0, 0)),
            pl.BlockSpec((1, d, tn), lambda l, j: (l, 0, j)),
            pl.BlockSpec((1, 1, tn), lambda l, j: (l, 0, j)),
        ],
        out_specs=pl.BlockSpec((1, MOD_ROWS, tn), lambda l, j: (l, 0, j)),
        out_shape=jax.ShapeDtypeStruct((depth, MOD_ROWS, n), F32),
        compiler_params=_cparams("parallel", "parallel"),
        name="adaln_mod",
    )(cvec, w_mod, b_mod.reshape(depth, 1, n))


def _conv_in_kernel(x_ref, g_ref, mod_ref, w_ref, b_ref, v_ref, *, bpb, n_lat):
    d = x_ref.shape[1]
    b = _batch_row(bpb, n_lat)
    h = _norm_mod(x_ref[...], g_ref[...], mod_ref, b, 0, d)
    u = jnp.dot(h.astype(BF16), w_ref[...], preferred_element_type=F32) + b_ref[...]
    v_ref[...] = u[:, :d] * _sigmoid(u[:, d:])


def _conv_in(x, g, mod, w_in, b_in, *, tm, bpb, n_lat):
    t, d = x.shape
    return pl.pallas_call(
        functools.partial(_conv_in_kernel, bpb=bpb, n_lat=n_lat),
        grid=(t // tm,),
        in_specs=[
            pl.BlockSpec((tm, d), lambda i: (i, 0)),
            pl.BlockSpec((1, d), lambda i: (0, 0)),
            pl.BlockSpec(mod.shape, lambda i: (0, 0)),
            pl.BlockSpec((d, 2 * d), lambda i: (0, 0)),
            pl.BlockSpec((1, 2 * d), lambda i: (0, 0)),
        ],
        out_specs=pl.BlockSpec((tm, d), lambda i: (i, 0)),
        out_shape=jax.ShapeDtypeStruct((t, d), F32),
        compiler_params=_cparams("parallel"),
        name="conv_in",
    )(x, g.reshape(1, d), mod, w_in.astype(BF16), b_in.reshape(1, 2 * d))


def _conv_out_kernel(vp_ref, vc_ref, vn_ref, x_ref, wdw_ref, bdw_ref, gn_ref, wo_ref, bo_ref, mod_ref,
                     o_ref, buf, z, sh, *, bpb, n_lat, tm):
    d = x_ref.shape[1]
    i = pl.program_id(0)
    j = i % bpb
    b = _batch_row(bpb, n_lat)
    h = CONV_HALO
    buf[0:h, :] = jnp.where(j == 0, 0.0, vp_ref[...])
    buf[h:h + tm, :] = vc_ref[...]
    buf[h + tm:, :] = jnp.where(j == bpb - 1, 0.0, vn_ref[...])
    rc, cw = CONV_ROW_CHUNK, CONV_COL_CHUNK
    off = h - CONV_WIDTH // 2
    n_sh = sh.shape[1]

    def col_chunk(ci, carry):
        c0 = pl.multiple_of(ci * cw, cw)
        for s in range(SUBLANES):
            sh[s] = buf[s:s + n_sh, pl.ds(c0, cw)]
        for r in range(tm // rc):
            acc = jnp.zeros((rc, cw), F32)
            for t in range(CONV_WIDTH):
                q, s = divmod(off + t, SUBLANES)
                row0 = r * rc + q * SUBLANES
                acc = acc + sh[s, row0:row0 + rc, :] * wdw_ref[t:t + 1, pl.ds(c0, cw)]
            z[r * rc:(r + 1) * rc, pl.ds(c0, cw)] = acc
        return carry

    lax.fori_loop(0, d // cw, col_chunk, 0)
    zz = _rms(z[...] + bdw_ref[...], gn_ref[...])
    zz = zz * _sigmoid(zz)
    y = jnp.dot(zz.astype(BF16), wo_ref[...], preferred_element_type=F32) + bo_ref[...]
    o_ref[...] = x_ref[...] + _mod_row(mod_ref, b, 2, d) * y


def _conv_out(v, x, mod, w_dw, b_dw, g_norm, w_out, b_out, *, tm, bpb, n_lat):
    t, d = x.shape
    hb = tm // CONV_HALO
    n_halo = t // CONV_HALO
    wdw = jnp.zeros((32, d), F32).at[:CONV_WIDTH].set(w_dw)
    n_tap_tiles = (CONV_HALO - CONV_WIDTH // 2 + CONV_WIDTH - 1) // SUBLANES
    return pl.pallas_call(
        functools.partial(_conv_out_kernel, bpb=bpb, n_lat=n_lat, tm=tm),
        grid=(t // tm,),
        in_specs=[
            pl.BlockSpec((CONV_HALO, d), lambda i: (jnp.maximum(i * hb - 1, 0), 0)),
            pl.BlockSpec((tm, d), lambda i: (i, 0)),
            pl.BlockSpec((CONV_HALO, d), lambda i: (jnp.minimum((i + 1) * hb, n_halo - 1), 0)),
            pl.BlockSpec((tm, d), lambda i: (i, 0)),
            pl.BlockSpec((32, d), lambda i: (0, 0)),
            pl.BlockSpec((1, d), lambda i: (0, 0)),
            pl.BlockSpec((1, d), lambda i: (0, 0)),
            pl.BlockSpec((d, d), lambda i: (0, 0)),
            pl.BlockSpec((1, d), lambda i: (0, 0)),
            pl.BlockSpec(mod.shape, lambda i: (0, 0)),
        ],
        out_specs=pl.BlockSpec((tm, d), lambda i: (i, 0)),
        out_shape=jax.ShapeDtypeStruct((t, d), F32),
        scratch_shapes=[
            pltpu.VMEM((tm + 2 * CONV_HALO, d), F32),
            pltpu.VMEM((tm, d), F32),
            pltpu.VMEM((SUBLANES, tm + n_tap_tiles * SUBLANES, CONV_COL_CHUNK), F32),
        ],
        compiler_params=_cparams("parallel"),
        name="conv_out",
    )(v, v, v, x, wdw, b_dw.reshape(1, d), g_norm.reshape(1, d), w_out.astype(BF16), b_out.reshape(1, d), mod)


def _prenorm_kernel(x_ref, g_ref, mod_ref, h_ref, *, bpb, n_lat):
    d = x_ref.shape[1]
    b = _batch_row(bpb, n_lat)
    h_ref[...] = _norm_mod(x_ref[...], g_ref[...], mod_ref, b, 0, d).astype(h_ref.dtype)


def _prenorm(x, g, mod, *, tm, bpb, n_lat, dtype):
    t, d = x.shape
    return pl.pallas_call(
        functools.partial(_prenorm_kernel, bpb=bpb, n_lat=n_lat),
        grid=(t // tm,),
        in_specs=[
            pl.BlockSpec((tm, d), lambda i: (i, 0)),
            pl.BlockSpec((1, d), lambda i: (0, 0)),
            pl.BlockSpec(mod.shape, lambda i: (0, 0)),
        ],
        out_specs=pl.BlockSpec((tm, d), lambda i: (i, 0)),
        out_shape=jax.ShapeDtypeStruct((t, d), dtype),
        compiler_params=_cparams("parallel"),
        name="prenorm",
    )(x, g.reshape(1, d), mod)


def _dft1_kernel(h_ref, m_ref, o_ref, *, nb_chunk, nb, na):
    bc = pl.program_id(2)
    for bi in range(nb_chunk):
        b = bc * nb_chunk + bi
        xs = h_ref[0, pl.ds(b, na, stride=nb), :]
        t = jnp.dot(m_ref[bi], xs.astype(BF16), preferred_element_type=F32)
        o_ref[0, 0, bi] = t[:na].astype(BF16)
        o_ref[0, 1, bi] = t[na:].astype(BF16)


def _dft_tables(seq):
    na, nb = DFT_NA, seq // DFT_NA
    ka = jnp.arange(na, dtype=jnp.int32)
    a = jnp.arange(na, dtype=jnp.int32)
    b = jnp.arange(nb, dtype=jnp.int32)
    n = a[None, None, :] * nb + b[:, None, None]
    ang = ((ka[None, :, None] * n) % seq).astype(F32) * (2.0 * math.pi / seq)
    m1 = jnp.concatenate([jnp.cos(ang), -jnp.sin(ang)], axis=1).astype(BF16)
    kb = jnp.arange(nb, dtype=jnp.int32)
    ang2 = ((kb[:, None] * b[None, :]) % nb).astype(F32) * (2.0 * math.pi / nb)
    c2, s2 = jnp.cos(ang2), jnp.sin(ang2)
    m2 = jnp.concatenate([jnp.concatenate([c2, s2], axis=1),
                          jnp.concatenate([s2, -c2], axis=1)], axis=0).astype(BF16)
    return m1, m2


def _const_lhs_matmul_kernel(a_ref, x_ref, o_ref):
    o_ref[0] = jnp.dot(a_ref[...], x_ref[0], preferred_element_type=F32).astype(o_ref.dtype)


def _const_lhs_matmul(a, x, *, cn):
    m, k = a.shape
    bsz, _, n = x.shape
    return pl.pallas_call(
        _const_lhs_matmul_kernel,
        grid=(bsz, n // cn),
        in_specs=[
            pl.BlockSpec((m, k), lambda b, j: (0, 0)),
            pl.BlockSpec((1, k, cn), lambda b, j: (b, 0, j)),
        ],
        out_specs=pl.BlockSpec((1, m, cn), lambda b, j: (b, 0, j)),
        out_shape=jax.ShapeDtypeStruct((bsz, m, n), BF16),
        compiler_params=_cparams("parallel", "parallel"),
        name="const_lhs_matmul",
    )(a, x)


def _seq_dft_two_stage(h, bsz, seq):
    d = h.shape[1]
    na, nb = DFT_NA, seq // DFT_NA
    gw = LANES
    nb_chunk = min(nb, 8)
    m1, m2 = _dft_tables(seq)
    t1 = pl.pallas_call(
        functools.partial(_dft1_kernel, nb_chunk=nb_chunk, nb=nb, na=na),
        grid=(bsz, d // gw, nb // nb_chunk),
        in_specs=[
            pl.BlockSpec((1, seq, gw), lambda b, g, c: (b, 0, g)),
            pl.BlockSpec((nb_chunk, 2 * na, na), lambda b, g, c: (c, 0, 0)),
        ],
        out_specs=pl.BlockSpec((1, 2, nb_chunk, na, gw), lambda b, g, c: (b, 0, c, 0, g)),
        out_shape=jax.ShapeDtypeStruct((bsz, 2, nb, na, d), BF16),
        compiler_params=_cparams("parallel", "parallel", "arbitrary"),
        name="dft_stage1",
    )(h.reshape(bsz, seq, d), m1)
    pq = _const_lhs_matmul(m2, t1.reshape(bsz, 2 * nb, na * d), cn=min(na * d, 8192))
    return pq.reshape(bsz, 2, seq, d)


def _seq_dft_dense(h, bsz, seq):
    d = h.shape[1]
    k = jnp.arange(seq, dtype=jnp.int32)
    ang = ((k[:, None] * k[None, :]) % seq).astype(F32) * (2.0 * math.pi / seq)
    a = jnp.concatenate([jnp.cos(ang), jnp.sin(ang)], axis=0).astype(BF16)
    pq = _const_lhs_matmul(a, h.reshape(bsz, seq, d), cn=d)
    return pq.reshape(bsz, 2, seq, d)


def _fourier_out_kernel(p_ref, q_ref, x_ref, cm_ref, wo_ref, bo_ref, mod_ref, o_ref, *, bpb, n_lat):
    d = x_ref.shape[1]
    gw = d // FNET_GROUPS
    b = _batch_row(bpb, n_lat)
    p, q = p_ref[0, 0], q_ref[0, 0]
    f = [jnp.dot(p[:, g * gw:(g + 1) * gw], cm_ref[:gw], preferred_element_type=F32)
         + jnp.dot(q[:, g * gw:(g + 1) * gw], cm_ref[gw:], preferred_element_type=F32)
         for g in range(FNET_GROUPS)]
    f = jnp.concatenate(f, axis=1).astype(BF16)
    y = jnp.dot(f, wo_ref[...], preferred_element_type=F32) + bo_ref[...]
    o_ref[...] = x_ref[...] + _mod_row(mod_ref, b, 2, d) * y


def _fourier_out(pq, x, mod, w_out, b_out, *, seq, tm, bpb, n_lat):
    t, d = x.shape
    gw = d // FNET_GROUPS
    k = jnp.arange(gw, dtype=jnp.int32)
    ang = ((k[:, None] * k[None, :]) % gw).astype(F32) * (2.0 * math.pi / gw)
    scale = 1.0 / math.sqrt(seq * gw)
    cm = (jnp.concatenate([jnp.cos(ang), -jnp.sin(ang)], axis=0) * scale).astype(BF16)
    return pl.pallas_call(
        functools.partial(_fourier_out_kernel, bpb=bpb, n_lat=n_lat),
        grid=(t // tm,),
        in_specs=[
            pl.BlockSpec((1, 1, tm, d), lambda i: (i // bpb, 0, i % bpb, 0)),
            pl.BlockSpec((1, 1, tm, d), lambda i: (i // bpb, 1, i % bpb, 0)),
            pl.BlockSpec((tm, d), lambda i: (i, 0)),
            pl.BlockSpec((2 * gw, gw), lambda i: (0, 0)),
            pl.BlockSpec((d, d), lambda i: (0, 0)),
            pl.BlockSpec((1, d), lambda i: (0, 0)),
            pl.BlockSpec(mod.shape, lambda i: (0, 0)),
        ],
        out_specs=pl.BlockSpec((tm, d), lambda i: (i, 0)),
        out_shape=jax.ShapeDtypeStruct((t, d), F32),
        compiler_params=_cparams("parallel"),
        name="fourier_out",
    )(pq, pq, x, cm, w_out.astype(BF16), b_out.reshape(1, d), mod)


def _head_perm():
    return jnp.concatenate([jnp.arange(0, HEAD_DIM, 2), jnp.arange(1, HEAD_DIM, 2)])


def _rope_tables(seq):
    rows = seq // GRID_W
    row = jnp.repeat(jnp.arange(rows, dtype=F32), GRID_W)
    col = jnp.tile(jnp.arange(GRID_W, dtype=F32), rows)
    n_pairs_axis = HEAD_DIM // 4
    inv = ROPE_THETA ** (-jnp.arange(n_pairs_axis, dtype=F32) / n_pairs_axis)
    ang = jnp.concatenate([row[:, None] * inv, col[:, None] * inv], axis=-1)
    cos, sin = jnp.cos(ang), jnp.sin(ang)
    return jnp.concatenate([cos, cos], axis=-1), jnp.concatenate([-sin, sin], axis=-1)


def _qkv_kernel(x_ref, g_ref, mod_ref, w_ref, gq_ref, gk_ref, cos_ref, sin_ref, q_ref, k_ref, v_ref, *, bpb, n_lat):
    d = x_ref.shape[1]
    b = _batch_row(bpb, n_lat)
    h = _norm_mod(x_ref[...], g_ref[...], mod_ref, b, 0, d)
    u = jnp.dot(h.astype(BF16), w_ref[...], preferred_element_type=F32)
    cos, sin = cos_ref[...], sin_ref[...]
    q_dim = q_ref.shape[1]
    kv_dim = k_ref.shape[1]

    def norm_rope(xh, gain):
        xh = _rms(xh, gain)
        return xh * cos + pltpu.roll(xh, HEAD_DIM // 2, 1) * sin

    q_scale = HEAD_DIM ** -0.5 * math.log2(math.e)
    for hh in range(q_dim // HEAD_DIM):
        sl = slice(hh * HEAD_DIM, (hh + 1) * HEAD_DIM)
        q_ref[:, sl] = (norm_rope(u[:, sl], gq_ref[...]) * q_scale).astype(BF16)
    for hh in range(kv_dim // HEAD_DIM):
        sl = slice(hh * HEAD_DIM, (hh + 1) * HEAD_DIM)
        k_ref[:, sl] = norm_rope(u[:, q_dim + hh * HEAD_DIM:q_dim + (hh + 1) * HEAD_DIM], gk_ref[...]).astype(BF16)
    v_ref[...] = u[:, q_dim + kv_dim:].astype(BF16)


def _qkv(x, g, mod, w_qkv, gq, gk, cos, sin, *, tm, bpb, n_lat):
    t, d = x.shape
    q_dim = N_KV_HEADS * Q_PER_KV * HEAD_DIM
    kv_dim = N_KV_HEADS * HEAD_DIM
    perm = _head_perm()
    n_heads_qk = (q_dim + kv_dim) // HEAD_DIM
    cols = (jnp.arange(n_heads_qk)[:, None] * HEAD_DIM + perm[None, :]).reshape(-1)
    cols = jnp.concatenate([cols, jnp.arange(q_dim + kv_dim, q_dim + 2 * kv_dim)])
    w = w_qkv[:, cols].astype(BF16)
    n_pos = cos.shape[0] // tm
    return pl.pallas_call(
        functools.partial(_qkv_kernel, bpb=bpb, n_lat=n_lat),
        grid=(t // tm,),
        in_specs=[
            pl.BlockSpec((tm, d), lambda i: (i, 0)),
            pl.BlockSpec((1, d), lambda i: (0, 0)),
            pl.BlockSpec(mod.shape, lambda i: (0, 0)),
            pl.BlockSpec((d, q_dim + 2 * kv_dim), lambda i: (0, 0)),
            pl.BlockSpec((1, HEAD_DIM), lambda i: (0, 0)),
            pl.BlockSpec((1, HEAD_DIM), lambda i: (0, 0)),
            pl.BlockSpec((tm, HEAD_DIM), lambda i: (i % n_pos, 0)),
            pl.BlockSpec((tm, HEAD_DIM), lambda i: (i % n_pos, 0)),
        ],
        out_specs=[
            pl.BlockSpec((tm, q_dim), lambda i: (i, 0)),
            pl.BlockSpec((tm, kv_dim), lambda i: (i, 0)),
            pl.BlockSpec((tm, kv_dim), lambda i: (i, 0)),
        ],
        out_shape=[
            jax.ShapeDtypeStruct((t, q_dim), BF16),
            jax.ShapeDtypeStruct((t, kv_dim), BF16),
            jax.ShapeDtypeStruct((t, kv_dim), BF16),
        ],
        compiler_params=_cparams("parallel"),
        name="qkv_proj",
    )(x, g.reshape(1, d), mod, w, gq[perm].reshape(1, HEAD_DIM), gk[perm].reshape(1, HEAD_DIM), cos, sin)


ATTN_ROW_CHUNK = 16
ATTN_MAX_TK = 1408


def _attn_kernel(qt_ref, k_ref, vt_ref, o_ref, *scratch, tq, tk, nk):
    s_ref, p_ref, acc_ref = scratch
    nq = Q_PER_KV * tq
    acc_ref[...] = jnp.zeros(acc_ref.shape, F32)
    rc = ATTN_ROW_CHUNK

    def kv_chunk(j, carry):
        m_prev, l_prev = carry
        kk = k_ref[0, pl.ds(pl.multiple_of(j * tk, tk), tk), :]
        s_ref[...] = jnp.dot(kk, qt_ref[0, 0, 0], preferred_element_type=F32)
        mx = s_ref[0:SUBLANES, :]
        for r in range(1, tk // SUBLANES):
            mx = jnp.maximum(mx, s_ref[r * SUBLANES:(r + 1) * SUBLANES, :])
        m_new = jnp.maximum(m_prev, jnp.max(mx, axis=0, keepdims=True))
        psum = jnp.zeros((SUBLANES, nq), F32)
        for r in range(tk // rc):
            p = jnp.exp2(s_ref[r * rc:(r + 1) * rc, :] - m_new)
            p_ref[r * rc:(r + 1) * rc, :] = p.astype(BF16)
            for h in range(rc // SUBLANES):
                psum = psum + p[h * SUBLANES:(h + 1) * SUBLANES]
        alpha = jnp.exp2(m_prev - m_new)
        acc_ref[...] = alpha * acc_ref[...] + jnp.dot(vt_ref[0, 0, j], p_ref[...], preferred_element_type=F32)
        return m_new, alpha * l_prev + jnp.sum(psum, axis=0, keepdims=True)

    init = (jnp.full((1, nq), -jnp.inf, F32), jnp.zeros((1, nq), F32))
    _, l_all = lax.fori_loop(0, nk, kv_chunk, init)
    out = (acc_ref[...] / l_all).T
    for g in range(Q_PER_KV):
        o_ref[0, :, g * HEAD_DIM:(g + 1) * HEAD_DIM] = out[g * tq:(g + 1) * tq].astype(o_ref.dtype)


def _attention(q, k_all, v_all, *, tq, tk):
    bsz, seq, q_dim = q.shape
    lk = k_all.shape[1]
    gq = Q_PER_KV * HEAD_DIM
    nq = Q_PER_KV * tq
    nk = lk // tk
    assert tk % ATTN_ROW_CHUNK == 0 and nq % LANES == 0
    qt = q.reshape(bsz, seq // tq, tq, N_KV_HEADS, Q_PER_KV, HEAD_DIM).transpose(0, 3, 1, 5, 4, 2)
    qt = qt.reshape(bsz, N_KV_HEADS, seq // tq, HEAD_DIM, nq)
    vt = v_all.reshape(bsz, nk, tk, N_KV_HEADS, HEAD_DIM).transpose(0, 3, 1, 4, 2)
    return pl.pallas_call(
        functools.partial(_attn_kernel, tq=tq, tk=tk, nk=nk),
        grid=(bsz, N_KV_HEADS, seq // tq),
        in_specs=[
            pl.BlockSpec((1, 1, 1, HEAD_DIM, nq), lambda b, h, i: (b, h, i, 0, 0)),
            pl.BlockSpec((1, lk, HEAD_DIM), lambda b, h, i: (b, 0, h)),
            pl.BlockSpec((1, 1, nk, HEAD_DIM, tk), lambda b, h, i: (b, h, 0, 0, 0)),
        ],
        out_specs=pl.BlockSpec((1, tq, gq), lambda b, h, i: (b, i, h)),
        out_shape=jax.ShapeDtypeStruct((bsz, seq, q_dim), BF16),
        scratch_shapes=[pltpu.VMEM((tk, nq), F32), pltpu.VMEM((tk, nq), BF16), pltpu.VMEM((HEAD_DIM, nq), F32)],
        compiler_params=_cparams("parallel", "parallel", "parallel"),
        name="flash_attention",
    )(qt, k_all, vt)


def _proj_residual_kernel(a_ref, x_ref, w_ref, mod_ref, o_ref, *, bpb, n_lat):
    d = x_ref.shape[1]
    b = _batch_row(bpb, n_lat)
    y = jnp.dot(a_ref[...], w_ref[...], preferred_element_type=F32)
    o_ref[...] = x_ref[...] + _mod_row(mod_ref, b, 2, d) * y


def _proj_residual(a, x, mod, w, *, tm, bpb, n_lat):
    t, d = x.shape
    ka = a.shape[1]
    return pl.pallas_call(
        functools.partial(_proj_residual_kernel, bpb=bpb, n_lat=n_lat),
        grid=(t // tm,),
        in_specs=[
            pl.BlockSpec((tm, ka), lambda i: (i, 0)),
            pl.BlockSpec((tm, d), lambda i: (i, 0)),
            pl.BlockSpec((ka, d), lambda i: (0, 0)),
            pl.BlockSpec(mod.shape, lambda i: (0, 0)),
        ],
        out_specs=pl.BlockSpec((tm, d), lambda i: (i, 0)),
        out_shape=jax.ShapeDtypeStruct((t, d), F32),
        compiler_params=_cparams("parallel"),
        name="proj_residual",
    )(a, x, w.astype(BF16), mod)


ROUTE_E, ROUTE_GATE, ROUTE_RANK = 0, 2, 4
ROUTER_LOGIT0 = N_EXPERT_GROUPS


def _router_kernel(x_ref, g_ref, mod_ref, wr_ref, br_ref, tri_ref, h_ref, r_ref, cnt_ref, carry, *, bpb, n_lat):
    d = x_ref.shape[1]
    tm = x_ref.shape[0]
    i = pl.program_id(0)
    b = _batch_row(bpb, n_lat)

    @pl.when(i == 0)
    def _():
        carry[...] = jnp.zeros(carry.shape, F32)

    h = _norm_mod(x_ref[...], g_ref[...], mod_ref, b, 3, d)
    h_ref[...] = h
    h_hi = h.astype(BF16)
    h_lo = (h - h_hi.astype(F32)).astype(BF16)
    logits = (jnp.dot(h_hi, wr_ref[0], preferred_element_type=F32) + jnp.dot(h_lo, wr_ref[0], preferred_element_type=F32)
              + jnp.dot(h_hi, wr_ref[1], preferred_element_type=F32) + br_ref[...])
    lane = lax.broadcasted_iota(jnp.int32, (tm, LANES), 1).astype(F32)
    big = float(LANES)
    neg = -jnp.inf
    gmask = lane < N_EXPERT_GROUPS
    gl = jnp.where(gmask, logits, neg)
    ge = jnp.exp(gl - jnp.max(gl, axis=-1, keepdims=True))
    gp = ge / jnp.sum(ge, axis=-1, keepdims=True)
    g_top = jnp.max(gp, axis=-1, keepdims=True)
    g_idx = jnp.min(jnp.where(gmask & (gp == g_top), lane, big), axis=-1, keepdims=True)
    lo = ROUTER_LOGIT0 + g_idx * EXPERTS_PER_GROUP
    emask = (lane >= lo) & (lane < lo + EXPERTS_PER_GROUP)
    el = jnp.where(emask, logits, neg)
    ee = jnp.exp(el - jnp.max(el, axis=-1, keepdims=True))
    ep = ee / jnp.sum(ee, axis=-1, keepdims=True)
    p1 = jnp.max(ep, axis=-1, keepdims=True)
    i1 = jnp.min(jnp.where(emask & (ep == p1), lane, big), axis=-1, keepdims=True)
    rest = emask & (lane != i1)
    p2 = jnp.max(jnp.where(rest, ep, -1.0), axis=-1, keepdims=True)
    i2 = jnp.min(jnp.where(rest & (ep == p2), lane, big), axis=-1, keepdims=True)
    denom = p1 + p2
    gate1 = g_top * p1 / denom
    gate2 = g_top * p2 / denom
    hit1 = lane == i1
    hit2 = lane == i2
    onehot = jnp.where(hit1 | hit2, 1.0, 0.0)
    cum = jnp.dot(tri_ref[...], onehot.astype(BF16), preferred_element_type=F32) + carry[...]
    rank1 = jnp.sum(jnp.where(hit1, cum, 0.0), axis=-1, keepdims=True)
    rank2 = jnp.sum(jnp.where(hit2, cum, 0.0), axis=-1, keepdims=True)
    carry[...] = carry[...] + jnp.sum(onehot, axis=0, keepdims=True)
    cnt_ref[...] = carry[...]
    rec = jnp.zeros((tm, LANES), F32)
    for ln, val in ((ROUTE_E, i1 - ROUTER_LOGIT0), (ROUTE_E + 1, i2 - ROUTER_LOGIT0), (ROUTE_GATE, gate1),
                    (ROUTE_GATE + 1, gate2), (ROUTE_RANK, rank1), (ROUTE_RANK + 1, rank2)):
        rec = jnp.where(lane == float(ln), val, rec)
    r_ref[...] = rec


def _router(x, g, mod, w_group, b_group, w_expert, b_expert, *, tm, bpb, n_lat):
    t, d = x.shape
    n_log = N_EXPERT_GROUPS + N_EXPERTS
    wr = jnp.zeros((d, LANES), F32).at[:, :N_EXPERT_GROUPS].set(w_group).at[:, N_EXPERT_GROUPS:n_log].set(w_expert)
    br = jnp.zeros((1, LANES), F32).at[0, :N_EXPERT_GROUPS].set(b_group).at[0, N_EXPERT_GROUPS:n_log].set(b_expert)
    wr_hi = wr.astype(BF16)
    wr = jnp.stack([wr_hi, (wr - wr_hi.astype(F32)).astype(BF16)])
    tri = (jnp.arange(tm)[:, None] > jnp.arange(tm)[None, :]).astype(BF16)
    return pl.pallas_call(
        functools.partial(_router_kernel, bpb=bpb, n_lat=n_lat),
        grid=(t // tm,),
        in_specs=[
            pl.BlockSpec((tm, d), lambda i: (i, 0)),
            pl.BlockSpec((1, d), lambda i: (0, 0)),
            pl.BlockSpec(mod.shape, lambda i: (0, 0)),
            pl.BlockSpec((2, d, LANES), lambda i: (0, 0, 0)),
            pl.BlockSpec((1, LANES), lambda i: (0, 0)),
            pl.BlockSpec((tm, tm), lambda i: (0, 0)),
        ],
        out_specs=[
            pl.BlockSpec((tm, d), lambda i: (i, 0)),
            pl.BlockSpec((tm, LANES), lambda i: (i, 0)),
            pl.BlockSpec((1, LANES), lambda i: (0, 0)),
        ],
        out_shape=[
            jax.ShapeDtypeStruct((t, d), F32),
            jax.ShapeDtypeStruct((t, LANES), F32),
            jax.ShapeDtypeStruct((1, LANES), F32),
        ],
        scratch_shapes=[pltpu.VMEM((1, LANES), F32)],
        compiler_params=_cparams("arbitrary"),
        name="moe_router",
    )(x, g.reshape(1, d), mod, wr, br, tri)


def _moe_ffn_kernel(be_ref, bv_ref, x_ref, wg_ref, wu_ref, wd_ref, o_ref, wgb, wub, wdb):
    i = pl.program_id(0)
    e = be_ref[i]
    e_prev = be_ref[jnp.maximum(i - 1, 0)]

    @pl.when((i == 0) | (e != e_prev))
    def _():
        wgb[...] = wg_ref[0, 0].astype(BF16)
        wub[...] = wu_ref[0, 0].astype(BF16)
        wdb[...] = wd_ref[0, 0].astype(BF16)

    n_valid = bv_ref[i]

    @pl.when(n_valid > 0)
    def _():
        xb = x_ref[...].astype(BF16)
        gt = jnp.dot(xb, wgb[...], preferred_element_type=F32)
        up = jnp.dot(xb, wub[...], preferred_element_type=F32)
        a = gt * _sigmoid(gt) * up
        o_ref[...] = jnp.dot(a.astype(BF16), wdb[...], preferred_element_type=F32)

    @pl.when(n_valid <= 0)
    def _():
        o_ref[...] = jnp.zeros(o_ref.shape, F32)


def _moe_ffn(buf, blk_expert, blk_valid, w_gate, w_up, w_down, layer):
    p, d = buf.shape
    de = w_gate.shape[-1]
    n_blocks = p // MOE_BLOCK
    grid_spec = pltpu.PrefetchScalarGridSpec(
        num_scalar_prefetch=2,
        grid=(n_blocks,),
        in_specs=[
            pl.BlockSpec((MOE_BLOCK, d), lambda i, be, bv: (i, 0)),
            pl.BlockSpec((1, 1, d, de), lambda i, be, bv: (layer, be[i], 0, 0)),
            pl.BlockSpec((1, 1, d, de), lambda i, be, bv: (layer, be[i], 0, 0)),
            pl.BlockSpec((1, 1, de, d), lambda i, be, bv: (layer, be[i], 0, 0)),
        ],
        out_specs=pl.BlockSpec((MOE_BLOCK, d), lambda i, be, bv: (i, 0)),
        scratch_shapes=[pltpu.VMEM((d, de), BF16), pltpu.VMEM((d, de), BF16), pltpu.VMEM((de, d), BF16)],
    )
    return pl.pallas_call(
        _moe_ffn_kernel,
        grid_spec=grid_spec,
        out_shape=jax.ShapeDtypeStruct((p, d), F32),
        compiler_params=_cparams("arbitrary"),
        name="moe_ffn",
    )(blk_expert, blk_valid, buf, w_gate, w_up, w_down)


DMA_ISSUE_UNROLL = 8


def _row_copy(src, src_row, dst, dst_row, sem):
    return pltpu.make_async_copy(src.at[pl.ds(src_row, 1)], dst.at[pl.ds(dst_row, 1)], sem)


def _moe_dispatch_kernel(pe_ref, dest_ref, h_ref, buf_ref, zero_ref, sems, zsem):
    tm = h_ref.shape[0]
    n_blocks = buf_ref.shape[0] // MOE_BLOCK

    def zero_block(first_row):
        return pltpu.make_async_copy(zero_ref, buf_ref.at[pl.ds(pl.multiple_of(first_row, MOE_BLOCK), MOE_BLOCK)], zsem)

    @pl.when(pl.program_id(0) == 0)
    def _():
        zero_ref[...] = jnp.zeros(zero_ref.shape, F32)

        def expert_has_rows(e):
            return pe_ref[e] > jnp.where(e > 0, pe_ref[jnp.maximum(e - 1, 0)], 0)

        def fill(start):
            def body(e, carry):
                @pl.when(expert_has_rows(e))
                def _():
                    cp = zero_block(pe_ref[e] - MOE_BLOCK)
                    cp.start() if start else cp.wait()
                return carry
            return body

        def tail(start):
            def body(b, carry):
                cp = zero_block(b * MOE_BLOCK)
                cp.start() if start else cp.wait()
                return carry
            return body

        first_unused = pe_ref[N_EXPERTS - 1] // MOE_BLOCK
        for start in (True, False):
            lax.fori_loop(0, N_EXPERTS, fill(start), 0)
            lax.fori_loop(first_unused, n_blocks, tail(start), 0)

    def issue(r, carry):
        for j in range(2):
            _row_copy(h_ref, r, buf_ref, dest_ref[0, 0, 2 * r + j], sems.at[j]).start()
        return carry

    lax.fori_loop(0, tm, issue, 0, unroll=DMA_ISSUE_UNROLL)
    for j in range(2):
        pltpu.make_async_copy(h_ref, buf_ref.at[pl.ds(0, tm)], sems.at[j]).wait()


def _moe_dispatch(h, dest, pad_end, n_slots, *, tm):
    t, d = h.shape
    grid_spec = pltpu.PrefetchScalarGridSpec(
        num_scalar_prefetch=1,
        grid=(t // tm,),
        in_specs=[
            pl.BlockSpec((1, 1, 2 * tm), lambda i, pe: (i, 0, 0), memory_space=pltpu.SMEM),
            pl.BlockSpec((tm, d), lambda i, pe: (i, 0)),
        ],
        out_specs=pl.BlockSpec(memory_space=pl.ANY),
        scratch_shapes=[pltpu.VMEM((MOE_BLOCK, d), F32), pltpu.SemaphoreType.DMA((2,)), pltpu.SemaphoreType.DMA(())],
    )
    return pl.pallas_call(
        _moe_dispatch_kernel,
        grid_spec=grid_spec,
        out_shape=jax.ShapeDtypeStruct((n_slots, d), F32),
        compiler_params=_cparams("arbitrary"),
        name="moe_dispatch",
    )(pad_end.astype(jnp.int32), dest.reshape(t // tm, 1, 2 * tm), h)


def _moe_combine_kernel(dest_ref, x_ref, r_ref, mod_ref, yb_ref, o_ref, y0, y1, sems, *, bpb, n_lat):
    tm, d = x_ref.shape
    b = _batch_row(bpb, n_lat)
    ys = (y0, y1)

    def issue(r, carry):
        for j in range(2):
            _row_copy(yb_ref, dest_ref[0, 0, 2 * r + j], ys[j], r, sems.at[j]).start()
        return carry

    lax.fori_loop(0, tm, issue, 0, unroll=DMA_ISSUE_UNROLL)
    for j in range(2):
        pltpu.make_async_copy(yb_ref.at[pl.ds(0, tm)], ys[j], sems.at[j]).wait()
    r = r_ref[...]
    y = y0[...] * r[:, ROUTE_GATE:ROUTE_GATE + 1] + y1[...] * r[:, ROUTE_GATE + 1:ROUTE_GATE + 2]
    o_ref[...] = x_ref[...] + _mod_row(mod_ref, b, 5, d) * y


def _moe_combine(x, yb, dest, route, mod, *, tm, bpb, n_lat):
    t, d = x.shape
    row = pl.BlockSpec((tm, d), lambda i: (i, 0))
    return pl.pallas_call(
        functools.partial(_moe_combine_kernel, bpb=bpb, n_lat=n_lat),
        grid=(t // tm,),
        in_specs=[
            pl.BlockSpec((1, 1, 2 * tm), lambda i: (i, 0, 0), memory_space=pltpu.SMEM),
            row,
            pl.BlockSpec((tm, LANES), lambda i: (i, 0)),
            pl.BlockSpec(mod.shape, lambda i: (0, 0)),
            pl.BlockSpec(memory_space=pl.ANY),
        ],
        out_specs=row,
        out_shape=jax.ShapeDtypeStruct((t, d), F32),
        scratch_shapes=[pltpu.VMEM((tm, d), F32), pltpu.VMEM((tm, d), F32), pltpu.SemaphoreType.DMA((2,))],
        compiler_params=_cparams("arbitrary"),
        name="moe_combine",
    )(dest.reshape(t // tm, 1, 2 * tm), x, route, mod, yb)


def _hier_moe_residual(x, g, mod, w_group, b_group, w_expert, b_expert, w_gate, w_up, w_down, layer, *, tm, bpb, n_lat):
    t, d = x.shape
    h, route, counts = _router(x, g, mod, w_group, b_group, w_expert, b_expert, tm=tm, bpb=bpb, n_lat=n_lat)
    counts = counts[0, ROUTER_LOGIT0:ROUTER_LOGIT0 + N_EXPERTS].astype(jnp.int32)
    expert = route[:, ROUTE_E:ROUTE_E + 2].astype(jnp.int32)
    rank = route[:, ROUTE_RANK:ROUTE_RANK + 2].astype(jnp.int32)
    padded = (counts + MOE_BLOCK - 1) // MOE_BLOCK * MOE_BLOCK
    pad_end = jnp.cumsum(padded)
    pad_start = pad_end - padded
    dest = jnp.sum(jnp.where(expert[:, :, None] == jnp.arange(N_EXPERTS), pad_start, 0), axis=-1) + rank
    n_blocks = -(-(2 * t) // MOE_BLOCK) + N_EXPERTS
    blk_first = jnp.arange(n_blocks, dtype=jnp.int32) * MOE_BLOCK
    blk_expert = jnp.minimum(jnp.sum(blk_first[:, None] >= pad_end[None, :], axis=1), N_EXPERTS - 1).astype(jnp.int32)
    blk_rows = jnp.clip((pad_start + counts)[blk_expert] - blk_first, 0, MOE_BLOCK).astype(jnp.int32)
    buf = _moe_dispatch(h, dest, pad_end, n_blocks * MOE_BLOCK, tm=tm)
    yb = _moe_ffn(buf, blk_expert, blk_rows, w_gate, w_up, w_down, layer)
    return _moe_combine(x, yb, dest, route, mod, tm=tm, bpb=bpb, n_lat=n_lat)


def kernel(x, c, ctx, c_ctx, norm_g, w_mod, b_mod, conv_w_in, conv_b_in, conv_w_dw, conv_b_dw, conv_norm_g, conv_w_out, conv_b_out, fnet_w_out, fnet_b_out, attn_w_qkv, attn_q_norm_g, attn_k_norm_g, attn_w_out, moe_w_group, moe_b_group, moe_w_expert, moe_b_expert, moe_w_gate, moe_w_up, moe_w_down):
    bsz, seq, d = x.shape
    n_ctx = ctx.shape[1]
    depth = norm_g.shape[0]
    assert bsz <= MOD_ROWS // 2 and seq % DFT_NA == 0 and seq % GRID_W == 0
    t_lat, t_ctx = bsz * seq, bsz * n_ctx
    tm = min(512, seq)
    tmc = min(tm, n_ctx)
    assert seq % tm == 0 and n_ctx % tmc == 0 and tm % tmc == 0
    lat = dict(tm=tm, bpb=seq // tm, n_lat=t_lat // tm)
    cx = dict(tm=tmc, bpb=1, n_lat=0)
    tmb = tm if t_ctx % tm == 0 else tmc
    both = dict(tm=tmb, bpb=seq // tmb, n_lat=t_lat // tmb)
    last_reader = max([i for i in range(depth) if i % N_MIXERS == 2], default=-1)

    cvec = jnp.zeros((MOD_ROWS, d), F32).at[:bsz].set(c).at[MOD_ROWS // 2].set(c_ctx)
    mod_all = _modulation(cvec, w_mod, b_mod)
    cos, sin = _rope_tables(seq)

    xl = x.reshape(t_lat, d)
    xc = ctx.reshape(t_ctx, d)
    for i in range(depth):
        m, j = i % N_MIXERS, i // N_MIXERS
        ctx_on = i <= last_reader
        ctx_full = i < last_reader
        mod = mod_all[i]
        g1, g2 = norm_g[i, 0], norm_g[i, 1]
        if m == 0:
            cp = (conv_w_dw[j], conv_b_dw[j], conv_norm_g[j], conv_w_out[j], conv_b_out[j])
            v = _conv_in(xl, g1, mod, conv_w_in[j], conv_b_in[j], **lat)
            xl = _conv_out(v, xl, mod, *cp, **lat)
            if ctx_full:
                vc = _conv_in(xc, g1, mod, conv_w_in[j], conv_b_in[j], **cx)
                xc = _conv_out(vc, xc, mod, *cp, tm=n_ctx, bpb=1, n_lat=0)
        elif m == 1:
            hl = _prenorm(xl, g1, mod, dtype=F32, **lat)
            pq = _seq_dft_two_stage(hl, bsz, seq)
            xl = _fourier_out(pq, xl, mod, fnet_w_out[j], fnet_b_out[j], seq=seq, **lat)
            if ctx_full:
                hc = _prenorm(xc, g1, mod, dtype=BF16, **cx)
                pqc = _seq_dft_dense(hc, bsz, n_ctx)
                xc = _fourier_out(pqc, xc, mod, fnet_w_out[j], fnet_b_out[j], seq=n_ctx, tm=n_ctx, bpb=1, n_lat=0)
        else:
            gq, gk = attn_q_norm_g[j], attn_k_norm_g[j]
            q, k, vv = _qkv(xl, g1, mod, attn_w_qkv[j], gq, gk, cos, sin, **lat)
            ones = jnp.ones((tmc, HEAD_DIM), F32)
            _, kc, vc = _qkv(xc, g1, mod, attn_w_qkv[j], gq, gk, ones, jnp.zeros_like(ones), **cx)
            kv_dim = k.shape[1]
            k_all = jnp.concatenate([kc.reshape(bsz, n_ctx, kv_dim), k.reshape(bsz, seq, kv_dim)], axis=1)
            v_all = jnp.concatenate([vc.reshape(bsz, n_ctx, kv_dim), vv.reshape(bsz, seq, kv_dim)], axis=1)
            lk = seq + n_ctx
            tk = max(t for t in range(LANES, ATTN_MAX_TK + 1, LANES) if lk % t == 0)
            o = _attention(q.reshape(bsz, seq, -1), k_all, v_all, tq=min(256, seq), tk=tk)
            xl = _proj_residual(o.reshape(t_lat, -1), xl, mod, attn_w_out[j], **lat)
            if ctx_full:
                raise NotImplementedError("context-stream attention output is only needed before the last reader layer")
        mp = (moe_w_group[i], moe_b_group[i], moe_w_expert[i], moe_b_expert[i], moe_w_gate, moe_w_up, moe_w_down, i)
        if ctx_full:
            xa = _hier_moe_residual(jnp.concatenate([xl, xc], axis=0), g2, mod, *mp, **both)
            xl, xc = xa[:t_lat], xa[t_lat:]
        else:
            xl = _hier_moe_residual(xl, g2, mod, *mp, **lat)
    return xl.reshape(bsz, seq, d)
```

```python
import functools
import math

import jax
import jax.numpy as jnp
from jax import lax
from jax.experimental import pallas as pl
from jax.experimental.pallas import tpu as pltpu

F32 = jnp.float32
BF16 = jnp.bfloat16
HIGHEST = lax.Precision.HIGHEST

EPS = 1e-6
GRID_W = 64
N_MIXERS = 3
CONV_WIDTH = 31
CONV_HALO = 16
CONV_ROW_CHUNK = 32
CONV_COL_CHUNK = 256
SUBLANES = 8
FNET_GROUPS = 4
HEAD_DIM = 128
N_KV_HEADS = 2
Q_PER_KV = 4
ROPE_THETA = 10000.0
N_EXPERT_GROUPS = 4
EXPERTS_PER_GROUP = 8
N_EXPERTS = N_EXPERT_GROUPS * EXPERTS_PER_GROUP
MOE_BLOCK = 256
LANES = 128
DFT_NA = 128
MOD_ROWS = 8
VMEM_LIMIT = 56 * 1024 * 1024


def _cparams(*sem):
    return pltpu.CompilerParams(dimension_semantics=sem, vmem_limit_bytes=VMEM_LIMIT)


def _rms(x, g):
    return x * lax.rsqrt(jnp.mean(x * x, axis=-1, keepdims=True) + EPS) * g


def _sigmoid(x):
    return 1.0 / (1.0 + jnp.exp(-x))


def _mod_row(mod_ref, b, k, d):
    return mod_ref[pl.ds(b, 1), pl.ds(k * d, d)]


def _norm_mod(x, g, mod_ref, b, k_shift, d):
    return _rms(x, g) * (1.0 + _mod_row(mod_ref, b, k_shift + 1, d)) + _mod_row(mod_ref, b, k_shift, d)


def _batch_row(bpb, n_lat_blocks):
    i = pl.program_id(0)
    return jnp.where(i < n_lat_blocks, i // bpb, MOD_ROWS // 2)


def _mod_kernel(c_ref, w_ref, b_ref, o_ref):
    c = c_ref[...]
    s = c * _sigmoid(c)
    o_ref[0] = jnp.dot(s, w_ref[0], precision=HIGHEST, preferred_element_type=F32) + b_ref[0]


def _modulation(cvec, w_mod, b_mod):
    depth, d, n = w_mod.shape
    tn = 1024
    return pl.pallas_call(
        _mod_kernel,
        grid=(depth, n // tn),
        in_specs=[
            pl.BlockSpec((MOD_ROWS, d), lambda l, j: (0, 0)),
            pl.BlockSpec((1, d, tn), lambda l, j: (l, 0, j)),
            pl.BlockSpec((1, 1, tn), lambda l, j: (l, 0, j)),
        ],
        out_specs=pl.BlockSpec((1, MOD_ROWS, tn), lambda l, j: (l, 0, j)),
        out_shape=jax.ShapeDtypeStruct((depth, MOD_ROWS, n), F32),
        compiler_params=_cparams("parallel", "parallel"),
        name="adaln_mod",
    )(cvec, w_mod, b_mod.reshape(depth, 1, n))


def _conv_in_kernel(x_ref, g_ref, mod_ref, w_ref, b_ref, v_ref, *, bpb, n_lat):
    d = x_ref.shape[1]
    b = _batch_row(bpb, n_lat)
    h = _norm_mod(x_ref[...], g_ref[...], mod_ref, b, 0, d)
    u = jnp.dot(h.astype(BF16), w_ref[...], preferred_element_type=F32) + b_ref[...]
    v_ref[...] = u[:, :d] * _sigmoid(u[:, d:])


def _conv_in(x, g, mod, w_in, b_in, *, tm, bpb, n_lat):
    t, d = x.shape
    return pl.pallas_call(
        functools.partial(_conv_in_kernel, bpb=bpb, n_lat=n_lat),
        grid=(t // tm,),
        in_specs=[
            pl.BlockSpec((tm, d), lambda i: (i, 0)),
            pl.BlockSpec((1, d), lambda i: (0, 0)),
            pl.BlockSpec(mod.shape, lambda i: (0, 0)),
            pl.BlockSpec((d, 2 * d), lambda i: (0, 0)),
            pl.BlockSpec((1, 2 * d), lambda i: (0, 0)),
        ],
        out_specs=pl.BlockSpec((tm, d), lambda i: (i, 0)),
        out_shape=jax.ShapeDtypeStruct((t, d), F32),
        compiler_params=_cparams("parallel"),
        name="conv_in",
    )(x, g.reshape(1, d), mod, w_in.astype(BF16), b_in.reshape(1, 2 * d))


def _conv_out_kernel(vp_ref, vc_ref, vn_ref, x_ref, wdw_ref, bdw_ref, gn_ref, wo_ref, bo_ref, mod_ref,
                     o_ref, buf, z, sh, *, bpb, n_lat, tm):
    d = x_ref.shape[1]
    i = pl.program_id(0)
    j = i % bpb
    b = _batch_row(bpb, n_lat)
    h = CONV_HALO
    buf[0:h, :] = jnp.where(j == 0, 0.0, vp_ref[...])
    buf[h:h + tm, :] = vc_ref[...]
    buf[h + tm:, :] = jnp.where(j == bpb - 1, 0.0, vn_ref[...])
    rc, cw = CONV_ROW_CHUNK, CONV_COL_CHUNK
    off = h - CONV_WIDTH // 2
    n_sh = sh.shape[1]

    def col_chunk(ci, carry):
        c0 = pl.multiple_of(ci * cw, cw)
        for s in range(SUBLANES):
            sh[s] = buf[s:s + n_sh, pl.ds(c0, cw)]
        for r in range(tm // rc):
            acc = jnp.zeros((rc, cw), F32)
            for t in range(CONV_WIDTH):
                q, s = divmod(off + t, SUBLANES)
                row0 = r * rc + q * SUBLANES
                acc = acc + sh[s, row0:row0 + rc, :] * wdw_ref[t:t + 1, pl.ds(c0, cw)]
            z[r * rc:(r + 1) * rc, pl.ds(c0, cw)] = acc
        return carry

    lax.fori_loop(0, d // cw, col_chunk, 0)
    zz = _rms(z[...] + bdw_ref[...], gn_ref[...])
    zz = zz * _sigmoid(zz)
    y = jnp.dot(zz.astype(BF16), wo_ref[...], preferred_element_type=F32) + bo_ref[...]
    o_ref[...] = x_ref[...] + _mod_row(mod_ref, b, 2, d) * y


def _conv_out(v, x, mod, w_dw, b_dw, g_norm, w_out, b_out, *, tm, bpb, n_lat):
    t, d = x.shape
    hb = tm // CONV_HALO
    n_halo = t // CONV_HALO
    wdw = jnp.zeros((32, d), F32).at[:CONV_WIDTH].set(w_dw)
    n_tap_tiles = (CONV_HALO - CONV_WIDTH // 2 + CONV_WIDTH - 1) // SUBLANES
    return pl.pallas_call(
        functools.partial(_conv_out_kernel, bpb=bpb, n_lat=n_lat, tm=tm),
        grid=(t // tm,),
        in_specs=[
            pl.BlockSpec((CONV_HALO, d), lambda i: (jnp.maximum(i * hb - 1, 0), 0)),
            pl.BlockSpec((tm, d), lambda i: (i, 0)),
            pl.BlockSpec((CONV_HALO, d), lambda i: (jnp.minimum((i + 1) * hb, n_halo - 1), 0)),
            pl.BlockSpec((tm, d), lambda i: (i, 0)),
            pl.BlockSpec((32, d), lambda i: (0, 0)),
            pl.BlockSpec((1, d), lambda i: (0, 0)),
            pl.BlockSpec((1, d), lambda i: (0, 0)),
            pl.BlockSpec((d, d), lambda i: (0, 0)),
            pl.BlockSpec((1, d), lambda i: (0, 0)),
            pl.BlockSpec(mod.shape, lambda i: (0, 0)),
        ],
        out_specs=pl.BlockSpec((tm, d), lambda i: (i, 0)),
        out_shape=jax.ShapeDtypeStruct((t, d), F32),
        scratch_shapes=[
            pltpu.VMEM((tm + 2 * CONV_HALO, d), F32),
            pltpu.VMEM((tm, d), F32),
            pltpu.VMEM((SUBLANES, tm + n_tap_tiles * SUBLANES, CONV_COL_CHUNK), F32),
        ],
        compiler_params=_cparams("parallel"),
        name="conv_out",
    )(v, v, v, x, wdw, b_dw.reshape(1, d), g_norm.reshape(1, d), w_out.astype(BF16), b_out.reshape(1, d), mod)


def _prenorm_kernel(x_ref, g_ref, mod_ref, h_ref, *, bpb, n_lat):
    d = x_ref.shape[1]
    b = _batch_row(bpb, n_lat)
    h_ref[...] = _norm_mod(x_ref[...], g_ref[...], mod_ref, b, 0, d).astype(h_ref.dtype)


def _prenorm(x, g, mod, *, tm, bpb, n_lat, dtype):
    t, d = x.shape
    return pl.pallas_call(
        functools.partial(_prenorm_kernel, bpb=bpb, n_lat=n_lat),
        grid=(t // tm,),
        in_specs=[
            pl.BlockSpec((tm, d), lambda i: (i, 0)),
            pl.BlockSpec((1, d), lambda i: (0, 0)),
            pl.BlockSpec(mod.shape, lambda i: (0, 0)),
        ],
        out_specs=pl.BlockSpec((tm, d), lambda i: (i, 0)),
        out_shape=jax.ShapeDtypeStruct((t, d), dtype),
        compiler_params=_cparams("parallel"),
        name="prenorm",
    )(x, g.reshape(1, d), mod)


def _dft1_kernel(h_ref, m_ref, o_ref, *, nb_chunk, nb, na):
    bc = pl.program_id(2)
    for bi in range(nb_chunk):
        b = bc * nb_chunk + bi
        xs = h_ref[0, pl.ds(b, na, stride=nb), :]
        t = jnp.dot(m_ref[bi], xs.astype(BF16), preferred_element_type=F32)
        o_ref[0, 0, bi] = t[:na].astype(BF16)
        o_ref[0, 1, bi] = t[na:].astype(BF16)


def _dft_tables(seq):
    na, nb = DFT_NA, seq // DFT_NA
    ka = jnp.arange(na, dtype=jnp.int32)
    a = jnp.arange(na, dtype=jnp.int32)
    b = jnp.arange(nb, dtype=jnp.int32)
    n = a[None, None, :] * nb + b[:, None, None]
    ang = ((ka[None, :, None] * n) % seq).astype(F32) * (2.0 * math.pi / seq)
    m1 = jnp.concatenate([jnp.cos(ang), -jnp.sin(ang)], axis=1).astype(BF16)
    kb = jnp.arange(nb, dtype=jnp.int32)
    ang2 = ((kb[:, None] * b[None, :]) % nb).astype(F32) * (2.0 * math.pi / nb)
    c2, s2 = jnp.cos(ang2), jnp.sin(ang2)
    m2 = jnp.concatenate([jnp.concatenate([c2, s2], axis=1),
                          jnp.concatenate([s2, -c2], axis=1)], axis=0).astype(BF16)
    return m1, m2


def _const_lhs_matmul_kernel(a_ref, x_ref, o_ref):
    o_ref[0] = jnp.dot(a_ref[...], x_ref[0], preferred_element_type=F32).astype(o_ref.dtype)


def _const_lhs_matmul(a, x, *, cn):
    m, k = a.shape
    bsz, _, n = x.shape
    return pl.pallas_call(
        _const_lhs_matmul_kernel,
        grid=(bsz, n // cn),
        in_specs=[
            pl.BlockSpec((m, k), lambda b, j: (0, 0)),
            pl.BlockSpec((1, k, cn), lambda b, j: (b, 0, j)),
        ],
        out_specs=pl.BlockSpec((1, m, cn), lambda b, j: (b, 0, j)),
        out_shape=jax.ShapeDtypeStruct((bsz, m, n), BF16),
        compiler_params=_cparams("parallel", "parallel"),
        name="const_lhs_matmul",
    )(a, x)


def _seq_dft_two_stage(h, bsz, seq):
    d = h.shape[1]
    na, nb = DFT_NA, seq // DFT_NA
    gw = LANES
    nb_chunk = min(nb, 8)
    m1, m2 = _dft_tables(seq)
    t1 = pl.pallas_call(
        functools.partial(_dft1_kernel, nb_chunk=nb_chunk, nb=nb, na=na),
        grid=(bsz, d // gw, nb // nb_chunk),
        in_specs=[
            pl.BlockSpec((1, seq, gw), lambda b, g, c: (b, 0, g)),
            pl.BlockSpec((nb_chunk, 2 * na, na), lambda b, g, c: (c, 0, 0)),
        ],
        out_specs=pl.BlockSpec((1, 2, nb_chunk, na, gw), lambda b, g, c: (b, 0, c, 0, g)),
        out_shape=jax.ShapeDtypeStruct((bsz, 2, nb, na, d), BF16),
        compiler_params=_cparams("parallel", "parallel", "arbitrary"),
        name="dft_stage1",
    )(h.reshape(bsz, seq, d), m1)
    pq = _const_lhs_matmul(m2, t1.reshape(bsz, 2 * nb, na * d), cn=min(na * d, 8192))
    return pq.reshape(bsz, 2, seq, d)


def _seq_dft_dense(h, bsz, seq):
    d = h.shape[1]
    k = jnp.arange(seq, dtype=jnp.int32)
    ang = ((k[:, None] * k[None, :]) % seq).astype(F32) * (2.0 * math.pi / seq)
    a = jnp.concatenate([jnp.cos(ang), jnp.sin(ang)], axis=0).astype(BF16)
    pq = _const_lhs_matmul(a, h.reshape(bsz, seq, d), cn=d)
    return pq.reshape(bsz, 2, seq, d)


def _fourier_out_kernel(p_ref, q_ref, x_ref, cm_ref, wo_ref, bo_ref, mod_ref, o_ref, *, bpb, n_lat):
    d = x_ref.shape[1]
    gw = d // FNET_GROUPS
    b = _batch_row(bpb, n_lat)
    p, q = p_ref[0, 0], q_ref[0, 0]
    f = [jnp.dot(p[:, g * gw:(g + 1) * gw], cm_ref[:gw], preferred_element_type=F32)
         + jnp.dot(q[:, g * gw:(g + 1) * gw], cm_ref[gw:], preferred_element_type=F32)
         for g in range(FNET_GROUPS)]
    f = jnp.concatenate(f, axis=1).astype(BF16)
    y = jnp.dot(f, wo_ref[...], preferred_element_type=F32) + bo_ref[...]
    o_ref[...] = x_ref[...] + _mod_row(mod_ref, b, 2, d) * y


def _fourier_out(pq, x, mod, w_out, b_out, *, seq, tm, bpb, n_lat):
    t, d = x.shape
    gw = d // FNET_GROUPS
    k = jnp.arange(gw, dtype=jnp.int32)
    ang = ((k[:, None] * k[None, :]) % gw).astype(F32) * (2.0 * math.pi / gw)
    scale = 1.0 / math.sqrt(seq * gw)
    cm = (jnp.concatenate([jnp.cos(ang), -jnp.sin(ang)], axis=0) * scale).astype(BF16)
    return pl.pallas_call(
        functools.partial(_fourier_out_kernel, bpb=bpb, n_lat=n_lat),
        grid=(t // tm,),
        in_specs=[
            pl.BlockSpec((1, 1, tm, d), lambda i: (i // bpb, 0, i % bpb, 0)),
            pl.BlockSpec((1, 1, tm, d), lambda i: (i // bpb, 1, i % bpb, 0)),
            pl.BlockSpec((tm, d), lambda i: (i, 0)),
            pl.BlockSpec((2 * gw, gw), lambda i: (0, 0)),
            pl.BlockSpec((d, d), lambda i: (0, 0)),
            pl.BlockSpec((1, d), lambda i: (0, 0)),
            pl.BlockSpec(mod.shape, lambda i: (0, 0)),
        ],
        out_specs=pl.BlockSpec((tm, d), lambda i: (i, 0)),
        out_shape=jax.ShapeDtypeStruct((t, d), F32),
        compiler_params=_cparams("parallel"),
        name="fourier_out",
    )(pq, pq, x, cm, w_out.astype(BF16), b_out.reshape(1, d), mod)


def _head_perm():
    return jnp.concatenate([jnp.arange(0, HEAD_DIM, 2), jnp.arange(1, HEAD_DIM, 2)])


def _rope_tables(seq):
    rows = seq // GRID_W
    row = jnp.repeat(jnp.arange(rows, dtype=F32), GRID_W)
    col = jnp.tile(jnp.arange(GRID_W, dtype=F32), rows)
    n_pairs_axis = HEAD_DIM // 4
    inv = ROPE_THETA ** (-jnp.arange(n_pairs_axis, dtype=F32) / n_pairs_axis)
    ang = jnp.concatenate([row[:, None] * inv, col[:, None] * inv], axis=-1)
    cos, sin = jnp.cos(ang), jnp.sin(ang)
    return jnp.concatenate([cos, cos], axis=-1), jnp.concatenate([-sin, sin], axis=-1)


def _qkv_kernel(x_ref, g_ref, mod_ref, w_ref, gq_ref, gk_ref, cos_ref, sin_ref, qt_ref, k_ref, v_ref, *, bpb, n_lat):
    d = x_ref.shape[1]
    b = _batch_row(bpb, n_lat)
    h = _norm_mod(x_ref[...], g_ref[...], mod_ref, b, 0, d)
    u = jnp.dot(h.astype(BF16), w_ref[...], preferred_element_type=F32)
    cos, sin = cos_ref[...], sin_ref[...]
    kv_dim = k_ref.shape[1]
    q_dim = Q_PER_KV * kv_dim
    tq = qt_ref.shape[4] // Q_PER_KV

    def norm_rope(xh, gain):
        xh = _rms(xh, gain)
        return xh * cos + pltpu.roll(xh, HEAD_DIM // 2, 1) * sin

    q_scale = HEAD_DIM ** -0.5 * math.log2(math.e)
    for hh in range(q_dim // HEAD_DIM):
        sl = slice(hh * HEAD_DIM, (hh + 1) * HEAD_DIM)
        kvh, g = divmod(hh, Q_PER_KV)
        q_t = (norm_rope(u[:, sl], gq_ref[...]) * q_scale).T.astype(BF16)
        for qb in range(qt_ref.shape[2]):
            qt_ref[0, kvh, qb, :, g * tq:(g + 1) * tq] = q_t[:, qb * tq:(qb + 1) * tq]
    for hh in range(kv_dim // HEAD_DIM):
        sl = slice(hh * HEAD_DIM, (hh + 1) * HEAD_DIM)
        k_ref[:, sl] = norm_rope(u[:, q_dim + hh * HEAD_DIM:q_dim + (hh + 1) * HEAD_DIM], gk_ref[...]).astype(BF16)
    v_ref[...] = u[:, q_dim + kv_dim:].astype(BF16)


def _qkv(x, g, mod, w_qkv, gq, gk, cos, sin, *, seq, tq, tm, bpb, n_lat):
    t, d = x.shape
    assert tm % tq == 0 and seq % tm == 0
    q_dim = N_KV_HEADS * Q_PER_KV * HEAD_DIM
    kv_dim = N_KV_HEADS * HEAD_DIM
    perm = _head_perm()
    n_heads_qk = (q_dim + kv_dim) // HEAD_DIM
    cols = (jnp.arange(n_heads_qk)[:, None] * HEAD_DIM + perm[None, :]).reshape(-1)
    cols = jnp.concatenate([cols, jnp.arange(q_dim + kv_dim, q_dim + 2 * kv_dim)])
    w = w_qkv[:, cols].astype(BF16)
    n_pos = cos.shape[0] // tm
    return pl.pallas_call(
        functools.partial(_qkv_kernel, bpb=bpb, n_lat=n_lat),
        grid=(t // tm,),
        in_specs=[
            pl.BlockSpec((tm, d), lambda i: (i, 0)),
            pl.BlockSpec((1, d), lambda i: (0, 0)),
            pl.BlockSpec(mod.shape, lambda i: (0, 0)),
            pl.BlockSpec((d, q_dim + 2 * kv_dim), lambda i: (0, 0)),
            pl.BlockSpec((1, HEAD_DIM), lambda i: (0, 0)),
            pl.BlockSpec((1, HEAD_DIM), lambda i: (0, 0)),
            pl.BlockSpec((tm, HEAD_DIM), lambda i: (i % n_pos, 0)),
            pl.BlockSpec((tm, HEAD_DIM), lambda i: (i % n_pos, 0)),
        ],
        out_specs=[
            pl.BlockSpec((1, N_KV_HEADS, tm // tq, HEAD_DIM, Q_PER_KV * tq),
                         lambda i: (i // (seq // tm), 0, i % (seq // tm), 0, 0)),
            pl.BlockSpec((tm, kv_dim), lambda i: (i, 0)),
            pl.BlockSpec((tm, kv_dim), lambda i: (i, 0)),
        ],
        out_shape=[
            jax.ShapeDtypeStruct((t // seq, N_KV_HEADS, seq // tq, HEAD_DIM, Q_PER_KV * tq), BF16),
            jax.ShapeDtypeStruct((t, kv_dim), BF16),
            jax.ShapeDtypeStruct((t, kv_dim), BF16),
        ],
        compiler_params=_cparams("parallel"),
        name="qkv_proj",
    )(x, g.reshape(1, d), mod, w, gq[perm].reshape(1, HEAD_DIM), gk[perm].reshape(1, HEAD_DIM), cos, sin)


ATTN_TQ = 256
ATTN_ROW_CHUNK = 16
ATTN_MAX_TK = 1408


def _attn_kernel(qt_ref, k_ref, vt_ref, o_ref, *scratch, tq, tk, nk):
    s_ref, p_ref, acc_ref = scratch
    nq = Q_PER_KV * tq
    acc_ref[...] = jnp.zeros(acc_ref.shape, F32)
    rc = ATTN_ROW_CHUNK

    def kv_chunk(j, carry):
        m_prev, l_prev = carry
        kk = k_ref[0, pl.ds(pl.multiple_of(j * tk, tk), tk), :]
        s_ref[...] = jnp.dot(kk, qt_ref[0, 0, 0], preferred_element_type=F32)
        mx = s_ref[0:SUBLANES, :]
        for r in range(1, tk // SUBLANES):
            mx = jnp.maximum(mx, s_ref[r * SUBLANES:(r + 1) * SUBLANES, :])
        m_new = jnp.maximum(m_prev, jnp.max(mx, axis=0, keepdims=True))
        psum = jnp.zeros((SUBLANES, nq), F32)
        for r in range(tk // rc):
            p = jnp.exp2(s_ref[r * rc:(r + 1) * rc, :] - m_new)
            p_ref[r * rc:(r + 1) * rc, :] = p.astype(BF16)
            for h in range(rc // SUBLANES):
                psum = psum + p[h * SUBLANES:(h + 1) * SUBLANES]
        alpha = jnp.exp2(m_prev - m_new)
        acc_ref[...] = alpha * acc_ref[...] + jnp.dot(vt_ref[0, 0, j], p_ref[...], preferred_element_type=F32)
        return m_new, alpha * l_prev + jnp.sum(psum, axis=0, keepdims=True)

    init = (jnp.full((1, nq), -jnp.inf, F32), jnp.zeros((1, nq), F32))
    _, l_all = lax.fori_loop(0, nk, kv_chunk, init)
    out = (acc_ref[...] / l_all).T
    for g in range(Q_PER_KV):
        o_ref[0, :, g * HEAD_DIM:(g + 1) * HEAD_DIM] = out[g * tq:(g + 1) * tq].astype(o_ref.dtype)


def _attention(qt, k_all, v_all, *, tk):
    bsz, _, n_qblk, _, nq = qt.shape
    tq = nq // Q_PER_KV
    seq = n_qblk * tq
    lk = k_all.shape[1]
    gq = Q_PER_KV * HEAD_DIM
    q_dim = N_KV_HEADS * gq
    nk = lk // tk
    assert tk % ATTN_ROW_CHUNK == 0 and nq % LANES == 0
    vt = v_all.reshape(bsz, nk, tk, N_KV_HEADS, HEAD_DIM).transpose(0, 3, 1, 4, 2)
    return pl.pallas_call(
        functools.partial(_attn_kernel, tq=tq, tk=tk, nk=nk),
        grid=(bsz, N_KV_HEADS, seq // tq),
        in_specs=[
            pl.BlockSpec((1, 1, 1, HEAD_DIM, nq), lambda b, h, i: (b, h, i, 0, 0)),
            pl.BlockSpec((1, lk, HEAD_DIM), lambda b, h, i: (b, 0, h)),
            pl.BlockSpec((1, 1, nk, HEAD_DIM, tk), lambda b, h, i: (b, h, 0, 0, 0)),
        ],
        out_specs=pl.BlockSpec((1, tq, gq), lambda b, h, i: (b, i, h)),
        out_shape=jax.ShapeDtypeStruct((bsz, seq, q_dim), BF16),
        scratch_shapes=[pltpu.VMEM((tk, nq), F32), pltpu.VMEM((tk, nq), BF16), pltpu.VMEM((HEAD_DIM, nq), F32)],
        compiler_params=_cparams("parallel", "parallel", "parallel"),
        name="flash_attention",
    )(qt, k_all, vt)


def _proj_residual_kernel(a_ref, x_ref, w_ref, mod_ref, o_ref, *, bpb, n_lat):
    d = x_ref.shape[1]
    b = _batch_row(bpb, n_lat)
    y = jnp.dot(a_ref[...], w_ref[...], preferred_element_type=F32)
    o_ref[...] = x_ref[...] + _mod_row(mod_ref, b, 2, d) * y


def _proj_residual(a, x, mod, w, *, tm, bpb, n_lat):
    t, d = x.shape
    ka = a.shape[1]
    return pl.pallas_call(
        functools.partial(_proj_residual_kernel, bpb=bpb, n_lat=n_lat),
        grid=(t // tm,),
        in_specs=[
            pl.BlockSpec((tm, ka), lambda i: (i, 0)),
            pl.BlockSpec((tm, d), lambda i: (i, 0)),
            pl.BlockSpec((ka, d), lambda i: (0, 0)),
            pl.BlockSpec(mod.shape, lambda i: (0, 0)),
        ],
        out_specs=pl.BlockSpec((tm, d), lambda i: (i, 0)),
        out_shape=jax.ShapeDtypeStruct((t, d), F32),
        compiler_params=_cparams("parallel"),
        name="proj_residual",
    )(a, x, w.astype(BF16), mod)


ROUTE_E, ROUTE_GATE, ROUTE_RANK = 0, 2, 4
ROUTER_LOGIT0 = N_EXPERT_GROUPS


U32 = jnp.uint32
BF16_HIGH_BITS = 0xFFFF0000


def _pack_bf16_pairs(x):
    half = x.shape[1] // 2
    hi = lax.bitcast_convert_type(x[:, :half].astype(BF16).astype(F32), U32)
    lo = lax.bitcast_convert_type(x[:, half:].astype(BF16).astype(F32), U32)
    return (hi & U32(BF16_HIGH_BITS)) | (lo >> 16)


def _unpack_bf16_pairs(w):
    return (lax.bitcast_convert_type(w & U32(BF16_HIGH_BITS), F32), lax.bitcast_convert_type(w << 16, F32))


def _router_kernel(x_ref, g_ref, mod_ref, wr_ref, br_ref, tri_ref, h_ref, r_ref, cnt_ref, carry, *, bpb, n_lat):
    d = x_ref.shape[1]
    tm = x_ref.shape[0]
    i = pl.program_id(0)
    b = _batch_row(bpb, n_lat)

    @pl.when(i == 0)
    def _():
        carry[...] = jnp.zeros(carry.shape, F32)

    h = _norm_mod(x_ref[...], g_ref[...], mod_ref, b, 3, d)
    h_ref[...] = _pack_bf16_pairs(h)
    h_hi = h.astype(BF16)
    h_lo = (h - h_hi.astype(F32)).astype(BF16)
    logits = (jnp.dot(h_hi, wr_ref[0], preferred_element_type=F32) + jnp.dot(h_lo, wr_ref[0], preferred_element_type=F32)
              + jnp.dot(h_hi, wr_ref[1], preferred_element_type=F32) + br_ref[...])
    lane = lax.broadcasted_iota(jnp.int32, (tm, LANES), 1).astype(F32)
    big = float(LANES)
    neg = -jnp.inf
    gmask = lane < N_EXPERT_GROUPS
    gl = jnp.where(gmask, logits, neg)
    ge = jnp.exp(gl - jnp.max(gl, axis=-1, keepdims=True))
    gp = ge / jnp.sum(ge, axis=-1, keepdims=True)
    g_top = jnp.max(gp, axis=-1, keepdims=True)
    g_idx = jnp.min(jnp.where(gmask & (gp == g_top), lane, big), axis=-1, keepdims=True)
    lo = ROUTER_LOGIT0 + g_idx * EXPERTS_PER_GROUP
    emask = (lane >= lo) & (lane < lo + EXPERTS_PER_GROUP)
    el = jnp.where(emask, logits, neg)
    ee = jnp.exp(el - jnp.max(el, axis=-1, keepdims=True))
    ep = ee / jnp.sum(ee, axis=-1, keepdims=True)
    p1 = jnp.max(ep, axis=-1, keepdims=True)
    i1 = jnp.min(jnp.where(emask & (ep == p1), lane, big), axis=-1, keepdims=True)
    rest = emask & (lane != i1)
    p2 = jnp.max(jnp.where(rest, ep, -1.0), axis=-1, keepdims=True)
    i2 = jnp.min(jnp.where(rest & (ep == p2), lane, big), axis=-1, keepdims=True)
    denom = p1 + p2
    gate1 = g_top * p1 / denom
    gate2 = g_top * p2 / denom
    hit1 = lane == i1
    hit2 = lane == i2
    onehot = jnp.where(hit1 | hit2, 1.0, 0.0)
    cum = jnp.dot(tri_ref[...], onehot.astype(BF16), preferred_element_type=F32) + carry[...]
    rank1 = jnp.sum(jnp.where(hit1, cum, 0.0), axis=-1, keepdims=True)
    rank2 = jnp.sum(jnp.where(hit2, cum, 0.0), axis=-1, keepdims=True)
    carry[...] = carry[...] + jnp.sum(onehot, axis=0, keepdims=True)
    cnt_ref[...] = carry[...]
    rec = jnp.zeros((tm, LANES), F32)
    for ln, val in ((ROUTE_E, i1 - ROUTER_LOGIT0), (ROUTE_E + 1, i2 - ROUTER_LOGIT0), (ROUTE_GATE, gate1),
                    (ROUTE_GATE + 1, gate2), (ROUTE_RANK, rank1), (ROUTE_RANK + 1, rank2)):
        rec = jnp.where(lane == float(ln), val, rec)
    r_ref[...] = rec


def _router(x, g, mod, w_group, b_group, w_expert, b_expert, *, tm, bpb, n_lat):
    t, d = x.shape
    n_log = N_EXPERT_GROUPS + N_EXPERTS
    wr = jnp.zeros((d, LANES), F32).at[:, :N_EXPERT_GROUPS].set(w_group).at[:, N_EXPERT_GROUPS:n_log].set(w_expert)
    br = jnp.zeros((1, LANES), F32).at[0, :N_EXPERT_GROUPS].set(b_group).at[0, N_EXPERT_GROUPS:n_log].set(b_expert)
    wr_hi = wr.astype(BF16)
    wr = jnp.stack([wr_hi, (wr - wr_hi.astype(F32)).astype(BF16)])
    tri = (jnp.arange(tm)[:, None] > jnp.arange(tm)[None, :]).astype(BF16)
    return pl.pallas_call(
        functools.partial(_router_kernel, bpb=bpb, n_lat=n_lat),
        grid=(t // tm,),
        in_specs=[
            pl.BlockSpec((tm, d), lambda i: (i, 0)),
            pl.BlockSpec((1, d), lambda i: (0, 0)),
            pl.BlockSpec(mod.shape, lambda i: (0, 0)),
            pl.BlockSpec((2, d, LANES), lambda i: (0, 0, 0)),
            pl.BlockSpec((1, LANES), lambda i: (0, 0)),
            pl.BlockSpec((tm, tm), lambda i: (0, 0)),
        ],
        out_specs=[
            pl.BlockSpec((tm, d // 2), lambda i: (i, 0)),
            pl.BlockSpec((tm, LANES), lambda i: (i, 0)),
            pl.BlockSpec((1, LANES), lambda i: (0, 0)),
        ],
        out_shape=[
            jax.ShapeDtypeStruct((t, d // 2), U32),
            jax.ShapeDtypeStruct((t, LANES), F32),
            jax.ShapeDtypeStruct((1, LANES), F32),
        ],
        scratch_shapes=[pltpu.VMEM((1, LANES), F32)],
        compiler_params=_cparams("arbitrary"),
        name="moe_router",
    )(x, g.reshape(1, d), mod, wr, br, tri)


def _moe_ffn_kernel(be_ref, bv_ref, x_ref, wg_ref, wu_ref, wd_ref, o_ref, wgb, wub, wdb):
    i = pl.program_id(0)
    e = be_ref[i]
    e_prev = be_ref[jnp.maximum(i - 1, 0)]

    @pl.when((i == 0) | (e != e_prev))
    def _():
        wgb[...] = wg_ref[0, 0].astype(BF16)
        wub[...] = wu_ref[0, 0].astype(BF16)
        wdb[...] = wd_ref[0, 0].astype(BF16)

    n_valid = bv_ref[i]

    @pl.when(n_valid > 0)
    def _():
        half = x_ref.shape[1]
        x_a, x_b = (v.astype(BF16) for v in _unpack_bf16_pairs(x_ref[...]))

        def proj(w):
            return (jnp.dot(x_a, w[:half], preferred_element_type=F32)
                    + jnp.dot(x_b, w[half:], preferred_element_type=F32))

        gt, up = proj(wgb), proj(wub)
        a = gt * _sigmoid(gt) * up
        o_ref[...] = _pack_bf16_pairs(jnp.dot(a.astype(BF16), wdb[...], preferred_element_type=F32))

    @pl.when(n_valid <= 0)
    def _():
        o_ref[...] = jnp.zeros(o_ref.shape, U32)


def _moe_ffn(buf, blk_expert, blk_valid, w_gate, w_up, w_down, layer):
    p = buf.shape[0]
    d = w_gate.shape[-2]
    de = w_gate.shape[-1]
    n_blocks = p // MOE_BLOCK
    grid_spec = pltpu.PrefetchScalarGridSpec(
        num_scalar_prefetch=2,
        grid=(n_blocks,),
        in_specs=[
            pl.BlockSpec((MOE_BLOCK, d // 2), lambda i, be, bv: (i, 0)),
            pl.BlockSpec((1, 1, d, de), lambda i, be, bv: (layer, be[i], 0, 0)),
            pl.BlockSpec((1, 1, d, de), lambda i, be, bv: (layer, be[i], 0, 0)),
            pl.BlockSpec((1, 1, de, d), lambda i, be, bv: (layer, be[i], 0, 0)),
        ],
        out_specs=pl.BlockSpec((MOE_BLOCK, d // 2), lambda i, be, bv: (i, 0)),
        scratch_shapes=[pltpu.VMEM((d, de), BF16), pltpu.VMEM((d, de), BF16), pltpu.VMEM((de, d), BF16)],
    )
    return pl.pallas_call(
        _moe_ffn_kernel,
        grid_spec=grid_spec,
        out_shape=jax.ShapeDtypeStruct((p, d // 2), U32),
        compiler_params=_cparams("arbitrary"),
        name="moe_ffn",
    )(blk_expert, blk_valid, buf, w_gate, w_up, w_down)


DMA_ISSUE_UNROLL = 8


def _row_copy(src, src_row, dst, dst_row, sem):
    return pltpu.make_async_copy(src.at[pl.ds(src_row, 1)], dst.at[pl.ds(dst_row, 1)], sem)


def _moe_dispatch_kernel(pe_ref, dest_ref, h_ref, buf_ref, zero_ref, sems, zsem):
    tm = h_ref.shape[0]
    n_blocks = buf_ref.shape[0] // MOE_BLOCK

    def zero_block(first_row):
        return pltpu.make_async_copy(zero_ref, buf_ref.at[pl.ds(pl.multiple_of(first_row, MOE_BLOCK), MOE_BLOCK)], zsem)

    @pl.when(pl.program_id(0) == 0)
    def _():
        zero_ref[...] = jnp.zeros(zero_ref.shape, zero_ref.dtype)

        def expert_has_rows(e):
            return pe_ref[e] > jnp.where(e > 0, pe_ref[jnp.maximum(e - 1, 0)], 0)

        def fill(start):
            def body(e, carry):
                @pl.when(expert_has_rows(e))
                def _():
                    cp = zero_block(pe_ref[e] - MOE_BLOCK)
                    cp.start() if start else cp.wait()
                return carry
            return body

        def tail(start):
            def body(b, carry):
                cp = zero_block(b * MOE_BLOCK)
                cp.start() if start else cp.wait()
                return carry
            return body

        first_unused = pe_ref[N_EXPERTS - 1] // MOE_BLOCK
        for start in (True, False):
            lax.fori_loop(0, N_EXPERTS, fill(start), 0)
            lax.fori_loop(first_unused, n_blocks, tail(start), 0)

    def issue(r, carry):
        for j in range(2):
            _row_copy(h_ref, r, buf_ref, dest_ref[0, 0, 2 * r + j], sems.at[j]).start()
        return carry

    lax.fori_loop(0, tm, issue, 0, unroll=DMA_ISSUE_UNROLL)
    for j in range(2):
        pltpu.make_async_copy(h_ref, buf_ref.at[pl.ds(0, tm)], sems.at[j]).wait()


def _moe_dispatch(h, dest, pad_end, n_slots, *, tm):
    t, d = h.shape
    grid_spec = pltpu.PrefetchScalarGridSpec(
        num_scalar_prefetch=1,
        grid=(t // tm,),
        in_specs=[
            pl.BlockSpec((1, 1, 2 * tm), lambda i, pe: (i, 0, 0), memory_space=pltpu.SMEM),
            pl.BlockSpec((tm, d), lambda i, pe: (i, 0)),
        ],
        out_specs=pl.BlockSpec(memory_space=pl.ANY),
        scratch_shapes=[pltpu.VMEM((MOE_BLOCK, d), h.dtype), pltpu.SemaphoreType.DMA((2,)), pltpu.SemaphoreType.DMA(())],
    )
    return pl.pallas_call(
        _moe_dispatch_kernel,
        grid_spec=grid_spec,
        out_shape=jax.ShapeDtypeStruct((n_slots, d), h.dtype),
        compiler_params=_cparams("arbitrary"),
        name="moe_dispatch",
    )(pad_end.astype(jnp.int32), dest.reshape(t // tm, 1, 2 * tm), h)


def _moe_combine_kernel(dest_ref, x_ref, r_ref, mod_ref, yb_ref, o_ref, y0, y1, sems, *, bpb, n_lat):
    tm, d = x_ref.shape
    b = _batch_row(bpb, n_lat)
    ys = (y0, y1)

    def issue(r, carry):
        for j in range(2):
            _row_copy(yb_ref, dest_ref[0, 0, 2 * r + j], ys[j], r, sems.at[j]).start()
        return carry

    lax.fori_loop(0, tm, issue, 0, unroll=DMA_ISSUE_UNROLL)
    for j in range(2):
        pltpu.make_async_copy(yb_ref.at[pl.ds(0, tm)], ys[j], sems.at[j]).wait()
    r = r_ref[...]
    half = d // 2
    gated = [tuple(v * r[:, ROUTE_GATE + j:ROUTE_GATE + j + 1] for v in _unpack_bf16_pairs(ys[j][...])) for j in range(2)]
    gate = _mod_row(mod_ref, b, 5, d)
    for c in range(2):
        cols = slice(c * half, (c + 1) * half)
        o_ref[:, cols] = x_ref[:, cols] + gate[:, cols] * (gated[0][c] + gated[1][c])


def _moe_combine(x, yb, dest, route, mod, *, tm, bpb, n_lat):
    t, d = x.shape
    row = pl.BlockSpec((tm, d), lambda i: (i, 0))
    return pl.pallas_call(
        functools.partial(_moe_combine_kernel, bpb=bpb, n_lat=n_lat),
        grid=(t // tm,),
        in_specs=[
            pl.BlockSpec((1, 1, 2 * tm), lambda i: (i, 0, 0), memory_space=pltpu.SMEM),
            row,
            pl.BlockSpec((tm, LANES), lambda i: (i, 0)),
            pl.BlockSpec(mod.shape, lambda i: (0, 0)),
            pl.BlockSpec(memory_space=pl.ANY),
        ],
        out_specs=row,
        out_shape=jax.ShapeDtypeStruct((t, d), F32),
        scratch_shapes=[pltpu.VMEM((tm, d // 2), U32), pltpu.VMEM((tm, d // 2), U32), pltpu.SemaphoreType.DMA((2,))],
        compiler_params=_cparams("arbitrary"),
        name="moe_combine",
    )(dest.reshape(t // tm, 1, 2 * tm), x, route, mod, yb)


def _hier_moe_residual(x, g, mod, w_group, b_group, w_expert, b_expert, w_gate, w_up, w_down, layer, *, tm, bpb, n_lat):
    t, d = x.shape
    h, route, counts = _router(x, g, mod, w_group, b_group, w_expert, b_expert, tm=tm, bpb=bpb, n_lat=n_lat)
    counts = counts[0, ROUTER_LOGIT0:ROUTER_LOGIT0 + N_EXPERTS].astype(jnp.int32)
    expert = route[:, ROUTE_E:ROUTE_E + 2].astype(jnp.int32)
    rank = route[:, ROUTE_RANK:ROUTE_RANK + 2].astype(jnp.int32)
    padded = (counts + MOE_BLOCK - 1) // MOE_BLOCK * MOE_BLOCK
    pad_end = jnp.cumsum(padded)
    pad_start = pad_end - padded
    dest = jnp.sum(jnp.where(expert[:, :, None] == jnp.arange(N_EXPERTS), pad_start, 0), axis=-1) + rank
    n_blocks = -(-(2 * t) // MOE_BLOCK) + N_EXPERTS
    blk_first = jnp.arange(n_blocks, dtype=jnp.int32) * MOE_BLOCK
    blk_expert = jnp.minimum(jnp.sum(blk_first[:, None] >= pad_end[None, :], axis=1), N_EXPERTS - 1).astype(jnp.int32)
    blk_rows = jnp.clip((pad_start + counts)[blk_expert] - blk_first, 0, MOE_BLOCK).astype(jnp.int32)
    buf = _moe_dispatch(h, dest, pad_end, n_blocks * MOE_BLOCK, tm=tm)
    yb = _moe_ffn(buf, blk_expert, blk_rows, w_gate, w_up, w_down, layer)
    return _moe_combine(x, yb, dest, route, mod, tm=tm, bpb=bpb, n_lat=n_lat)


def kernel(x, c, ctx, c_ctx, norm_g, w_mod, b_mod, conv_w_in, conv_b_in, conv_w_dw, conv_b_dw, conv_norm_g, conv_w_out, conv_b_out, fnet_w_out, fnet_b_out, attn_w_qkv, attn_q_norm_g, attn_k_norm_g, attn_w_out, moe_w_group, moe_b_group, moe_w_expert, moe_b_expert, moe_w_gate, moe_w_up, moe_w_down):
    bsz, seq, d = x.shape
    n_ctx = ctx.shape[1]
    depth = norm_g.shape[0]
    assert bsz <= MOD_ROWS // 2 and seq % DFT_NA == 0 and seq % GRID_W == 0
    t_lat, t_ctx = bsz * seq, bsz * n_ctx
    tm = min(512, seq)
    tmc = min(tm, n_ctx)
    assert seq % tm == 0 and n_ctx % tmc == 0 and tm % tmc == 0
    lat = dict(tm=tm, bpb=seq // tm, n_lat=t_lat // tm)
    cx = dict(tm=tmc, bpb=1, n_lat=0)
    tmb = tm if t_ctx % tm == 0 else tmc
    both = dict(tm=tmb, bpb=seq // tmb, n_lat=t_lat // tmb)
    last_reader = max([i for i in range(depth) if i % N_MIXERS == 2], default=-1)

    cvec = jnp.zeros((MOD_ROWS, d), F32).at[:bsz].set(c).at[MOD_ROWS // 2].set(c_ctx)
    mod_all = _modulation(cvec, w_mod, b_mod)
    cos, sin = _rope_tables(seq)

    xl = x.reshape(t_lat, d)
    xc = ctx.reshape(t_ctx, d)
    for i in range(depth):
        m, j = i % N_MIXERS, i // N_MIXERS
        ctx_on = i <= last_reader
        ctx_full = i < last_reader
        mod = mod_all[i]
        g1, g2 = norm_g[i, 0], norm_g[i, 1]
        if m == 0:
            cp = (conv_w_dw[j], conv_b_dw[j], conv_norm_g[j], conv_w_out[j], conv_b_out[j])
            v = _conv_in(xl, g1, mod, conv_w_in[j], conv_b_in[j], **lat)
            xl = _conv_out(v, xl, mod, *cp, **lat)
            if ctx_full:
                vc = _conv_in(xc, g1, mod, conv_w_in[j], conv_b_in[j], **cx)
                xc = _conv_out(vc, xc, mod, *cp, tm=n_ctx, bpb=1, n_lat=0)
        elif m == 1:
            hl = _prenorm(xl, g1, mod, dtype=F32, **lat)
            pq = _seq_dft_two_stage(hl, bsz, seq)
            xl = _fourier_out(pq, xl, mod, fnet_w_out[j], fnet_b_out[j], seq=seq, **lat)
            if ctx_full:
                hc = _prenorm(xc, g1, mod, dtype=BF16, **cx)
                pqc = _seq_dft_dense(hc, bsz, n_ctx)
                xc = _fourier_out(pqc, xc, mod, fnet_w_out[j], fnet_b_out[j], seq=n_ctx, tm=n_ctx, bpb=1, n_lat=0)
        else:
            gq, gk = attn_q_norm_g[j], attn_k_norm_g[j]
            qt, k, vv = _qkv(xl, g1, mod, attn_w_qkv[j], gq, gk, cos, sin, seq=seq, tq=min(ATTN_TQ, seq), **lat)
            ones = jnp.ones((tmc, HEAD_DIM), F32)
            _, kc, vc = _qkv(xc, g1, mod, attn_w_qkv[j], gq, gk, ones, jnp.zeros_like(ones),
                             seq=n_ctx, tq=min(ATTN_TQ, tmc), **cx)
            kv_dim = k.shape[1]
            k_all = jnp.concatenate([kc.reshape(bsz, n_ctx, kv_dim), k.reshape(bsz, seq, kv_dim)], axis=1)
            v_all = jnp.concatenate([vc.reshape(bsz, n_ctx, kv_dim), vv.reshape(bsz, seq, kv_dim)], axis=1)
            lk = seq + n_ctx
            tk = max(t for t in range(LANES, ATTN_MAX_TK + 1, LANES) if lk % t == 0)
            o = _attention(qt, k_all, v_all, tk=tk)
            xl = _proj_residual(o.reshape(t_lat, -1), xl, mod, attn_w_out[j], **lat)
            if ctx_full:
                raise NotImplementedError("context-stream attention output is only needed before the last reader layer")
        mp = (moe_w_group[i], moe_b_group[i], moe_w_expert[i], moe_b_expert[i], moe_w_gate, moe_w_up, moe_w_down, i)
        if ctx_full:
            xa = _hier_moe_residual(jnp.concatenate([xl, xc], axis=0), g2, mod, *mp, **both)
            xl, xc = xa[:t_lat], xa[t_lat:]
        else:
            xl = _hier_moe_residual(xl, g2, mod, *mp, **lat)
    return xl.reshape(bsz, seq, d)
```

```python
import functools
import math

import jax
import jax.numpy as jnp
from jax import lax
from jax.experimental import pallas as pl
from jax.experimental.pallas import tpu as pltpu

F32 = jnp.float32
BF16 = jnp.bfloat16
HIGHEST = lax.Precision.HIGHEST

EPS = 1e-6
GRID_W = 64
N_MIXERS = 3
CONV_WIDTH = 31
CONV_HALO = 16
CONV_ROW_CHUNK = 32
CONV_COL_CHUNK = 256
SUBLANES = 8
FNET_GROUPS = 4
HEAD_DIM = 128
N_KV_HEADS = 2
Q_PER_KV = 4
ROPE_THETA = 10000.0
N_EXPERT_GROUPS = 4
EXPERTS_PER_GROUP = 8
N_EXPERTS = N_EXPERT_GROUPS * EXPERTS_PER_GROUP
MOE_BLOCK = 256
LANES = 128
DFT_NA = 128
MOD_ROWS = 8
VMEM_LIMIT = 56 * 1024 * 1024


def _cparams(*sem):
    return pltpu.CompilerParams(dimension_semantics=sem, vmem_limit_bytes=VMEM_LIMIT)


def _rms(x, g):
    return x * lax.rsqrt(jnp.mean(x * x, axis=-1, keepdims=True) + EPS) * g


def _sigmoid(x):
    return 1.0 / (1.0 + jnp.exp(-x))


def _mod_row(mod_ref, b, k, d):
    return mod_ref[pl.ds(b, 1), pl.ds(k * d, d)]


def _norm_mod(x, g, mod_ref, b, k_shift, d):
    return _rms(x, g) * (1.0 + _mod_row(mod_ref, b, k_shift + 1, d)) + _mod_row(mod_ref, b, k_shift, d)


def _batch_row(bpb, n_lat_blocks):
    i = pl.program_id(0)
    return jnp.where(i < n_lat_blocks, i // bpb, MOD_ROWS // 2)


def _mod_kernel(c_ref, w_ref, b_ref, o_ref):
    c = c_ref[...]
    s = c * _sigmoid(c)
    o_ref[0] = jnp.dot(s, w_ref[0], precision=HIGHEST, preferred_element_type=F32) + b_ref[0]


def _modulation(cvec, w_mod, b_mod):
    depth, d, n = w_mod.shape
    tn = 1024
    return pl.pallas_call(
        _mod_kernel,
        grid=(depth, n // tn),
        in_specs=[
            pl.BlockSpec((MOD_ROWS, d), lambda l, j: (0, 0)),
            pl.BlockSpec((1, d, tn), lambda l, j: (l, 0, j)),
            pl.BlockSpec((1, 1, tn), lambda l, j: (l, 0, j)),
        ],
        out_specs=pl.BlockSpec((1, MOD_ROWS, tn), lambda l, j: (l, 0, j)),
        out_shape=jax.ShapeDtypeStruct((depth, MOD_ROWS, n), F32),
        compiler_params=_cparams("parallel", "parallel"),
        name="adaln_mod",
    )(cvec, w_mod, b_mod.reshape(depth, 1, n))


def _conv_in_kernel(x_ref, g_ref, mod_ref, w_ref, b_ref, v_ref, *, bpb, n_lat):
    d = x_ref.shape[1]
    b = _batch_row(bpb, n_lat)
    h = _norm_mod(x_ref[...], g_ref[...], mod_ref, b, 0, d)
    u = jnp.dot(h.astype(BF16), w_ref[...], preferred_element_type=F32) + b_ref[...]
    v_ref[...] = u[:, :d] * _sigmoid(u[:, d:])


def _conv_in(x, g, mod, w_in, b_in, *, tm, bpb, n_lat):
    t, d = x.shape
    return pl.pallas_call(
        functools.partial(_conv_in_kernel, bpb=bpb, n_lat=n_lat),
        grid=(t // tm,),
        in_specs=[
            pl.BlockSpec((tm, d), lambda i: (i, 0)),
            pl.BlockSpec((1, d), lambda i: (0, 0)),
            pl.BlockSpec(mod.shape, lambda i: (0, 0)),
            pl.BlockSpec((d, 2 * d), lambda i: (0, 0)),
            pl.BlockSpec((1, 2 * d), lambda i: (0, 0)),
        ],
        out_specs=pl.BlockSpec((tm, d), lambda i: (i, 0)),
        out_shape=jax.ShapeDtypeStruct((t, d), F32),
        compiler_params=_cparams("parallel"),
        name="conv_in",
    )(x, g.reshape(1, d), mod, w_in.astype(BF16), b_in.reshape(1, 2 * d))


def _conv_out_kernel(vp_ref, vc_ref, vn_ref, x_ref, wdw_ref, bdw_ref, gn_ref, wo_ref, bo_ref, mod_ref,
                     o_ref, buf, z, sh, *, bpb, n_lat, tm):
    d = x_ref.shape[1]
    i = pl.program_id(0)
    j = i % bpb
    b = _batch_row(bpb, n_lat)
    h = CONV_HALO
    buf[0:h, :] = jnp.where(j == 0, 0.0, vp_ref[...])
    buf[h:h + tm, :] = vc_ref[...]
    buf[h + tm:, :] = jnp.where(j == bpb - 1, 0.0, vn_ref[...])
    rc, cw = CONV_ROW_CHUNK, CONV_COL_CHUNK
    off = h - CONV_WIDTH // 2
    n_sh = sh.shape[1]

    def col_chunk(ci, carry):
        c0 = pl.multiple_of(ci * cw, cw)
        for s in range(SUBLANES):
            sh[s] = buf[s:s + n_sh, pl.ds(c0, cw)]
        for r in range(tm // rc):
            acc = jnp.zeros((rc, cw), F32)
            for t in range(CONV_WIDTH):
                q, s = divmod(off + t, SUBLANES)
                row0 = r * rc + q * SUBLANES
                acc = acc + sh[s, row0:row0 + rc, :] * wdw_ref[t:t + 1, pl.ds(c0, cw)]
            z[r * rc:(r + 1) * rc, pl.ds(c0, cw)] = acc
        return carry

    lax.fori_loop(0, d // cw, col_chunk, 0)
    zz = _rms(z[...] + bdw_ref[...], gn_ref[...])
    zz = zz * _sigmoid(zz)
    y = jnp.dot(zz.astype(BF16), wo_ref[...], preferred_element_type=F32) + bo_ref[...]
    o_ref[...] = x_ref[...] + _mod_row(mod_ref, b, 2, d) * y


def _conv_out(v, x, mod, w_dw, b_dw, g_norm, w_out, b_out, *, tm, bpb, n_lat):
    t, d = x.shape
    hb = tm // CONV_HALO
    n_halo = t // CONV_HALO
    wdw = jnp.zeros((32, d), F32).at[:CONV_WIDTH].set(w_dw)
    n_tap_tiles = (CONV_HALO - CONV_WIDTH // 2 + CONV_WIDTH - 1) // SUBLANES
    return pl.pallas_call(
        functools.partial(_conv_out_kernel, bpb=bpb, n_lat=n_lat, tm=tm),
        grid=(t // tm,),
        in_specs=[
            pl.BlockSpec((CONV_HALO, d), lambda i: (jnp.maximum(i * hb - 1, 0), 0)),
            pl.BlockSpec((tm, d), lambda i: (i, 0)),
            pl.BlockSpec((CONV_HALO, d), lambda i: (jnp.minimum((i + 1) * hb, n_halo - 1), 0)),
            pl.BlockSpec((tm, d), lambda i: (i, 0)),
            pl.BlockSpec((32, d), lambda i: (0, 0)),
            pl.BlockSpec((1, d), lambda i: (0, 0)),
            pl.BlockSpec((1, d), lambda i: (0, 0)),
            pl.BlockSpec((d, d), lambda i: (0, 0)),
            pl.BlockSpec((1, d), lambda i: (0, 0)),
            pl.BlockSpec(mod.shape, lambda i: (0, 0)),
        ],
        out_specs=pl.BlockSpec((tm, d), lambda i: (i, 0)),
        out_shape=jax.ShapeDtypeStruct((t, d), F32),
        scratch_shapes=[
            pltpu.VMEM((tm + 2 * CONV_HALO, d), F32),
            pltpu.VMEM((tm, d), F32),
            pltpu.VMEM((SUBLANES, tm + n_tap_tiles * SUBLANES, CONV_COL_CHUNK), F32),
        ],
        compiler_params=_cparams("parallel"),
        name="conv_out",
    )(v, v, v, x, wdw, b_dw.reshape(1, d), g_norm.reshape(1, d), w_out.astype(BF16), b_out.reshape(1, d), mod)


def _prenorm_kernel(x_ref, g_ref, mod_ref, h_ref, *, bpb, n_lat):
    d = x_ref.shape[1]
    b = _batch_row(bpb, n_lat)
    h_ref[...] = _norm_mod(x_ref[...], g_ref[...], mod_ref, b, 0, d).astype(h_ref.dtype)


def _prenorm(x, g, mod, *, tm, bpb, n_lat, dtype):
    t, d = x.shape
    return pl.pallas_call(
        functools.partial(_prenorm_kernel, bpb=bpb, n_lat=n_lat),
        grid=(t // tm,),
        in_specs=[
            pl.BlockSpec((tm, d), lambda i: (i, 0)),
            pl.BlockSpec((1, d), lambda i: (0, 0)),
            pl.BlockSpec(mod.shape, lambda i: (0, 0)),
        ],
        out_specs=pl.BlockSpec((tm, d), lambda i: (i, 0)),
        out_shape=jax.ShapeDtypeStruct((t, d), dtype),
        compiler_params=_cparams("parallel"),
        name="prenorm",
    )(x, g.reshape(1, d), mod)


def _dft1_kernel(h_ref, m_ref, o_ref, *, nb_chunk, nb, na):
    bc = pl.program_id(2)
    for bi in range(nb_chunk):
        b = bc * nb_chunk + bi
        xs = h_ref[0, pl.ds(b, na, stride=nb), :]
        t = jnp.dot(m_ref[bi], xs.astype(BF16), preferred_element_type=F32)
        o_ref[0, 0, bi] = t[:na].astype(BF16)
        o_ref[0, 1, bi] = t[na:].astype(BF16)


def _dft_tables(seq):
    na, nb = DFT_NA, seq // DFT_NA
    ka = jnp.arange(na, dtype=jnp.int32)
    a = jnp.arange(na, dtype=jnp.int32)
    b = jnp.arange(nb, dtype=jnp.int32)
    n = a[None, None, :] * nb + b[:, None, None]
    ang = ((ka[None, :, None] * n) % seq).astype(F32) * (2.0 * math.pi / seq)
    m1 = jnp.concatenate([jnp.cos(ang), -jnp.sin(ang)], axis=1).astype(BF16)
    kb = jnp.arange(nb, dtype=jnp.int32)
    ang2 = ((kb[:, None] * b[None, :]) % nb).astype(F32) * (2.0 * math.pi / nb)
    c2, s2 = jnp.cos(ang2), jnp.sin(ang2)
    m2 = jnp.concatenate([jnp.concatenate([c2, s2], axis=1),
                          jnp.concatenate([s2, -c2], axis=1)], axis=0).astype(BF16)
    return m1, m2


def _const_lhs_matmul_kernel(a_ref, x_ref, o_ref):
    o_ref[0] = jnp.dot(a_ref[...], x_ref[0], preferred_element_type=F32).astype(o_ref.dtype)


def _const_lhs_matmul(a, x, *, cn):
    m, k = a.shape
    bsz, _, n = x.shape
    return pl.pallas_call(
        _const_lhs_matmul_kernel,
        grid=(bsz, n // cn),
        in_specs=[
            pl.BlockSpec((m, k), lambda b, j: (0, 0)),
            pl.BlockSpec((1, k, cn), lambda b, j: (b, 0, j)),
        ],
        out_specs=pl.BlockSpec((1, m, cn), lambda b, j: (b, 0, j)),
        out_shape=jax.ShapeDtypeStruct((bsz, m, n), BF16),
        compiler_params=_cparams("parallel", "parallel"),
        name="const_lhs_matmul",
    )(a, x)


def _seq_dft_two_stage(h, bsz, seq):
    d = h.shape[1]
    na, nb = DFT_NA, seq // DFT_NA
    gw = LANES
    nb_chunk = min(nb, 8)
    m1, m2 = _dft_tables(seq)
    t1 = pl.pallas_call(
        functools.partial(_dft1_kernel, nb_chunk=nb_chunk, nb=nb, na=na),
        grid=(bsz, d // gw, nb // nb_chunk),
        in_specs=[
            pl.BlockSpec((1, seq, gw), lambda b, g, c: (b, 0, g)),
            pl.BlockSpec((nb_chunk, 2 * na, na), lambda b, g, c: (c, 0, 0)),
        ],
        out_specs=pl.BlockSpec((1, 2, nb_chunk, na, gw), lambda b, g, c: (b, 0, c, 0, g)),
        out_shape=jax.ShapeDtypeStruct((bsz, 2, nb, na, d), BF16),
        compiler_params=_cparams("parallel", "parallel", "arbitrary"),
        name="dft_stage1",
    )(h.reshape(bsz, seq, d), m1)
    pq = _const_lhs_matmul(m2, t1.reshape(bsz, 2 * nb, na * d), cn=min(na * d, 8192))
    return pq.reshape(bsz, 2, seq, d)


def _seq_dft_dense(h, bsz, seq):
    d = h.shape[1]
    k = jnp.arange(seq, dtype=jnp.int32)
    ang = ((k[:, None] * k[None, :]) % seq).astype(F32) * (2.0 * math.pi / seq)
    a = jnp.concatenate([jnp.cos(ang), jnp.sin(ang)], axis=0).astype(BF16)
    pq = _const_lhs_matmul(a, h.reshape(bsz, seq, d), cn=d)
    return pq.reshape(bsz, 2, seq, d)


def _fourier_out_kernel(p_ref, q_ref, x_ref, cm_ref, wo_ref, bo_ref, mod_ref, o_ref, *, bpb, n_lat):
    d = x_ref.shape[1]
    gw = d // FNET_GROUPS
    b = _batch_row(bpb, n_lat)
    p, q = p_ref[0, 0], q_ref[0, 0]
    f = [jnp.dot(p[:, g * gw:(g + 1) * gw], cm_ref[:gw], preferred_element_type=F32)
         + jnp.dot(q[:, g * gw:(g + 1) * gw], cm_ref[gw:], preferred_element_type=F32)
         for g in range(FNET_GROUPS)]
    f = jnp.concatenate(f, axis=1).astype(BF16)
    y = jnp.dot(f, wo_ref[...], preferred_element_type=F32) + bo_ref[...]
    o_ref[...] = x_ref[...] + _mod_row(mod_ref, b, 2, d) * y


def _fourier_out(pq, x, mod, w_out, b_out, *, seq, tm, bpb, n_lat):
    t, d = x.shape
    gw = d // FNET_GROUPS
    k = jnp.arange(gw, dtype=jnp.int32)
    ang = ((k[:, None] * k[None, :]) % gw).astype(F32) * (2.0 * math.pi / gw)
    scale = 1.0 / math.sqrt(seq * gw)
    cm = (jnp.concatenate([jnp.cos(ang), -jnp.sin(ang)], axis=0) * scale).astype(BF16)
    return pl.pallas_call(
        functools.partial(_fourier_out_kernel, bpb=bpb, n_lat=n_lat),
        grid=(t // tm,),
        in_specs=[
            pl.BlockSpec((1, 1, tm, d), lambda i: (i // bpb, 0, i % bpb, 0)),
            pl.BlockSpec((1, 1, tm, d), lambda i: (i // bpb, 1, i % bpb, 0)),
            pl.BlockSpec((tm, d), lambda i: (i, 0)),
            pl.BlockSpec((2 * gw, gw), lambda i: (0, 0)),
            pl.BlockSpec((d, d), lambda i: (0, 0)),
            pl.BlockSpec((1, d), lambda i: (0, 0)),
            pl.BlockSpec(mod.shape, lambda i: (0, 0)),
        ],
        out_specs=pl.BlockSpec((tm, d), lambda i: (i, 0)),
        out_shape=jax.ShapeDtypeStruct((t, d), F32),
        compiler_params=_cparams("parallel"),
        name="fourier_out",
    )(pq, pq, x, cm, w_out.astype(BF16), b_out.reshape(1, d), mod)


def _head_perm():
    return jnp.concatenate([jnp.arange(0, HEAD_DIM, 2), jnp.arange(1, HEAD_DIM, 2)])


def _rope_tables(seq):
    rows = seq // GRID_W
    row = jnp.repeat(jnp.arange(rows, dtype=F32), GRID_W)
    col = jnp.tile(jnp.arange(GRID_W, dtype=F32), rows)
    n_pairs_axis = HEAD_DIM // 4
    inv = ROPE_THETA ** (-jnp.arange(n_pairs_axis, dtype=F32) / n_pairs_axis)
    ang = jnp.concatenate([row[:, None] * inv, col[:, None] * inv], axis=-1)
    cos, sin = jnp.cos(ang), jnp.sin(ang)
    return jnp.concatenate([cos, cos], axis=-1), jnp.concatenate([-sin, sin], axis=-1)


def _qkv_kernel(x_ref, g_ref, mod_ref, w_ref, gq_ref, gk_ref, cos_ref, sin_ref, qt_ref, k_ref, v_ref, *, bpb, n_lat):
    d = x_ref.shape[1]
    b = _batch_row(bpb, n_lat)
    h = _norm_mod(x_ref[...], g_ref[...], mod_ref, b, 0, d)
    u = jnp.dot(h.astype(BF16), w_ref[...], preferred_element_type=F32)
    cos, sin = cos_ref[...], sin_ref[...]
    kv_dim = k_ref.shape[1]
    q_dim = Q_PER_KV * kv_dim
    tq = qt_ref.shape[4] // Q_PER_KV

    def norm_rope(xh, gain):
        xh = _rms(xh, gain)
        return xh * cos + pltpu.roll(xh, HEAD_DIM // 2, 1) * sin

    q_scale = HEAD_DIM ** -0.5 * math.log2(math.e)
    for hh in range(q_dim // HEAD_DIM):
        sl = slice(hh * HEAD_DIM, (hh + 1) * HEAD_DIM)
        kvh, g = divmod(hh, Q_PER_KV)
        q_t = (norm_rope(u[:, sl], gq_ref[...]) * q_scale).T.astype(BF16)
        for qb in range(qt_ref.shape[2]):
            qt_ref[0, kvh, qb, :, g * tq:(g + 1) * tq] = q_t[:, qb * tq:(qb + 1) * tq]
    for hh in range(kv_dim // HEAD_DIM):
        sl = slice(hh * HEAD_DIM, (hh + 1) * HEAD_DIM)
        k_ref[:, sl] = norm_rope(u[:, q_dim + hh * HEAD_DIM:q_dim + (hh + 1) * HEAD_DIM], gk_ref[...]).astype(BF16)
    v_ref[...] = u[:, q_dim + kv_dim:].astype(BF16)


def _qkv(x, g, mod, w_qkv, gq, gk, cos, sin, *, seq, tq, tm, bpb, n_lat):
    t, d = x.shape
    assert tm % tq == 0 and seq % tm == 0
    q_dim = N_KV_HEADS * Q_PER_KV * HEAD_DIM
    kv_dim = N_KV_HEADS * HEAD_DIM
    perm = _head_perm()
    n_heads_qk = (q_dim + kv_dim) // HEAD_DIM
    cols = (jnp.arange(n_heads_qk)[:, None] * HEAD_DIM + perm[None, :]).reshape(-1)
    cols = jnp.concatenate([cols, jnp.arange(q_dim + kv_dim, q_dim + 2 * kv_dim)])
    w = w_qkv[:, cols].astype(BF16)
    n_pos = cos.shape[0] // tm
    return pl.pallas_call(
        functools.partial(_qkv_kernel, bpb=bpb, n_lat=n_lat),
        grid=(t // tm,),
        in_specs=[
            pl.BlockSpec((tm, d), lambda i: (i, 0)),
            pl.BlockSpec((1, d), lambda i: (0, 0)),
            pl.BlockSpec(mod.shape, lambda i: (0, 0)),
            pl.BlockSpec((d, q_dim + 2 * kv_dim), lambda i: (0, 0)),
            pl.BlockSpec((1, HEAD_DIM), lambda i: (0, 0)),
            pl.BlockSpec((1, HEAD_DIM), lambda i: (0, 0)),
            pl.BlockSpec((tm, HEAD_DIM), lambda i: (i % n_pos, 0)),
            pl.BlockSpec((tm, HEAD_DIM), lambda i: (i % n_pos, 0)),
        ],
        out_specs=[
            pl.BlockSpec((1, N_KV_HEADS, tm // tq, HEAD_DIM, Q_PER_KV * tq),
                         lambda i: (i // (seq // tm), 0, i % (seq // tm), 0, 0)),
            pl.BlockSpec((tm, kv_dim), lambda i: (i, 0)),
            pl.BlockSpec((tm, kv_dim), lambda i: (i, 0)),
        ],
        out_shape=[
            jax.ShapeDtypeStruct((t // seq, N_KV_HEADS, seq // tq, HEAD_DIM, Q_PER_KV * tq), BF16),
            jax.ShapeDtypeStruct((t, kv_dim), BF16),
            jax.ShapeDtypeStruct((t, kv_dim), BF16),
        ],
        compiler_params=_cparams("parallel"),
        name="qkv_proj",
    )(x, g.reshape(1, d), mod, w, gq[perm].reshape(1, HEAD_DIM), gk[perm].reshape(1, HEAD_DIM), cos, sin)


ATTN_TQ = 256
ATTN_ROW_CHUNK = 16
ATTN_MAX_TK = 1408


def _attn_kernel(qt_ref, k_ref, vt_ref, o_ref, *scratch, tq, tk, nk):
    s_ref, p_ref, acc_ref = scratch
    nq = Q_PER_KV * tq
    acc_ref[...] = jnp.zeros(acc_ref.shape, F32)
    rc = ATTN_ROW_CHUNK

    def kv_chunk(j, carry):
        m_prev, l_prev = carry
        kk = k_ref[0, pl.ds(pl.multiple_of(j * tk, tk), tk), :]
        s_ref[...] = jnp.dot(kk, qt_ref[0, 0, 0], preferred_element_type=F32)
        mx = s_ref[0:SUBLANES, :]
        for r in range(1, tk // SUBLANES):
            mx = jnp.maximum(mx, s_ref[r * SUBLANES:(r + 1) * SUBLANES, :])
        m_new = jnp.maximum(m_prev, jnp.max(mx, axis=0, keepdims=True))
        psum = jnp.zeros((SUBLANES, nq), F32)
        for r in range(tk // rc):
            p = jnp.exp2(s_ref[r * rc:(r + 1) * rc, :] - m_new)
            p_ref[r * rc:(r + 1) * rc, :] = p.astype(BF16)
            for h in range(rc // SUBLANES):
                psum = psum + p[h * SUBLANES:(h + 1) * SUBLANES]
        alpha = jnp.exp2(m_prev - m_new)
        acc_ref[...] = alpha * acc_ref[...] + jnp.dot(vt_ref[0, 0, j], p_ref[...], preferred_element_type=F32)
        return m_new, alpha * l_prev + jnp.sum(psum, axis=0, keepdims=True)

    init = (jnp.full((1, nq), -jnp.inf, F32), jnp.zeros((1, nq), F32))
    _, l_all = lax.fori_loop(0, nk, kv_chunk, init)
    out = (acc_ref[...] / l_all).T
    for g in range(Q_PER_KV):
        o_ref[0, :, g * HEAD_DIM:(g + 1) * HEAD_DIM] = out[g * tq:(g + 1) * tq].astype(o_ref.dtype)


def _attention(qt, k_all, v_all, *, tk):
    bsz, _, n_qblk, _, nq = qt.shape
    tq = nq // Q_PER_KV
    seq = n_qblk * tq
    lk = k_all.shape[1]
    gq = Q_PER_KV * HEAD_DIM
    q_dim = N_KV_HEADS * gq
    nk = lk // tk
    assert tk % ATTN_ROW_CHUNK == 0 and nq % LANES == 0
    vt = v_all.reshape(bsz, nk, tk, N_KV_HEADS, HEAD_DIM).transpose(0, 3, 1, 4, 2)
    return pl.pallas_call(
        functools.partial(_attn_kernel, tq=tq, tk=tk, nk=nk),
        grid=(bsz, N_KV_HEADS, seq // tq),
        in_specs=[
            pl.BlockSpec((1, 1, 1, HEAD_DIM, nq), lambda b, h, i: (b, h, i, 0, 0)),
            pl.BlockSpec((1, lk, HEAD_DIM), lambda b, h, i: (b, 0, h)),
            pl.BlockSpec((1, 1, nk, HEAD_DIM, tk), lambda b, h, i: (b, h, 0, 0, 0)),
        ],
        out_specs=pl.BlockSpec((1, tq, gq), lambda b, h, i: (b, i, h)),
        out_shape=jax.ShapeDtypeStruct((bsz, seq, q_dim), BF16),
        scratch_shapes=[pltpu.VMEM((tk, nq), F32), pltpu.VMEM((tk, nq), BF16), pltpu.VMEM((HEAD_DIM, nq), F32)],
        compiler_params=_cparams("parallel", "parallel", "parallel"),
        name="flash_attention",
    )(qt, k_all, vt)


def _proj_residual_kernel(a_ref, x_ref, w_ref, mod_ref, o_ref, *, bpb, n_lat):
    d = x_ref.shape[1]
    b = _batch_row(bpb, n_lat)
    y = jnp.dot(a_ref[...], w_ref[...], preferred_element_type=F32)
    o_ref[...] = x_ref[...] + _mod_row(mod_ref, b, 2, d) * y


def _proj_residual(a, x, mod, w, *, tm, bpb, n_lat):
    t, d = x.shape
    ka = a.shape[1]
    return pl.pallas_call(
        functools.partial(_proj_residual_kernel, bpb=bpb, n_lat=n_lat),
        grid=(t // tm,),
        in_specs=[
            pl.BlockSpec((tm, ka), lambda i: (i, 0)),
            pl.BlockSpec((tm, d), lambda i: (i, 0)),
            pl.BlockSpec((ka, d), lambda i: (0, 0)),
            pl.BlockSpec(mod.shape, lambda i: (0, 0)),
        ],
        out_specs=pl.BlockSpec((tm, d), lambda i: (i, 0)),
        out_shape=jax.ShapeDtypeStruct((t, d), F32),
        compiler_params=_cparams("parallel"),
        name="proj_residual",
    )(a, x, w.astype(BF16), mod)


ROUTE_E, ROUTE_GATE, ROUTE_RANK = 0, 2, 4
ROUTER_LOGIT0 = N_EXPERT_GROUPS


U32 = jnp.uint32
BF16_HIGH_BITS = 0xFFFF0000


def _pack_bf16_pairs(x):
    half = x.shape[1] // 2
    hi = lax.bitcast_convert_type(x[:, :half].astype(BF16).astype(F32), U32)
    lo = lax.bitcast_convert_type(x[:, half:].astype(BF16).astype(F32), U32)
    return (hi & U32(BF16_HIGH_BITS)) | (lo >> 16)


def _unpack_bf16_pairs(w):
    return (lax.bitcast_convert_type(w & U32(BF16_HIGH_BITS), F32), lax.bitcast_convert_type(w << 16, F32))


def _router_kernel(x_ref, g_ref, mod_ref, wr_ref, br_ref, tri_ref, h_ref, r_ref, cnt_ref, carry, *, bpb, n_lat):
    d = x_ref.shape[1]
    tm = x_ref.shape[0]
    i = pl.program_id(0)
    b = _batch_row(bpb, n_lat)

    @pl.when(i == 0)
    def _():
        carry[...] = jnp.zeros(carry.shape, F32)

    h = _norm_mod(x_ref[...], g_ref[...], mod_ref, b, 3, d)
    h_ref[...] = _pack_bf16_pairs(h)
    h_hi = h.astype(BF16)
    h_lo = (h - h_hi.astype(F32)).astype(BF16)
    logits = (jnp.dot(h_hi, wr_ref[0], preferred_element_type=F32) + jnp.dot(h_lo, wr_ref[0], preferred_element_type=F32)
              + jnp.dot(h_hi, wr_ref[1], preferred_element_type=F32) + br_ref[...])
    lane = lax.broadcasted_iota(jnp.int32, (tm, LANES), 1).astype(F32)
    big = float(LANES)
    neg = -jnp.inf
    gmask = lane < N_EXPERT_GROUPS
    gl = jnp.where(gmask, logits, neg)
    ge = jnp.exp(gl - jnp.max(gl, axis=-1, keepdims=True))
    gp = ge / jnp.sum(ge, axis=-1, keepdims=True)
    g_top = jnp.max(gp, axis=-1, keepdims=True)
    g_idx = jnp.min(jnp.where(gmask & (gp == g_top), lane, big), axis=-1, keepdims=True)
    lo = ROUTER_LOGIT0 + g_idx * EXPERTS_PER_GROUP
    emask = (lane >= lo) & (lane < lo + EXPERTS_PER_GROUP)
    el = jnp.where(emask, logits, neg)
    ee = jnp.exp(el - jnp.max(el, axis=-1, keepdims=True))
    ep = ee / jnp.sum(ee, axis=-1, keepdims=True)
    p1 = jnp.max(ep, axis=-1, keepdims=True)
    i1 = jnp.min(jnp.where(emask & (ep == p1), lane, big), axis=-1, keepdims=True)
    rest = emask & (lane != i1)
    p2 = jnp.max(jnp.where(rest, ep, -1.0), axis=-1, keepdims=True)
    i2 = jnp.min(jnp.where(rest & (ep == p2), lane, big), axis=-1, keepdims=True)
    denom = p1 + p2
    gate1 = g_top * p1 / denom
    gate2 = g_top * p2 / denom
    hit1 = lane == i1
    hit2 = lane == i2
    onehot = jnp.where(hit1 | hit2, 1.0, 0.0)
    cum = jnp.dot(tri_ref[...], onehot.astype(BF16), preferred_element_type=F32) + carry[...]
    rank1 = jnp.sum(jnp.where(hit1, cum, 0.0), axis=-1, keepdims=True)
    rank2 = jnp.sum(jnp.where(hit2, cum, 0.0), axis=-1, keepdims=True)
    carry[...] = carry[...] + jnp.sum(onehot, axis=0, keepdims=True)
    cnt_ref[...] = carry[...]
    rec = jnp.zeros((tm, LANES), F32)
    for ln, val in ((ROUTE_E, i1 - ROUTER_LOGIT0), (ROUTE_E + 1, i2 - ROUTER_LOGIT0), (ROUTE_GATE, gate1),
                    (ROUTE_GATE + 1, gate2), (ROUTE_RANK, rank1), (ROUTE_RANK + 1, rank2)):
        rec = jnp.where(lane == float(ln), val, rec)
    r_ref[...] = rec


def _router(x, g, mod, w_group, b_group, w_expert, b_expert, *, tm, bpb, n_lat):
    t, d = x.shape
    n_log = N_EXPERT_GROUPS + N_EXPERTS
    wr = jnp.zeros((d, LANES), F32).at[:, :N_EXPERT_GROUPS].set(w_group).at[:, N_EXPERT_GROUPS:n_log].set(w_expert)
    br = jnp.zeros((1, LANES), F32).at[0, :N_EXPERT_GROUPS].set(b_group).at[0, N_EXPERT_GROUPS:n_log].set(b_expert)
    wr_hi = wr.astype(BF16)
    wr = jnp.stack([wr_hi, (wr - wr_hi.astype(F32)).astype(BF16)])
    tri = (jnp.arange(tm)[:, None] > jnp.arange(tm)[None, :]).astype(BF16)
    return pl.pallas_call(
        functools.partial(_router_kernel, bpb=bpb, n_lat=n_lat),
        grid=(t // tm,),
        in_specs=[
            pl.BlockSpec((tm, d), lambda i: (i, 0)),
            pl.BlockSpec((1, d), lambda i: (0, 0)),
            pl.BlockSpec(mod.shape, lambda i: (0, 0)),
            pl.BlockSpec((2, d, LANES), lambda i: (0, 0, 0)),
            pl.BlockSpec((1, LANES), lambda i: (0, 0)),
            pl.BlockSpec((tm, tm), lambda i: (0, 0)),
        ],
        out_specs=[
            pl.BlockSpec((tm, d // 2), lambda i: (i, 0)),
            pl.BlockSpec((tm, LANES), lambda i: (i, 0)),
            pl.BlockSpec((1, LANES), lambda i: (0, 0)),
        ],
        out_shape=[
            jax.ShapeDtypeStruct((t, d // 2), U32),
            jax.ShapeDtypeStruct((t, LANES), F32),
            jax.ShapeDtypeStruct((1, LANES), F32),
        ],
        scratch_shapes=[pltpu.VMEM((1, LANES), F32)],
        compiler_params=_cparams("arbitrary"),
        name="moe_router",
    )(x, g.reshape(1, d), mod, wr, br, tri)


def _moe_ffn_kernel(be_ref, bv_ref, x_ref, wg_ref, wu_ref, wd_ref, o_ref, wgb, wub, wdb):
    i = pl.program_id(0)
    e = be_ref[i]
    e_prev = be_ref[jnp.maximum(i - 1, 0)]

    @pl.when((i == 0) | (e != e_prev))
    def _():
        wgb[...] = wg_ref[0, 0].astype(BF16)
        wub[...] = wu_ref[0, 0].astype(BF16)
        wdb[...] = wd_ref[0, 0].astype(BF16)

    n_valid = bv_ref[i]

    @pl.when(n_valid > 0)
    def _():
        half = x_ref.shape[1]
        x_a, x_b = (v.astype(BF16) for v in _unpack_bf16_pairs(x_ref[...]))

        def proj(w):
            return (jnp.dot(x_a, w[:half], preferred_element_type=F32)
                    + jnp.dot(x_b, w[half:], preferred_element_type=F32))

        gt, up = proj(wgb), proj(wub)
        a = gt * _sigmoid(gt) * up
        o_ref[...] = _pack_bf16_pairs(jnp.dot(a.astype(BF16), wdb[...], preferred_element_type=F32))

    @pl.when(n_valid <= 0)
    def _():
        o_ref[...] = jnp.zeros(o_ref.shape, U32)


def _moe_ffn(buf, blk_expert, blk_valid, w_gate, w_up, w_down, layer):
    p = buf.shape[0]
    d = w_gate.shape[-2]
    de = w_gate.shape[-1]
    n_blocks = p // MOE_BLOCK
    grid_spec = pltpu.PrefetchScalarGridSpec(
        num_scalar_prefetch=2,
        grid=(n_blocks,),
        in_specs=[
            pl.BlockSpec((MOE_BLOCK, d // 2), lambda i, be, bv: (i, 0)),
            pl.BlockSpec((1, 1, d, de), lambda i, be, bv: (layer, be[i], 0, 0)),
            pl.BlockSpec((1, 1, d, de), lambda i, be, bv: (layer, be[i], 0, 0)),
            pl.BlockSpec((1, 1, de, d), lambda i, be, bv: (layer, be[i], 0, 0)),
        ],
        out_specs=pl.BlockSpec((MOE_BLOCK, d // 2), lambda i, be, bv: (i, 0)),
        scratch_shapes=[pltpu.VMEM((d, de), BF16), pltpu.VMEM((d, de), BF16), pltpu.VMEM((de, d), BF16)],
    )
    return pl.pallas_call(
        _moe_ffn_kernel,
        grid_spec=grid_spec,
        out_shape=jax.ShapeDtypeStruct((p, d // 2), U32),
        compiler_params=_cparams("arbitrary"),
        name="moe_ffn",
    )(blk_expert, blk_valid, buf, w_gate, w_up, w_down)


DMA_ISSUE_UNROLL = 8


def _row_copy(src, src_row, dst, dst_row, sem):
    return pltpu.make_async_copy(src.at[pl.ds(src_row, 1)], dst.at[pl.ds(dst_row, 1)], sem)


def _moe_dispatch_kernel(pe_ref, dest_ref, h_ref, buf_ref, zero_ref, sems, zsem):
    tm = h_ref.shape[0] * SUBLANES
    n_blocks = buf_ref.shape[0] // MOE_BLOCK

    def zero_block(first_row):
        return pltpu.make_async_copy(zero_ref, buf_ref.at[pl.ds(pl.multiple_of(first_row, MOE_BLOCK), MOE_BLOCK)], zsem)

    @pl.when(pl.program_id(0) == 0)
    def _():
        zero_ref[...] = jnp.zeros(zero_ref.shape, zero_ref.dtype)

        def expert_has_rows(e):
            return pe_ref[e] > jnp.where(e > 0, pe_ref[jnp.maximum(e - 1, 0)], 0)

        def fill(start):
            def body(e, carry):
                @pl.when(expert_has_rows(e))
                def _():
                    cp = zero_block(pe_ref[e] - MOE_BLOCK)
                    cp.start() if start else cp.wait()
                return carry
            return body

        def tail(start):
            def body(b, carry):
                cp = zero_block(b * MOE_BLOCK)
                cp.start() if start else cp.wait()
                return carry
            return body

        first_unused = pe_ref[N_EXPERTS - 1] // MOE_BLOCK
        for start in (True, False):
            lax.fori_loop(0, N_EXPERTS, fill(start), 0)
            lax.fori_loop(first_unused, n_blocks, tail(start), 0)

    def issue(tile, carry):
        for k in range(SUBLANES):
            for j in range(2):
                slot = dest_ref[0, 0, 2 * SUBLANES * tile + 2 * k + j]
                pltpu.make_async_copy(h_ref.at[tile, pl.ds(k, 1)], buf_ref.at[pl.ds(slot, 1)], sems.at[j]).start()
        return carry

    lax.fori_loop(0, tm // SUBLANES, issue, 0)
    for j in range(2):
        rows = buf_ref.at[pl.ds(0, tm)]
        pltpu.make_async_copy(rows, rows, sems.at[j]).wait()


def _moe_dispatch(h, dest, pad_end, n_slots, *, tm):
    t, d = h.shape
    grid_spec = pltpu.PrefetchScalarGridSpec(
        num_scalar_prefetch=1,
        grid=(t // tm,),
        in_specs=[
            pl.BlockSpec((1, 1, 2 * tm), lambda i, pe: (i, 0, 0), memory_space=pltpu.SMEM),
            pl.BlockSpec((tm // SUBLANES, SUBLANES, d), lambda i, pe: (i, 0, 0)),
        ],
        out_specs=pl.BlockSpec(memory_space=pl.ANY),
        scratch_shapes=[pltpu.VMEM((MOE_BLOCK, d), h.dtype), pltpu.SemaphoreType.DMA((2,)), pltpu.SemaphoreType.DMA(())],
    )
    return pl.pallas_call(
        _moe_dispatch_kernel,
        grid_spec=grid_spec,
        out_shape=jax.ShapeDtypeStruct((n_slots, d), h.dtype),
        compiler_params=_cparams("arbitrary"),
        name="moe_dispatch",
    )(pad_end.astype(jnp.int32), dest.reshape(t // tm, 1, 2 * tm), h.reshape(t // SUBLANES, SUBLANES, d))


def _moe_combine_kernel(dest_ref, x_ref, r_ref, mod_ref, yb_ref, o_ref, y0, y1, sems, *, bpb, n_lat):
    tm, d = x_ref.shape
    b = _batch_row(bpb, n_lat)
    ys = (y0, y1)

    def issue(tile, carry):
        for k in range(SUBLANES):
            for j in range(2):
                slot = dest_ref[0, 0, 2 * SUBLANES * tile + 2 * k + j]
                pltpu.make_async_copy(yb_ref.at[pl.ds(slot, 1)], ys[j].at[tile, pl.ds(k, 1)], sems.at[j]).start()
        return carry

    lax.fori_loop(0, tm // SUBLANES, issue, 0)
    for j in range(2):
        rows = yb_ref.at[pl.ds(0, tm)]
        pltpu.make_async_copy(rows, rows, sems.at[j]).wait()
    r = r_ref[...]
    half = d // 2
    gated = [tuple(v * r[:, ROUTE_GATE + j:ROUTE_GATE + j + 1] for v in _unpack_bf16_pairs(ys[j][...].reshape(tm, half)))
             for j in range(2)]
    gate = _mod_row(mod_ref, b, 5, d)
    for c in range(2):
        cols = slice(c * half, (c + 1) * half)
        o_ref[:, cols] = x_ref[:, cols] + gate[:, cols] * (gated[0][c] + gated[1][c])


def _moe_combine(x, yb, dest, route, mod, *, tm, bpb, n_lat):
    t, d = x.shape
    row = pl.BlockSpec((tm, d), lambda i: (i, 0))
    return pl.pallas_call(
        functools.partial(_moe_combine_kernel, bpb=bpb, n_lat=n_lat),
        grid=(t // tm,),
        in_specs=[
            pl.BlockSpec((1, 1, 2 * tm), lambda i: (i, 0, 0), memory_space=pltpu.SMEM),
            row,
            pl.BlockSpec((tm, LANES), lambda i: (i, 0)),
            pl.BlockSpec(mod.shape, lambda i: (0, 0)),
            pl.BlockSpec(memory_space=pl.ANY),
        ],
        out_specs=row,
        out_shape=jax.ShapeDtypeStruct((t, d), F32),
        scratch_shapes=[pltpu.VMEM((tm // SUBLANES, SUBLANES, d // 2), U32) for _ in range(2)] + [pltpu.SemaphoreType.DMA((2,))],
        compiler_params=_cparams("arbitrary"),
        name="moe_combine",
    )(dest.reshape(t // tm, 1, 2 * tm), x, route, mod, yb)


def _hier_moe_residual(x, g, mod, w_group, b_group, w_expert, b_expert, w_gate, w_up, w_down, layer, *, tm, bpb, n_lat):
    t, d = x.shape
    h, route, counts = _router(x, g, mod, w_group, b_group, w_expert, b_expert, tm=tm, bpb=bpb, n_lat=n_lat)
    counts = counts[0, ROUTER_LOGIT0:ROUTER_LOGIT0 + N_EXPERTS].astype(jnp.int32)
    expert = route[:, ROUTE_E:ROUTE_E + 2].astype(jnp.int32)
    rank = route[:, ROUTE_RANK:ROUTE_RANK + 2].astype(jnp.int32)
    padded = (counts + MOE_BLOCK - 1) // MOE_BLOCK * MOE_BLOCK
    pad_end = jnp.cumsum(padded)
    pad_start = pad_end - padded
    dest = jnp.sum(jnp.where(expert[:, :, None] == jnp.arange(N_EXPERTS), pad_start, 0), axis=-1) + rank
    n_blocks = -(-(2 * t) // MOE_BLOCK) + N_EXPERTS
    blk_first = jnp.arange(n_blocks, dtype=jnp.int32) * MOE_BLOCK
    blk_expert = jnp.minimum(jnp.sum(blk_first[:, None] >= pad_end[None, :], axis=1), N_EXPERTS - 1).astype(jnp.int32)
    blk_rows = jnp.clip((pad_start + counts)[blk_expert] - blk_first, 0, MOE_BLOCK).astype(jnp.int32)
    buf = _moe_dispatch(h, dest, pad_end, n_blocks * MOE_BLOCK, tm=tm)
    yb = _moe_ffn(buf, blk_expert, blk_rows, w_gate, w_up, w_down, layer)
    return _moe_combine(x, yb, dest, route, mod, tm=tm, bpb=bpb, n_lat=n_lat)


def kernel(x, c, ctx, c_ctx, norm_g, w_mod, b_mod, conv_w_in, conv_b_in, conv_w_dw, conv_b_dw, conv_norm_g, conv_w_out, conv_b_out, fnet_w_out, fnet_b_out, attn_w_qkv, attn_q_norm_g, attn_k_norm_g, attn_w_out, moe_w_group, moe_b_group, moe_w_expert, moe_b_expert, moe_w_gate, moe_w_up, moe_w_down):
    bsz, seq, d = x.shape
    n_ctx = ctx.shape[1]
    depth = norm_g.shape[0]
    assert bsz <= MOD_ROWS // 2 and seq % DFT_NA == 0 and seq % GRID_W == 0
    t_lat, t_ctx = bsz * seq, bsz * n_ctx
    tm = min(512, seq)
    tmc = min(tm, n_ctx)
    assert seq % tm == 0 and n_ctx % tmc == 0 and tm % tmc == 0
    lat = dict(tm=tm, bpb=seq // tm, n_lat=t_lat // tm)
    cx = dict(tm=tmc, bpb=1, n_lat=0)
    tmb = tm if t_ctx % tm == 0 else tmc
    both = dict(tm=tmb, bpb=seq // tmb, n_lat=t_lat // tmb)
    last_reader = max([i for i in range(depth) if i % N_MIXERS == 2], default=-1)

    cvec = jnp.zeros((MOD_ROWS, d), F32).at[:bsz].set(c).at[MOD_ROWS // 2].set(c_ctx)
    mod_all = _modulation(cvec, w_mod, b_mod)
    cos, sin = _rope_tables(seq)

    xl = x.reshape(t_lat, d)
    xc = ctx.reshape(t_ctx, d)
    for i in range(depth):
        m, j = i % N_MIXERS, i // N_MIXERS
        ctx_on = i <= last_reader
        ctx_full = i < last_reader
        mod = mod_all[i]
        g1, g2 = norm_g[i, 0], norm_g[i, 1]
        if m == 0:
            cp = (conv_w_dw[j], conv_b_dw[j], conv_norm_g[j], conv_w_out[j], conv_b_out[j])
            v = _conv_in(xl, g1, mod, conv_w_in[j], conv_b_in[j], **lat)
            xl = _conv_out(v, xl, mod, *cp, **lat)
            if ctx_full:
                vc = _conv_in(xc, g1, mod, conv_w_in[j], conv_b_in[j], **cx)
                xc = _conv_out(vc, xc, mod, *cp, tm=n_ctx, bpb=1, n_lat=0)
        elif m == 1:
            hl = _prenorm(xl, g1, mod, dtype=F32, **lat)
            pq = _seq_dft_two_stage(hl, bsz, seq)
            xl = _fourier_out(pq, xl, mod, fnet_w_out[j], fnet_b_out[j], seq=seq, **lat)
            if ctx_full:
                hc = _prenorm(xc, g1, mod, dtype=BF16, **cx)
                pqc = _seq_dft_dense(hc, bsz, n_ctx)
                xc = _fourier_out(pqc, xc, mod, fnet_w_out[j], fnet_b_out[j], seq=n_ctx, tm=n_ctx, bpb=1, n_lat=0)
        else:
            gq, gk = attn_q_norm_g[j], attn_k_norm_g[j]
            qt, k, vv = _qkv(xl, g1, mod, attn_w_qkv[j], gq, gk, cos, sin, seq=seq, tq=min(ATTN_TQ, seq), **lat)
            ones = jnp.ones((tmc, HEAD_DIM), F32)
            _, kc, vc = _qkv(xc, g1, mod, attn_w_qkv[j], gq, gk, ones, jnp.zeros_like(ones),
                             seq=n_ctx, tq=min(ATTN_TQ, tmc), **cx)
            kv_dim = k.shape[1]
            k_all = jnp.concatenate([kc.reshape(bsz, n_ctx, kv_dim), k.reshape(bsz, seq, kv_dim)], axis=1)
            v_all = jnp.concatenate([vc.reshape(bsz, n_ctx, kv_dim), vv.reshape(bsz, seq, kv_dim)], axis=1)
            lk = seq + n_ctx
            tk = max(t for t in range(LANES, ATTN_MAX_TK + 1, LANES) if lk % t == 0)
            o = _attention(qt, k_all, v_all, tk=tk)
            xl = _proj_residual(o.reshape(t_lat, -1), xl, mod, attn_w_out[j], **lat)
            if ctx_full:
                raise NotImplementedError("context-stream attention output is only needed before the last reader layer")
        mp = (moe_w_group[i], moe_b_group[i], moe_w_expert[i], moe_b_expert[i], moe_w_gate, moe_w_up, moe_w_down, i)
        if ctx_full:
            xa = _hier_moe_residual(jnp.concatenate([xl, xc], axis=0), g2, mod, *mp, **both)
            xl, xc = xa[:t_lat], xa[t_lat:]
        else:
            xl = _hier_moe_residual(xl, g2, mod, *mp, **lat)
    return xl.reshape(bsz, seq, d)
```

```python
import functools
import math

import jax
import jax.numpy as jnp
from jax import lax
from jax.experimental import pallas as pl
from jax.experimental.pallas import tpu as pltpu

F32 = jnp.float32
BF16 = jnp.bfloat16
HIGHEST = lax.Precision.HIGHEST

EPS = 1e-6
GRID_W = 64
N_MIXERS = 3
CONV_WIDTH = 31
CONV_HALO = 16
CONV_ROW_CHUNK = 32
CONV_COL_CHUNK = 256
SUBLANES = 8
FNET_GROUPS = 4
HEAD_DIM = 128
N_KV_HEADS = 2
Q_PER_KV = 4
ROPE_THETA = 10000.0
N_EXPERT_GROUPS = 4
EXPERTS_PER_GROUP = 8
N_EXPERTS = N_EXPERT_GROUPS * EXPERTS_PER_GROUP
MOE_BLOCK = 256
LANES = 128
DFT_NA = 128
MOD_ROWS = 8
VMEM_LIMIT = 56 * 1024 * 1024


def _cparams(*sem):
    return pltpu.CompilerParams(dimension_semantics=sem, vmem_limit_bytes=VMEM_LIMIT)


def _rms(x, g):
    return x * lax.rsqrt(jnp.mean(x * x, axis=-1, keepdims=True) + EPS) * g


def _sigmoid(x):
    return 1.0 / (1.0 + jnp.exp(-x))


def _mod_row(mod_ref, b, k, d):
    return mod_ref[pl.ds(b, 1), pl.ds(k * d, d)]


def _norm_mod(x, g, mod_ref, b, k_shift, d):
    return _rms(x, g) * (1.0 + _mod_row(mod_ref, b, k_shift + 1, d)) + _mod_row(mod_ref, b, k_shift, d)


def _batch_row(bpb, n_lat_blocks):
    i = pl.program_id(0)
    return jnp.where(i < n_lat_blocks, i // bpb, MOD_ROWS // 2)


def _mod_kernel(c_ref, w_ref, b_ref, o_ref):
    c = c_ref[...]
    s = c * _sigmoid(c)
    o_ref[0] = jnp.dot(s, w_ref[0], precision=HIGHEST, preferred_element_type=F32) + b_ref[0]


def _modulation(cvec, w_mod, b_mod):
    depth, d, n = w_mod.shape
    tn = 1024
    return pl.pallas_call(
        _mod_kernel,
        grid=(depth, n // tn),
        in_specs=[
            pl.BlockSpec((MOD_ROWS, d), lambda l, j: (0, 0)),
            pl.BlockSpec((1, d, tn), lambda l, j: (l, 0, j)),
            pl.BlockSpec((1, 1, tn), lambda l, j: (l, 0, j)),
        ],
        out_specs=pl.BlockSpec((1, MOD_ROWS, tn), lambda l, j: (l, 0, j)),
        out_shape=jax.ShapeDtypeStruct((depth, MOD_ROWS, n), F32),
        compiler_params=_cparams("parallel", "parallel"),
        name="adaln_mod",
    )(cvec, w_mod, b_mod.reshape(depth, 1, n))


def _conv_in_kernel(x_ref, g_ref, mod_ref, w_ref, b_ref, v_ref, *, bpb, n_lat):
    d = x_ref.shape[1]
    b = _batch_row(bpb, n_lat)
    h = _norm_mod(x_ref[...], g_ref[...], mod_ref, b, 0, d)
    u = jnp.dot(h.astype(BF16), w_ref[...], preferred_element_type=F32) + b_ref[...]
    v_ref[...] = u[:, :d] * _sigmoid(u[:, d:])


def _conv_in(x, g, mod, w_in, b_in, *, tm, bpb, n_lat):
    t, d = x.shape
    return pl.pallas_call(
        functools.partial(_conv_in_kernel, bpb=bpb, n_lat=n_lat),
        grid=(t // tm,),
        in_specs=[
            pl.BlockSpec((tm, d), lambda i: (i, 0)),
            pl.BlockSpec((1, d), lambda i: (0, 0)),
            pl.BlockSpec(mod.shape, lambda i: (0, 0)),
            pl.BlockSpec((d, 2 * d), lambda i: (0, 0)),
            pl.BlockSpec((1, 2 * d), lambda i: (0, 0)),
        ],
        out_specs=pl.BlockSpec((tm, d), lambda i: (i, 0)),
        out_shape=jax.ShapeDtypeStruct((t, d), F32),
        compiler_params=_cparams("parallel"),
        name="conv_in",
    )(x, g.reshape(1, d), mod, w_in.astype(BF16), b_in.reshape(1, 2 * d))


def _conv_out_kernel(vp_ref, vc_ref, vn_ref, x_ref, wdw_ref, bdw_ref, gn_ref, wo_ref, bo_ref, mod_ref,
                     o_ref, buf, z, sh, *, bpb, n_lat, tm):
    d = x_ref.shape[1]
    i = pl.program_id(0)
    j = i % bpb
    b = _batch_row(bpb, n_lat)
    h = CONV_HALO
    buf[0:h, :] = jnp.where(j == 0, 0.0, vp_ref[...])
    buf[h:h + tm, :] = vc_ref[...]
    buf[h + tm:, :] = jnp.where(j == bpb - 1, 0.0, vn_ref[...])
    rc, cw = CONV_ROW_CHUNK, CONV_COL_CHUNK
    off = h - CONV_WIDTH // 2
    n_sh = sh.shape[1]

    def col_chunk(ci, carry):
        c0 = pl.multiple_of(ci * cw, cw)
        for s in range(SUBLANES):
            sh[s] = buf[s:s + n_sh, pl.ds(c0, cw)]
        for r in range(tm // rc):
            acc = jnp.zeros((rc, cw), F32)
            for t in range(CONV_WIDTH):
                q, s = divmod(off + t, SUBLANES)
                row0 = r * rc + q * SUBLANES
                acc = acc + sh[s, row0:row0 + rc, :] * wdw_ref[t:t + 1, pl.ds(c0, cw)]
            z[r * rc:(r + 1) * rc, pl.ds(c0, cw)] = acc
        return carry

    lax.fori_loop(0, d // cw, col_chunk, 0)
    zz = _rms(z[...] + bdw_ref[...], gn_ref[...])
    zz = zz * _sigmoid(zz)
    y = jnp.dot(zz.astype(BF16), wo_ref[...], preferred_element_type=F32) + bo_ref[...]
    o_ref[...] = x_ref[...] + _mod_row(mod_ref, b, 2, d) * y


def _conv_out(v, x, mod, w_dw, b_dw, g_norm, w_out, b_out, *, tm, bpb, n_lat):
    t, d = x.shape
    hb = tm // CONV_HALO
    n_halo = t // CONV_HALO
    wdw = jnp.zeros((32, d), F32).at[:CONV_WIDTH].set(w_dw)
    n_tap_tiles = (CONV_HALO - CONV_WIDTH // 2 + CONV_WIDTH - 1) // SUBLANES
    return pl.pallas_call(
        functools.partial(_conv_out_kernel, bpb=bpb, n_lat=n_lat, tm=tm),
        grid=(t // tm,),
        in_specs=[
            pl.BlockSpec((CONV_HALO, d), lambda i: (jnp.maximum(i * hb - 1, 0), 0)),
            pl.BlockSpec((tm, d), lambda i: (i, 0)),
            pl.BlockSpec((CONV_HALO, d), lambda i: (jnp.minimum((i + 1) * hb, n_halo - 1), 0)),
            pl.BlockSpec((tm, d), lambda i: (i, 0)),
            pl.BlockSpec((32, d), lambda i: (0, 0)),
            pl.BlockSpec((1, d), lambda i: (0, 0)),
            pl.BlockSpec((1, d), lambda i: (0, 0)),
            pl.BlockSpec((d, d), lambda i: (0, 0)),
            pl.BlockSpec((1, d), lambda i: (0, 0)),
            pl.BlockSpec(mod.shape, lambda i: (0, 0)),
        ],
        out_specs=pl.BlockSpec((tm, d), lambda i: (i, 0)),
        out_shape=jax.ShapeDtypeStruct((t, d), F32),
        scratch_shapes=[
            pltpu.VMEM((tm + 2 * CONV_HALO, d), F32),
            pltpu.VMEM((tm, d), F32),
            pltpu.VMEM((SUBLANES, tm + n_tap_tiles * SUBLANES, CONV_COL_CHUNK), F32),
        ],
        compiler_params=_cparams("parallel"),
        name="conv_out",
    )(v, v, v, x, wdw, b_dw.reshape(1, d), g_norm.reshape(1, d), w_out.astype(BF16), b_out.reshape(1, d), mod)


def _prenorm_kernel(x_ref, g_ref, mod_ref, h_ref, *, bpb, n_lat):
    d = x_ref.shape[1]
    b = _batch_row(bpb, n_lat)
    h_ref[...] = _norm_mod(x_ref[...], g_ref[...], mod_ref, b, 0, d).astype(h_ref.dtype)


def _prenorm(x, g, mod, *, tm, bpb, n_lat, dtype):
    t, d = x.shape
    return pl.pallas_call(
        functools.partial(_prenorm_kernel, bpb=bpb, n_lat=n_lat),
        grid=(t // tm,),
        in_specs=[
            pl.BlockSpec((tm, d), lambda i: (i, 0)),
            pl.BlockSpec((1, d), lambda i: (0, 0)),
            pl.BlockSpec(mod.shape, lambda i: (0, 0)),
        ],
        out_specs=pl.BlockSpec((tm, d), lambda i: (i, 0)),
        out_shape=jax.ShapeDtypeStruct((t, d), dtype),
        compiler_params=_cparams("parallel"),
        name="prenorm",
    )(x, g.reshape(1, d), mod)


def _dft1_kernel(h_ref, m_ref, o_ref, *, nb_chunk, nb, na):
    bc = pl.program_id(2)
    for bi in range(nb_chunk):
        b = bc * nb_chunk + bi
        xs = h_ref[0, pl.ds(b, na, stride=nb), :]
        t = jnp.dot(m_ref[bi], xs.astype(BF16), preferred_element_type=F32)
        o_ref[0, 0, bi] = t[:na].astype(BF16)
        o_ref[0, 1, bi] = t[na:].astype(BF16)


def _dft_tables(seq):
    na, nb = DFT_NA, seq // DFT_NA
    ka = jnp.arange(na, dtype=jnp.int32)
    a = jnp.arange(na, dtype=jnp.int32)
    b = jnp.arange(nb, dtype=jnp.int32)
    n = a[None, None, :] * nb + b[:, None, None]
    ang = ((ka[None, :, None] * n) % seq).astype(F32) * (2.0 * math.pi / seq)
    m1 = jnp.concatenate([jnp.cos(ang), -jnp.sin(ang)], axis=1).astype(BF16)
    kb = jnp.arange(nb, dtype=jnp.int32)
    ang2 = ((kb[:, None] * b[None, :]) % nb).astype(F32) * (2.0 * math.pi / nb)
    c2, s2 = jnp.cos(ang2), jnp.sin(ang2)
    m2 = jnp.concatenate([jnp.concatenate([c2, s2], axis=1),
                          jnp.concatenate([s2, -c2], axis=1)], axis=0).astype(BF16)
    return m1, m2


def _const_lhs_matmul_kernel(a_ref, x_ref, o_ref):
    o_ref[0] = jnp.dot(a_ref[...], x_ref[0], preferred_element_type=F32).astype(o_ref.dtype)


def _const_lhs_matmul(a, x, *, cn):
    m, k = a.shape
    bsz, _, n = x.shape
    return pl.pallas_call(
        _const_lhs_matmul_kernel,
        grid=(bsz, n // cn),
        in_specs=[
            pl.BlockSpec((m, k), lambda b, j: (0, 0)),
            pl.BlockSpec((1, k, cn), lambda b, j: (b, 0, j)),
        ],
        out_specs=pl.BlockSpec((1, m, cn), lambda b, j: (b, 0, j)),
        out_shape=jax.ShapeDtypeStruct((bsz, m, n), BF16),
        compiler_params=_cparams("parallel", "parallel"),
        name="const_lhs_matmul",
    )(a, x)


def _seq_dft_two_stage(h, bsz, seq):
    d = h.shape[1]
    na, nb = DFT_NA, seq // DFT_NA
    gw = LANES
    nb_chunk = min(nb, 8)
    m1, m2 = _dft_tables(seq)
    t1 = pl.pallas_call(
        functools.partial(_dft1_kernel, nb_chunk=nb_chunk, nb=nb, na=na),
        grid=(bsz, d // gw, nb // nb_chunk),
        in_specs=[
            pl.BlockSpec((1, seq, gw), lambda b, g, c: (b, 0, g)),
            pl.BlockSpec((nb_chunk, 2 * na, na), lambda b, g, c: (c, 0, 0)),
        ],
        out_specs=pl.BlockSpec((1, 2, nb_chunk, na, gw), lambda b, g, c: (b, 0, c, 0, g)),
        out_shape=jax.ShapeDtypeStruct((bsz, 2, nb, na, d), BF16),
        compiler_params=_cparams("parallel", "parallel", "arbitrary"),
        name="dft_stage1",
    )(h.reshape(bsz, seq, d), m1)
    pq = _const_lhs_matmul(m2, t1.reshape(bsz, 2 * nb, na * d), cn=min(na * d, 8192))
    return pq.reshape(bsz, 2, seq, d)


def _seq_dft_dense(h, bsz, seq):
    d = h.shape[1]
    k = jnp.arange(seq, dtype=jnp.int32)
    ang = ((k[:, None] * k[None, :]) % seq).astype(F32) * (2.0 * math.pi / seq)
    a = jnp.concatenate([jnp.cos(ang), jnp.sin(ang)], axis=0).astype(BF16)
    pq = _const_lhs_matmul(a, h.reshape(bsz, seq, d), cn=d)
    return pq.reshape(bsz, 2, seq, d)


def _fourier_out_kernel(p_ref, q_ref, x_ref, cm_ref, wo_ref, bo_ref, mod_ref, o_ref, *, bpb, n_lat):
    d = x_ref.shape[1]
    gw = d // FNET_GROUPS
    b = _batch_row(bpb, n_lat)
    p, q = p_ref[0, 0], q_ref[0, 0]
    f = [jnp.dot(p[:, g * gw:(g + 1) * gw], cm_ref[:gw], preferred_element_type=F32)
         + jnp.dot(q[:, g * gw:(g + 1) * gw], cm_ref[gw:], preferred_element_type=F32)
         for g in range(FNET_GROUPS)]
    f = jnp.concatenate(f, axis=1).astype(BF16)
    y = jnp.dot(f, wo_ref[...], preferred_element_type=F32) + bo_ref[...]
    o_ref[...] = x_ref[...] + _mod_row(mod_ref, b, 2, d) * y


def _fourier_out(pq, x, mod, w_out, b_out, *, seq, tm, bpb, n_lat):
    t, d = x.shape
    gw = d // FNET_GROUPS
    k = jnp.arange(gw, dtype=jnp.int32)
    ang = ((k[:, None] * k[None, :]) % gw).astype(F32) * (2.0 * math.pi / gw)
    scale = 1.0 / math.sqrt(seq * gw)
    cm = (jnp.concatenate([jnp.cos(ang), -jnp.sin(ang)], axis=0) * scale).astype(BF16)
    return pl.pallas_call(
        functools.partial(_fourier_out_kernel, bpb=bpb, n_lat=n_lat),
        grid=(t // tm,),
        in_specs=[
            pl.BlockSpec((1, 1, tm, d), lambda i: (i // bpb, 0, i % bpb, 0)),
            pl.BlockSpec((1, 1, tm, d), lambda i: (i // bpb, 1, i % bpb, 0)),
            pl.BlockSpec((tm, d), lambda i: (i, 0)),
            pl.BlockSpec((2 * gw, gw), lambda i: (0, 0)),
            pl.BlockSpec((d, d), lambda i: (0, 0)),
            pl.BlockSpec((1, d), lambda i: (0, 0)),
            pl.BlockSpec(mod.shape, lambda i: (0, 0)),
        ],
        out_specs=pl.BlockSpec((tm, d), lambda i: (i, 0)),
        out_shape=jax.ShapeDtypeStruct((t, d), F32),
        compiler_params=_cparams("parallel"),
        name="fourier_out",
    )(pq, pq, x, cm, w_out.astype(BF16), b_out.reshape(1, d), mod)


def _head_perm():
    return jnp.concatenate([jnp.arange(0, HEAD_DIM, 2), jnp.arange(1, HEAD_DIM, 2)])


def _rope_tables(seq):
    rows = seq // GRID_W
    row = jnp.repeat(jnp.arange(rows, dtype=F32), GRID_W)
    col = jnp.tile(jnp.arange(GRID_W, dtype=F32), rows)
    n_pairs_axis = HEAD_DIM // 4
    inv = ROPE_THETA ** (-jnp.arange(n_pairs_axis, dtype=F32) / n_pairs_axis)
    ang = jnp.concatenate([row[:, None] * inv, col[:, None] * inv], axis=-1)
    cos, sin = jnp.cos(ang), jnp.sin(ang)
    return jnp.concatenate([cos, cos], axis=-1), jnp.concatenate([-sin, sin], axis=-1)


def _qkv_kernel(x_ref, g_ref, mod_ref, w_ref, gq_ref, gk_ref, cos_ref, sin_ref, qt_ref, k_ref, v_ref, *, bpb, n_lat):
    d = x_ref.shape[1]
    b = _batch_row(bpb, n_lat)
    h = _norm_mod(x_ref[...], g_ref[...], mod_ref, b, 0, d)
    u = jnp.dot(h.astype(BF16), w_ref[...], preferred_element_type=F32)
    cos, sin = cos_ref[...], sin_ref[...]
    kv_dim = k_ref.shape[1]
    q_dim = Q_PER_KV * kv_dim
    tq = qt_ref.shape[4] // Q_PER_KV

    def norm_rope(xh, gain):
        xh = _rms(xh, gain)
        return xh * cos + pltpu.roll(xh, HEAD_DIM // 2, 1) * sin

    q_scale = HEAD_DIM ** -0.5 * math.log2(math.e)
    for hh in range(q_dim // HEAD_DIM):
        sl = slice(hh * HEAD_DIM, (hh + 1) * HEAD_DIM)
        kvh, g = divmod(hh, Q_PER_KV)
        q_t = (norm_rope(u[:, sl], gq_ref[...]) * q_scale).T.astype(BF16)
        for qb in range(qt_ref.shape[2]):
            qt_ref[0, kvh, qb, :, g * tq:(g + 1) * tq] = q_t[:, qb * tq:(qb + 1) * tq]
    for hh in range(kv_dim // HEAD_DIM):
        sl = slice(hh * HEAD_DIM, (hh + 1) * HEAD_DIM)
        k_ref[:, sl] = norm_rope(u[:, q_dim + hh * HEAD_DIM:q_dim + (hh + 1) * HEAD_DIM], gk_ref[...]).astype(BF16)
    v_ref[...] = u[:, q_dim + kv_dim:].astype(BF16)


def _qkv(x, g, mod, w_qkv, gq, gk, cos, sin, *, seq, tq, tm, bpb, n_lat):
    t, d = x.shape
    assert tm % tq == 0 and seq % tm == 0
    q_dim = N_KV_HEADS * Q_PER_KV * HEAD_DIM
    kv_dim = N_KV_HEADS * HEAD_DIM
    perm = _head_perm()
    n_heads_qk = (q_dim + kv_dim) // HEAD_DIM
    cols = (jnp.arange(n_heads_qk)[:, None] * HEAD_DIM + perm[None, :]).reshape(-1)
    cols = jnp.concatenate([cols, jnp.arange(q_dim + kv_dim, q_dim + 2 * kv_dim)])
    w = w_qkv[:, cols].astype(BF16)
    n_pos = cos.shape[0] // tm
    return pl.pallas_call(
        functools.partial(_qkv_kernel, bpb=bpb, n_lat=n_lat),
        grid=(t // tm,),
        in_specs=[
            pl.BlockSpec((tm, d), lambda i: (i, 0)),
            pl.BlockSpec((1, d), lambda i: (0, 0)),
            pl.BlockSpec(mod.shape, lambda i: (0, 0)),
            pl.BlockSpec((d, q_dim + 2 * kv_dim), lambda i: (0, 0)),
            pl.BlockSpec((1, HEAD_DIM), lambda i: (0, 0)),
            pl.BlockSpec((1, HEAD_DIM), lambda i: (0, 0)),
            pl.BlockSpec((tm, HEAD_DIM), lambda i: (i % n_pos, 0)),
            pl.BlockSpec((tm, HEAD_DIM), lambda i: (i % n_pos, 0)),
        ],
        out_specs=[
            pl.BlockSpec((1, N_KV_HEADS, tm // tq, HEAD_DIM, Q_PER_KV * tq),
                         lambda i: (i // (seq // tm), 0, i % (seq // tm), 0, 0)),
            pl.BlockSpec((tm, kv_dim), lambda i: (i, 0)),
            pl.BlockSpec((tm, kv_dim), lambda i: (i, 0)),
        ],
        out_shape=[
            jax.ShapeDtypeStruct((t // seq, N_KV_HEADS, seq // tq, HEAD_DIM, Q_PER_KV * tq), BF16),
            jax.ShapeDtypeStruct((t, kv_dim), BF16),
            jax.ShapeDtypeStruct((t, kv_dim), BF16),
        ],
        compiler_params=_cparams("parallel"),
        name="qkv_proj",
    )(x, g.reshape(1, d), mod, w, gq[perm].reshape(1, HEAD_DIM), gk[perm].reshape(1, HEAD_DIM), cos, sin)


ATTN_TQ = 256
ATTN_ROW_CHUNK = 16
ATTN_MAX_TK = 1408


def _attn_kernel(qt_ref, k_ref, vt_ref, o_ref, *scratch, tq, tk, nk):
    s_ref, p_ref, acc_ref = scratch
    nq = Q_PER_KV * tq
    acc_ref[...] = jnp.zeros(acc_ref.shape, F32)
    rc = ATTN_ROW_CHUNK

    def kv_chunk(j, carry):
        m_prev, l_prev = carry
        kk = k_ref[0, pl.ds(pl.multiple_of(j * tk, tk), tk), :]
        s_ref[...] = jnp.dot(kk, qt_ref[0, 0, 0], preferred_element_type=F32)
        mx = s_ref[0:SUBLANES, :]
        for r in range(1, tk // SUBLANES):
            mx = jnp.maximum(mx, s_ref[r * SUBLANES:(r + 1) * SUBLANES, :])
        m_new = jnp.maximum(m_prev, jnp.max(mx, axis=0, keepdims=True))
        psum = jnp.zeros((SUBLANES, nq), F32)
        for r in range(tk // rc):
            p = jnp.exp2(s_ref[r * rc:(r + 1) * rc, :] - m_new)
            p_ref[r * rc:(r + 1) * rc, :] = p.astype(BF16)
            for h in range(rc // SUBLANES):
                psum = psum + p[h * SUBLANES:(h + 1) * SUBLANES]
        alpha = jnp.exp2(m_prev - m_new)
        acc_ref[...] = alpha * acc_ref[...] + jnp.dot(vt_ref[0, 0, j], p_ref[...], preferred_element_type=F32)
        return m_new, alpha * l_prev + jnp.sum(psum, axis=0, keepdims=True)

    init = (jnp.full((1, nq), -jnp.inf, F32), jnp.zeros((1, nq), F32))
    _, l_all = lax.fori_loop(0, nk, kv_chunk, init)
    out = (acc_ref[...] / l_all).T
    for g in range(Q_PER_KV):
        o_ref[0, :, g * HEAD_DIM:(g + 1) * HEAD_DIM] = out[g * tq:(g + 1) * tq].astype(o_ref.dtype)


def _attention(qt, k_all, v_all, *, tk):
    bsz, _, n_qblk, _, nq = qt.shape
    tq = nq // Q_PER_KV
    seq = n_qblk * tq
    lk = k_all.shape[1]
    gq = Q_PER_KV * HEAD_DIM
    q_dim = N_KV_HEADS * gq
    nk = lk // tk
    assert tk % ATTN_ROW_CHUNK == 0 and nq % LANES == 0
    vt = v_all.reshape(bsz, nk, tk, N_KV_HEADS, HEAD_DIM).transpose(0, 3, 1, 4, 2)
    return pl.pallas_call(
        functools.partial(_attn_kernel, tq=tq, tk=tk, nk=nk),
        grid=(bsz, N_KV_HEADS, seq // tq),
        in_specs=[
            pl.BlockSpec((1, 1, 1, HEAD_DIM, nq), lambda b, h, i: (b, h, i, 0, 0)),
            pl.BlockSpec((1, lk, HEAD_DIM), lambda b, h, i: (b, 0, h)),
            pl.BlockSpec((1, 1, nk, HEAD_DIM, tk), lambda b, h, i: (b, h, 0, 0, 0)),
        ],
        out_specs=pl.BlockSpec((1, tq, gq), lambda b, h, i: (b, i, h)),
        out_shape=jax.ShapeDtypeStruct((bsz, seq, q_dim), BF16),
        scratch_shapes=[pltpu.VMEM((tk, nq), F32), pltpu.VMEM((tk, nq), BF16), pltpu.VMEM((HEAD_DIM, nq), F32)],
        compiler_params=_cparams("parallel", "parallel", "parallel"),
        name="flash_attention",
    )(qt, k_all, vt)


def _proj_residual_kernel(a_ref, x_ref, w_ref, mod_ref, o_ref, *, bpb, n_lat):
    d = x_ref.shape[1]
    b = _batch_row(bpb, n_lat)
    y = jnp.dot(a_ref[...], w_ref[...], preferred_element_type=F32)
    o_ref[...] = x_ref[...] + _mod_row(mod_ref, b, 2, d) * y


def _proj_residual(a, x, mod, w, *, tm, bpb, n_lat):
    t, d = x.shape
    ka = a.shape[1]
    return pl.pallas_call(
        functools.partial(_proj_residual_kernel, bpb=bpb, n_lat=n_lat),
        grid=(t // tm,),
        in_specs=[
            pl.BlockSpec((tm, ka), lambda i: (i, 0)),
            pl.BlockSpec((tm, d), lambda i: (i, 0)),
            pl.BlockSpec((ka, d), lambda i: (0, 0)),
            pl.BlockSpec(mod.shape, lambda i: (0, 0)),
        ],
        out_specs=pl.BlockSpec((tm, d), lambda i: (i, 0)),
        out_shape=jax.ShapeDtypeStruct((t, d), F32),
        compiler_params=_cparams("parallel"),
        name="proj_residual",
    )(a, x, w.astype(BF16), mod)


ROUTE_E, ROUTE_GATE, ROUTE_RANK = 0, 2, 4
ROUTER_LOGIT0 = N_EXPERT_GROUPS


U32 = jnp.uint32
BF16_HIGH_BITS = 0xFFFF0000


def _pack_bf16_pairs(x):
    half = x.shape[1] // 2
    hi = lax.bitcast_convert_type(x[:, :half].astype(BF16).astype(F32), U32)
    lo = lax.bitcast_convert_type(x[:, half:].astype(BF16).astype(F32), U32)
    return (hi & U32(BF16_HIGH_BITS)) | (lo >> 16)


def _unpack_bf16_pairs(w):
    return (lax.bitcast_convert_type(w & U32(BF16_HIGH_BITS), F32), lax.bitcast_convert_type(w << 16, F32))


def _router_kernel(x_ref, g_ref, mod_ref, wr_ref, br_ref, tri_ref, h_ref, r_ref, cnt_ref, carry, *, bpb, n_lat):
    d = x_ref.shape[1]
    tm = x_ref.shape[0]
    i = pl.program_id(0)
    b = _batch_row(bpb, n_lat)

    @pl.when(i == 0)
    def _():
        carry[...] = jnp.zeros(carry.shape, F32)

    h = _norm_mod(x_ref[...], g_ref[...], mod_ref, b, 3, d)
    h_ref[...] = _pack_bf16_pairs(h)
    h_hi = h.astype(BF16)
    h_lo = (h - h_hi.astype(F32)).astype(BF16)
    logits = (jnp.dot(h_hi, wr_ref[0], preferred_element_type=F32) + jnp.dot(h_lo, wr_ref[0], preferred_element_type=F32)
              + jnp.dot(h_hi, wr_ref[1], preferred_element_type=F32) + br_ref[...])
    lane = lax.broadcasted_iota(jnp.int32, (tm, LANES), 1).astype(F32)
    big = float(LANES)
    neg = -jnp.inf
    gmask = lane < N_EXPERT_GROUPS
    gl = jnp.where(gmask, logits, neg)
    ge = jnp.exp(gl - jnp.max(gl, axis=-1, keepdims=True))
    gp = ge / jnp.sum(ge, axis=-1, keepdims=True)
    g_top = jnp.max(gp, axis=-1, keepdims=True)
    g_idx = jnp.min(jnp.where(gmask & (gp == g_top), lane, big), axis=-1, keepdims=True)
    lo = ROUTER_LOGIT0 + g_idx * EXPERTS_PER_GROUP
    emask = (lane >= lo) & (lane < lo + EXPERTS_PER_GROUP)
    el = jnp.where(emask, logits, neg)
    ee = jnp.exp(el - jnp.max(el, axis=-1, keepdims=True))
    ep = ee / jnp.sum(ee, axis=-1, keepdims=True)
    p1 = jnp.max(ep, axis=-1, keepdims=True)
    i1 = jnp.min(jnp.where(emask & (ep == p1), lane, big), axis=-1, keepdims=True)
    rest = emask & (lane != i1)
    p2 = jnp.max(jnp.where(rest, ep, -1.0), axis=-1, keepdims=True)
    i2 = jnp.min(jnp.where(rest & (ep == p2), lane, big), axis=-1, keepdims=True)
    denom = p1 + p2
    gate1 = g_top * p1 / denom
    gate2 = g_top * p2 / denom
    hit1 = lane == i1
    hit2 = lane == i2
    onehot = jnp.where(hit1 | hit2, 1.0, 0.0)
    cum = jnp.dot(tri_ref[...], onehot.astype(BF16), preferred_element_type=F32) + carry[...]
    rank1 = jnp.sum(jnp.where(hit1, cum, 0.0), axis=-1, keepdims=True)
    rank2 = jnp.sum(jnp.where(hit2, cum, 0.0), axis=-1, keepdims=True)
    carry[...] = carry[...] + jnp.sum(onehot, axis=0, keepdims=True)
    cnt_ref[...] = carry[...]
    rec = jnp.zeros((tm, LANES), F32)
    for ln, val in ((ROUTE_E, i1 - ROUTER_LOGIT0), (ROUTE_E + 1, i2 - ROUTER_LOGIT0), (ROUTE_GATE, gate1),
                    (ROUTE_GATE + 1, gate2), (ROUTE_RANK, rank1), (ROUTE_RANK + 1, rank2)):
        rec = jnp.where(lane == float(ln), val, rec)
    r_ref[...] = rec


def _router(x, g, mod, w_group, b_group, w_expert, b_expert, *, tm, bpb, n_lat):
    t, d = x.shape
    n_log = N_EXPERT_GROUPS + N_EXPERTS
    wr = jnp.zeros((d, LANES), F32).at[:, :N_EXPERT_GROUPS].set(w_group).at[:, N_EXPERT_GROUPS:n_log].set(w_expert)
    br = jnp.zeros((1, LANES), F32).at[0, :N_EXPERT_GROUPS].set(b_group).at[0, N_EXPERT_GROUPS:n_log].set(b_expert)
    wr_hi = wr.astype(BF16)
    wr = jnp.stack([wr_hi, (wr - wr_hi.astype(F32)).astype(BF16)])
    tri = (jnp.arange(tm)[:, None] > jnp.arange(tm)[None, :]).astype(BF16)
    return pl.pallas_call(
        functools.partial(_router_kernel, bpb=bpb, n_lat=n_lat),
        grid=(t // tm,),
        in_specs=[
            pl.BlockSpec((tm, d), lambda i: (i, 0)),
            pl.BlockSpec((1, d), lambda i: (0, 0)),
            pl.BlockSpec(mod.shape, lambda i: (0, 0)),
            pl.BlockSpec((2, d, LANES), lambda i: (0, 0, 0)),
            pl.BlockSpec((1, LANES), lambda i: (0, 0)),
            pl.BlockSpec((tm, tm), lambda i: (0, 0)),
        ],
        out_specs=[
            pl.BlockSpec((tm, d // 2), lambda i: (i, 0)),
            pl.BlockSpec((tm, LANES), lambda i: (i, 0)),
            pl.BlockSpec((1, LANES), lambda i: (0, 0)),
        ],
        out_shape=[
            jax.ShapeDtypeStruct((t, d // 2), U32),
            jax.ShapeDtypeStruct((t, LANES), F32),
            jax.ShapeDtypeStruct((1, LANES), F32),
        ],
        scratch_shapes=[pltpu.VMEM((1, LANES), F32)],
        compiler_params=_cparams("arbitrary"),
        name="moe_router",
    )(x, g.reshape(1, d), mod, wr, br, tri)


def _moe_ffn_kernel(be_ref, bv_ref, x_ref, wg_ref, wu_ref, wd_ref, o_ref, wgb, wub, wdb):
    i = pl.program_id(0)
    e = be_ref[i]
    e_prev = be_ref[jnp.maximum(i - 1, 0)]

    @pl.when((i == 0) | (e != e_prev))
    def _():
        wgb[...] = wg_ref[0, 0].astype(BF16)
        wub[...] = wu_ref[0, 0].astype(BF16)
        wdb[...] = wd_ref[0, 0].astype(BF16)

    n_valid = bv_ref[i]

    @pl.when(n_valid > 0)
    def _():
        half = x_ref.shape[2]
        x_a, x_b = (v.astype(BF16) for v in _unpack_bf16_pairs(x_ref[:, 0, :]))

        def proj(w):
            return (jnp.dot(x_a, w[:half], preferred_element_type=F32)
                    + jnp.dot(x_b, w[half:], preferred_element_type=F32))

        gt, up = proj(wgb), proj(wub)
        a = gt * _sigmoid(gt) * up
        o_ref[:, 0, :] = _pack_bf16_pairs(jnp.dot(a.astype(BF16), wdb[...], preferred_element_type=F32))

    @pl.when(n_valid <= 0)
    def _():
        o_ref[...] = jnp.zeros(o_ref.shape, U32)


def _moe_ffn(buf, blk_expert, blk_valid, w_gate, w_up, w_down, layer):
    p = buf.shape[0]
    d = w_gate.shape[-2]
    de = w_gate.shape[-1]
    n_blocks = p // MOE_BLOCK
    grid_spec = pltpu.PrefetchScalarGridSpec(
        num_scalar_prefetch=2,
        grid=(n_blocks,),
        in_specs=[
            pl.BlockSpec((MOE_BLOCK, 1, d // 2), lambda i, be, bv: (i, 0, 0)),
            pl.BlockSpec((1, 1, d, de), lambda i, be, bv: (layer, be[i], 0, 0)),
            pl.BlockSpec((1, 1, d, de), lambda i, be, bv: (layer, be[i], 0, 0)),
            pl.BlockSpec((1, 1, de, d), lambda i, be, bv: (layer, be[i], 0, 0)),
        ],
        out_specs=pl.BlockSpec((MOE_BLOCK, 1, d // 2), lambda i, be, bv: (i, 0, 0)),
        scratch_shapes=[pltpu.VMEM((d, de), BF16), pltpu.VMEM((d, de), BF16), pltpu.VMEM((de, d), BF16)],
    )
    return pl.pallas_call(
        _moe_ffn_kernel,
        grid_spec=grid_spec,
        out_shape=jax.ShapeDtypeStruct((p, 1, d // 2), U32),
        compiler_params=_cparams("arbitrary"),
        name="moe_ffn",
    )(blk_expert, blk_valid, buf, w_gate, w_up, w_down)


DMA_ISSUE_UNROLL = 8


def _row_copy(src, src_row, dst, dst_row, sem):
    return pltpu.make_async_copy(src.at[pl.ds(src_row, 1)], dst.at[pl.ds(dst_row, 1)], sem)


def _moe_dispatch_kernel(pe_ref, dest_ref, h_ref, buf_ref, zero_ref, sems, zsem):
    tm = h_ref.shape[0] * SUBLANES
    n_blocks = buf_ref.shape[0] // MOE_BLOCK

    def zero_block(first_row):
        return pltpu.make_async_copy(zero_ref, buf_ref.at[pl.ds(pl.multiple_of(first_row, MOE_BLOCK), MOE_BLOCK)], zsem)

    @pl.when(pl.program_id(0) == 0)
    def _():
        zero_ref[...] = jnp.zeros(zero_ref.shape, zero_ref.dtype)

        def expert_has_rows(e):
            return pe_ref[e] > jnp.where(e > 0, pe_ref[jnp.maximum(e - 1, 0)], 0)

        def fill(start):
            def body(e, carry):
                @pl.when(expert_has_rows(e))
                def _():
                    cp = zero_block(pe_ref[e] - MOE_BLOCK)
                    cp.start() if start else cp.wait()
                return carry
            return body

        def tail(start):
            def body(b, carry):
                cp = zero_block(b * MOE_BLOCK)
                cp.start() if start else cp.wait()
                return carry
            return body

        first_unused = pe_ref[N_EXPERTS - 1] // MOE_BLOCK
        for start in (True, False):
            lax.fori_loop(0, N_EXPERTS, fill(start), 0)
            lax.fori_loop(first_unused, n_blocks, tail(start), 0)

    def issue(tile, carry):
        for k in range(SUBLANES):
            for j in range(2):
                slot = dest_ref[0, 0, 2 * SUBLANES * tile + 2 * k + j]
                pltpu.make_async_copy(h_ref.at[tile, pl.ds(k, 1)], buf_ref.at[slot], sems.at[j]).start()
        return carry

    lax.fori_loop(0, tm // SUBLANES, issue, 0)
    for j in range(2):
        rows = buf_ref.at[pl.ds(0, tm)]
        pltpu.make_async_copy(rows, rows, sems.at[j]).wait()


def _moe_dispatch(h, dest, pad_end, n_slots, *, tm):
    t, d = h.shape
    grid_spec = pltpu.PrefetchScalarGridSpec(
        num_scalar_prefetch=1,
        grid=(t // tm,),
        in_specs=[
            pl.BlockSpec((1, 1, 2 * tm), lambda i, pe: (i, 0, 0), memory_space=pltpu.SMEM),
            pl.BlockSpec((tm // SUBLANES, SUBLANES, d), lambda i, pe: (i, 0, 0)),
        ],
        out_specs=pl.BlockSpec(memory_space=pl.ANY),
        scratch_shapes=[pltpu.VMEM((MOE_BLOCK, 1, d), h.dtype), pltpu.SemaphoreType.DMA((2,)), pltpu.SemaphoreType.DMA(())],
    )
    return pl.pallas_call(
        _moe_dispatch_kernel,
        grid_spec=grid_spec,
        out_shape=jax.ShapeDtypeStruct((n_slots, 1, d), h.dtype),
        compiler_params=_cparams("arbitrary"),
        name="moe_dispatch",
    )(pad_end.astype(jnp.int32), dest.reshape(t // tm, 1, 2 * tm), h.reshape(t // SUBLANES, SUBLANES, d))


def _moe_combine_kernel(dest_ref, x_ref, r_ref, mod_ref, yb_ref, o_ref, y0, y1, sems, *, bpb, n_lat):
    tm, d = x_ref.shape
    b = _batch_row(bpb, n_lat)
    ys = (y0, y1)

    def issue(tile, carry):
        for k in range(SUBLANES):
            for j in range(2):
                slot = dest_ref[0, 0, 2 * SUBLANES * tile + 2 * k + j]
                pltpu.make_async_copy(yb_ref.at[slot], ys[j].at[tile, pl.ds(k, 1)], sems.at[j]).start()
        return carry

    lax.fori_loop(0, tm // SUBLANES, issue, 0)
    for j in range(2):
        rows = yb_ref.at[pl.ds(0, tm)]
        pltpu.make_async_copy(rows, rows, sems.at[j]).wait()
    r = r_ref[...]
    half = d // 2
    gated = [tuple(v * r[:, ROUTE_GATE + j:ROUTE_GATE + j + 1] for v in _unpack_bf16_pairs(ys[j][...].reshape(tm, half)))
             for j in range(2)]
    gate = _mod_row(mod_ref, b, 5, d)
    for c in range(2):
        cols = slice(c * half, (c + 1) * half)
        o_ref[:, cols] = x_ref[:, cols] + gate[:, cols] * (gated[0][c] + gated[1][c])


def _moe_combine(x, yb, dest, route, mod, *, tm, bpb, n_lat):
    t, d = x.shape
    row = pl.BlockSpec((tm, d), lambda i: (i, 0))
    return pl.pallas_call(
        functools.partial(_moe_combine_kernel, bpb=bpb, n_lat=n_lat),
        grid=(t // tm,),
        in_specs=[
            pl.BlockSpec((1, 1, 2 * tm), lambda i: (i, 0, 0), memory_space=pltpu.SMEM),
            row,
            pl.BlockSpec((tm, LANES), lambda i: (i, 0)),
            pl.BlockSpec(mod.shape, lambda i: (0, 0)),
            pl.BlockSpec(memory_space=pl.ANY),
        ],
        out_specs=row,
        out_shape=jax.ShapeDtypeStruct((t, d), F32),
        scratch_shapes=[pltpu.VMEM((tm // SUBLANES, SUBLANES, d // 2), U32) for _ in range(2)] + [pltpu.SemaphoreType.DMA((2,))],
        compiler_params=_cparams("arbitrary"),
        name="moe_combine",
    )(dest.reshape(t // tm, 1, 2 * tm), x, route, mod, yb)


def _hier_moe_residual(x, g, mod, w_group, b_group, w_expert, b_expert, w_gate, w_up, w_down, layer, *, tm, bpb, n_lat):
    t, d = x.shape
    h, route, counts = _router(x, g, mod, w_group, b_group, w_expert, b_expert, tm=tm, bpb=bpb, n_lat=n_lat)
    counts = counts[0, ROUTER_LOGIT0:ROUTER_LOGIT0 + N_EXPERTS].astype(jnp.int32)
    expert = route[:, ROUTE_E:ROUTE_E + 2].astype(jnp.int32)
    rank = route[:, ROUTE_RANK:ROUTE_RANK + 2].astype(jnp.int32)
    padded = (counts + MOE_BLOCK - 1) // MOE_BLOCK * MOE_BLOCK
    pad_end = jnp.cumsum(padded)
    pad_start = pad_end - padded
    dest = jnp.sum(jnp.where(expert[:, :, None] == jnp.arange(N_EXPERTS), pad_start, 0), axis=-1) + rank
    n_blocks = -(-(2 * t) // MOE_BLOCK) + N_EXPERTS
    blk_first = jnp.arange(n_blocks, dtype=jnp.int32) * MOE_BLOCK
    blk_expert = jnp.minimum(jnp.sum(blk_first[:, None] >= pad_end[None, :], axis=1), N_EXPERTS - 1).astype(jnp.int32)
    blk_rows = jnp.clip((pad_start + counts)[blk_expert] - blk_first, 0, MOE_BLOCK).astype(jnp.int32)
    buf = _moe_dispatch(h, dest, pad_end, n_blocks * MOE_BLOCK, tm=tm)
    yb = _moe_ffn(buf, blk_expert, blk_rows, w_gate, w_up, w_down, layer)
    return _moe_combine(x, yb, dest, route, mod, tm=tm, bpb=bpb, n_lat=n_lat)


def kernel(x, c, ctx, c_ctx, norm_g, w_mod, b_mod, conv_w_in, conv_b_in, conv_w_dw, conv_b_dw, conv_norm_g, conv_w_out, conv_b_out, fnet_w_out, fnet_b_out, attn_w_qkv, attn_q_norm_g, attn_k_norm_g, attn_w_out, moe_w_group, moe_b_group, moe_w_expert, moe_b_expert, moe_w_gate, moe_w_up, moe_w_down):
    bsz, seq, d = x.shape
    n_ctx = ctx.shape[1]
    depth = norm_g.shape[0]
    assert bsz <= MOD_ROWS // 2 and seq % DFT_NA == 0 and seq % GRID_W == 0
    t_lat, t_ctx = bsz * seq, bsz * n_ctx
    tm = min(512, seq)
    tmc = min(tm, n_ctx)
    assert seq % tm == 0 and n_ctx % tmc == 0 and tm % tmc == 0
    lat = dict(tm=tm, bpb=seq // tm, n_lat=t_lat // tm)
    cx = dict(tm=tmc, bpb=1, n_lat=0)
    tmb = tm if t_ctx % tm == 0 else tmc
    both = dict(tm=tmb, bpb=seq // tmb, n_lat=t_lat // tmb)
    last_reader = max([i for i in range(depth) if i % N_MIXERS == 2], default=-1)

    cvec = jnp.zeros((MOD_ROWS, d), F32).at[:bsz].set(c).at[MOD_ROWS // 2].set(c_ctx)
    mod_all = _modulation(cvec, w_mod, b_mod)
    cos, sin = _rope_tables(seq)

    xl = x.reshape(t_lat, d)
    xc = ctx.reshape(t_ctx, d)
    for i in range(depth):
        m, j = i % N_MIXERS, i // N_MIXERS
        ctx_on = i <= last_reader
        ctx_full = i < last_reader
        mod = mod_all[i]
        g1, g2 = norm_g[i, 0], norm_g[i, 1]
        if m == 0:
            cp = (conv_w_dw[j], conv_b_dw[j], conv_norm_g[j], conv_w_out[j], conv_b_out[j])
            v = _conv_in(xl, g1, mod, conv_w_in[j], conv_b_in[j], **lat)
            xl = _conv_out(v, xl, mod, *cp, **lat)
            if ctx_full:
                vc = _conv_in(xc, g1, mod, conv_w_in[j], conv_b_in[j], **cx)
                xc = _conv_out(vc, xc, mod, *cp, tm=n_ctx, bpb=1, n_lat=0)
        elif m == 1:
            hl = _prenorm(xl, g1, mod, dtype=F32, **lat)
            pq = _seq_dft_two_stage(hl, bsz, seq)
            xl = _fourier_out(pq, xl, mod, fnet_w_out[j], fnet_b_out[j], seq=seq, **lat)
            if ctx_full:
                hc = _prenorm(xc, g1, mod, dtype=BF16, **cx)
                pqc = _seq_dft_dense(hc, bsz, n_ctx)
                xc = _fourier_out(pqc, xc, mod, fnet_w_out[j], fnet_b_out[j], seq=n_ctx, tm=n_ctx, bpb=1, n_lat=0)
        else:
            gq, gk = attn_q_norm_g[j], attn_k_norm_g[j]
            qt, k, vv = _qkv(xl, g1, mod, attn_w_qkv[j], gq, gk, cos, sin, seq=seq, tq=min(ATTN_TQ, seq), **lat)
            ones = jnp.ones((tmc, HEAD_DIM), F32)
            _, kc, vc = _qkv(xc, g1, mod, attn_w_qkv[j], gq, gk, ones, jnp.zeros_like(ones),
                             seq=n_ctx, tq=min(ATTN_TQ, tmc), **cx)
            kv_dim = k.shape[1]
            k_all = jnp.concatenate([kc.reshape(bsz, n_ctx, kv_dim), k.reshape(bsz, seq, kv_dim)], axis=1)
            v_all = jnp.concatenate([vc.reshape(bsz, n_ctx, kv_dim), vv.reshape(bsz, seq, kv_dim)], axis=1)
            lk = seq + n_ctx
            tk = max(t for t in range(LANES, ATTN_MAX_TK + 1, LANES) if lk % t == 0)
            o = _attention(qt, k_all, v_all, tk=tk)
            xl = _proj_residual(o.reshape(t_lat, -1), xl, mod, attn_w_out[j], **lat)
            if ctx_full:
                raise NotImplementedError("context-stream attention output is only needed before the last reader layer")
        mp = (moe_w_group[i], moe_b_group[i], moe_w_expert[i], moe_b_expert[i], moe_w_gate, moe_w_up, moe_w_down, i)
        if ctx_full:
            xa = _hier_moe_residual(jnp.concatenate([xl, xc], axis=0), g2, mod, *mp, **both)
            xl, xc = xa[:t_lat], xa[t_lat:]
        else:
            xl = _hier_moe_residual(xl, g2, mod, *mp, **lat)
    return xl.reshape(bsz, seq, d)
```

```python
import functools
import math

import jax
import jax.numpy as jnp
from jax import lax
from jax.experimental import pallas as pl
from jax.experimental.pallas import tpu as pltpu

F32 = jnp.float32
BF16 = jnp.bfloat16
HIGHEST = lax.Precision.HIGHEST

EPS = 1e-6
GRID_W = 64
N_MIXERS = 3
CONV_WIDTH = 31
CONV_HALO = 16
CONV_ROW_CHUNK = 32
CONV_COL_CHUNK = 256
SUBLANES = 8
FNET_GROUPS = 4
HEAD_DIM = 128
N_KV_HEADS = 2
Q_PER_KV = 4
ROPE_THETA = 10000.0
N_EXPERT_GROUPS = 4
EXPERTS_PER_GROUP = 8
N_EXPERTS = N_EXPERT_GROUPS * EXPERTS_PER_GROUP
MOE_BLOCK = 256
LANES = 128
DFT_NA = 128
MOD_ROWS = 8
VMEM_LIMIT = 56 * 1024 * 1024


def _cparams(*sem):
    return pltpu.CompilerParams(dimension_semantics=sem, vmem_limit_bytes=VMEM_LIMIT)


def _rms(x, g):
    return x * lax.rsqrt(jnp.mean(x * x, axis=-1, keepdims=True) + EPS) * g


def _sigmoid(x):
    return 1.0 / (1.0 + jnp.exp(-x))


def _mod_row(mod_ref, b, k, d):
    return mod_ref[pl.ds(b, 1), pl.ds(k * d, d)]


def _norm_mod(x, g, mod_ref, b, k_shift, d):
    return _rms(x, g) * (1.0 + _mod_row(mod_ref, b, k_shift + 1, d)) + _mod_row(mod_ref, b, k_shift, d)


def _batch_row(bpb, n_lat_blocks):
    i = pl.program_id(0)
    return jnp.where(i < n_lat_blocks, i // bpb, MOD_ROWS // 2)


def _mod_kernel(c_ref, w_ref, b_ref, o_ref):
    c = c_ref[...]
    s = c * _sigmoid(c)
    o_ref[0] = jnp.dot(s, w_ref[0], precision=HIGHEST, preferred_element_type=F32) + b_ref[0]


def _modulation(cvec, w_mod, b_mod):
    depth, d, n = w_mod.shape
    tn = 1024
    return pl.pallas_call(
        _mod_kernel,
        grid=(depth, n // tn),
        in_specs=[
            pl.BlockSpec((MOD_ROWS, d), lambda l, j: (0, 0)),
            pl.BlockSpec((1, d, tn), lambda l, j: (l, 0, j)),
            pl.BlockSpec((1, 1, tn), lambda l, j: (l, 0, j)),
        ],
        out_specs=pl.BlockSpec((1, MOD_ROWS, tn), lambda l, j: (l, 0, j)),
        out_shape=jax.ShapeDtypeStruct((depth, MOD_ROWS, n), F32),
        compiler_params=_cparams("parallel", "parallel"),
        name="adaln_mod",
    )(cvec, w_mod, b_mod.reshape(depth, 1, n))


def _conv_in_kernel(x_ref, g_ref, mod_ref, w_ref, b_ref, v_ref, *, bpb, n_lat):
    d = x_ref.shape[1]
    b = _batch_row(bpb, n_lat)
    h = _norm_mod(x_ref[...], g_ref[...], mod_ref, b, 0, d)
    u = jnp.dot(h.astype(BF16), w_ref[...], preferred_element_type=F32) + b_ref[...]
    v_ref[...] = u[:, :d] * _sigmoid(u[:, d:])


def _conv_in(x, g, mod, w_in, b_in, *, tm, bpb, n_lat):
    t, d = x.shape
    return pl.pallas_call(
        functools.partial(_conv_in_kernel, bpb=bpb, n_lat=n_lat),
        grid=(t // tm,),
        in_specs=[
            pl.BlockSpec((tm, d), lambda i: (i, 0)),
            pl.BlockSpec((1, d), lambda i: (0, 0)),
            pl.BlockSpec(mod.shape, lambda i: (0, 0)),
            pl.BlockSpec((d, 2 * d), lambda i: (0, 0)),
            pl.BlockSpec((1, 2 * d), lambda i: (0, 0)),
        ],
        out_specs=pl.BlockSpec((tm, d), lambda i: (i, 0)),
        out_shape=jax.ShapeDtypeStruct((t, d), F32),
        compiler_params=_cparams("parallel"),
        name="conv_in",
    )(x, g.reshape(1, d), mod, w_in.astype(BF16), b_in.reshape(1, 2 * d))


def _conv_out_kernel(vp_ref, vc_ref, vn_ref, x_ref, wdw_ref, bdw_ref, gn_ref, wo_ref, bo_ref, mod_ref,
                     o_ref, buf, z, sh, *, bpb, n_lat, tm):
    d = x_ref.shape[1]
    i = pl.program_id(0)
    j = i % bpb
    b = _batch_row(bpb, n_lat)
    h = CONV_HALO
    buf[0:h, :] = jnp.where(j == 0, 0.0, vp_ref[...])
    buf[h:h + tm, :] = vc_ref[...]
    buf[h + tm:, :] = jnp.where(j == bpb - 1, 0.0, vn_ref[...])
    rc, cw = CONV_ROW_CHUNK, CONV_COL_CHUNK
    off = h - CONV_WIDTH // 2
    n_sh = sh.shape[1]

    def col_chunk(ci, carry):
        c0 = pl.multiple_of(ci * cw, cw)
        for s in range(SUBLANES):
            sh[s] = buf[s:s + n_sh, pl.ds(c0, cw)]
        for r in range(tm // rc):
            acc = jnp.zeros((rc, cw), F32)
            for t in range(CONV_WIDTH):
                q, s = divmod(off + t, SUBLANES)
                row0 = r * rc + q * SUBLANES
                acc = acc + sh[s, row0:row0 + rc, :] * wdw_ref[t:t + 1, pl.ds(c0, cw)]
            z[r * rc:(r + 1) * rc, pl.ds(c0, cw)] = acc
        return carry

    lax.fori_loop(0, d // cw, col_chunk, 0)
    zz = _rms(z[...] + bdw_ref[...], gn_ref[...])
    zz = zz * _sigmoid(zz)
    y = jnp.dot(zz.astype(BF16), wo_ref[...], preferred_element_type=F32) + bo_ref[...]
    o_ref[...] = x_ref[...] + _mod_row(mod_ref, b, 2, d) * y


def _conv_out(v, x, mod, w_dw, b_dw, g_norm, w_out, b_out, *, tm, bpb, n_lat):
    t, d = x.shape
    hb = tm // CONV_HALO
    n_halo = t // CONV_HALO
    wdw = jnp.zeros((32, d), F32).at[:CONV_WIDTH].set(w_dw)
    n_tap_tiles = (CONV_HALO - CONV_WIDTH // 2 + CONV_WIDTH - 1) // SUBLANES
    return pl.pallas_call(
        functools.partial(_conv_out_kernel, bpb=bpb, n_lat=n_lat, tm=tm),
        grid=(t // tm,),
        in_specs=[
            pl.BlockSpec((CONV_HALO, d), lambda i: (jnp.maximum(i * hb - 1, 0), 0)),
            pl.BlockSpec((tm, d), lambda i: (i, 0)),
            pl.BlockSpec((CONV_HALO, d), lambda i: (jnp.minimum((i + 1) * hb, n_halo - 1), 0)),
            pl.BlockSpec((tm, d), lambda i: (i, 0)),
            pl.BlockSpec((32, d), lambda i: (0, 0)),
            pl.BlockSpec((1, d), lambda i: (0, 0)),
            pl.BlockSpec((1, d), lambda i: (0, 0)),
            pl.BlockSpec((d, d), lambda i: (0, 0)),
            pl.BlockSpec((1, d), lambda i: (0, 0)),
            pl.BlockSpec(mod.shape, lambda i: (0, 0)),
        ],
        out_specs=pl.BlockSpec((tm, d), lambda i: (i, 0)),
        out_shape=jax.ShapeDtypeStruct((t, d), F32),
        scratch_shapes=[
            pltpu.VMEM((tm + 2 * CONV_HALO, d), F32),
            pltpu.VMEM((tm, d), F32),
            pltpu.VMEM((SUBLANES, tm + n_tap_tiles * SUBLANES, CONV_COL_CHUNK), F32),
        ],
        compiler_params=_cparams("parallel"),
        name="conv_out",
    )(v, v, v, x, wdw, b_dw.reshape(1, d), g_norm.reshape(1, d), w_out.astype(BF16), b_out.reshape(1, d), mod)


def _prenorm_kernel(x_ref, g_ref, mod_ref, h_ref, *, bpb, n_lat):
    d = x_ref.shape[1]
    b = _batch_row(bpb, n_lat)
    h_ref[...] = _norm_mod(x_ref[...], g_ref[...], mod_ref, b, 0, d).astype(h_ref.dtype)


def _prenorm(x, g, mod, *, tm, bpb, n_lat, dtype):
    t, d = x.shape
    return pl.pallas_call(
        functools.partial(_prenorm_kernel, bpb=bpb, n_lat=n_lat),
        grid=(t // tm,),
        in_specs=[
            pl.BlockSpec((tm, d), lambda i: (i, 0)),
            pl.BlockSpec((1, d), lambda i: (0, 0)),
            pl.BlockSpec(mod.shape, lambda i: (0, 0)),
        ],
        out_specs=pl.BlockSpec((tm, d), lambda i: (i, 0)),
        out_shape=jax.ShapeDtypeStruct((t, d), dtype),
        compiler_params=_cparams("parallel"),
        name="prenorm",
    )(x, g.reshape(1, d), mod)


def _dft1_kernel(h_ref, m_ref, o_ref, *, nb_chunk, nb, na):
    bc = pl.program_id(2)
    for bi in range(nb_chunk):
        b = bc * nb_chunk + bi
        xs = h_ref[0, pl.ds(b, na, stride=nb), :]
        t = jnp.dot(m_ref[bi], xs.astype(BF16), preferred_element_type=F32)
        o_ref[0, 0, bi] = t[:na].astype(BF16)
        o_ref[0, 1, bi] = t[na:].astype(BF16)


def _dft_tables(seq):
    na, nb = DFT_NA, seq // DFT_NA
    ka = jnp.arange(na, dtype=jnp.int32)
    a = jnp.arange(na, dtype=jnp.int32)
    b = jnp.arange(nb, dtype=jnp.int32)
    n = a[None, None, :] * nb + b[:, None, None]
    ang = ((ka[None, :, None] * n) % seq).astype(F32) * (2.0 * math.pi / seq)
    m1 = jnp.concatenate([jnp.cos(ang), -jnp.sin(ang)], axis=1).astype(BF16)
    kb = jnp.arange(nb, dtype=jnp.int32)
    ang2 = ((kb[:, None] * b[None, :]) % nb).astype(F32) * (2.0 * math.pi / nb)
    c2, s2 = jnp.cos(ang2), jnp.sin(ang2)
    m2 = jnp.concatenate([jnp.concatenate([c2, s2], axis=1),
                          jnp.concatenate([s2, -c2], axis=1)], axis=0).astype(BF16)
    return m1, m2


def _const_lhs_matmul_kernel(a_ref, x_ref, o_ref):
    o_ref[0] = jnp.dot(a_ref[...], x_ref[0], preferred_element_type=F32).astype(o_ref.dtype)


def _const_lhs_matmul(a, x, *, cn):
    m, k = a.shape
    bsz, _, n = x.shape
    return pl.pallas_call(
        _const_lhs_matmul_kernel,
        grid=(bsz, n // cn),
        in_specs=[
            pl.BlockSpec((m, k), lambda b, j: (0, 0)),
            pl.BlockSpec((1, k, cn), lambda b, j: (b, 0, j)),
        ],
        out_specs=pl.BlockSpec((1, m, cn), lambda b, j: (b, 0, j)),
        out_shape=jax.ShapeDtypeStruct((bsz, m, n), BF16),
        compiler_params=_cparams("parallel", "parallel"),
        name="const_lhs_matmul",
    )(a, x)


def _seq_dft_two_stage(h, bsz, seq):
    d = h.shape[1]
    na, nb = DFT_NA, seq // DFT_NA
    gw = LANES
    nb_chunk = min(nb, 8)
    m1, m2 = _dft_tables(seq)
    t1 = pl.pallas_call(
        functools.partial(_dft1_kernel, nb_chunk=nb_chunk, nb=nb, na=na),
        grid=(bsz, d // gw, nb // nb_chunk),
        in_specs=[
            pl.BlockSpec((1, seq, gw), lambda b, g, c: (b, 0, g)),
            pl.BlockSpec((nb_chunk, 2 * na, na), lambda b, g, c: (c, 0, 0)),
        ],
        out_specs=pl.BlockSpec((1, 2, nb_chunk, na, gw), lambda b, g, c: (b, 0, c, 0, g)),
        out_shape=jax.ShapeDtypeStruct((bsz, 2, nb, na, d), BF16),
        compiler_params=_cparams("parallel", "parallel", "arbitrary"),
        name="dft_stage1",
    )(h.reshape(bsz, seq, d), m1)
    pq = _const_lhs_matmul(m2, t1.reshape(bsz, 2 * nb, na * d), cn=min(na * d, 8192))
    return pq.reshape(bsz, 2, seq, d)


def _seq_dft_dense(h, bsz, seq):
    d = h.shape[1]
    k = jnp.arange(seq, dtype=jnp.int32)
    ang = ((k[:, None] * k[None, :]) % seq).astype(F32) * (2.0 * math.pi / seq)
    a = jnp.concatenate([jnp.cos(ang), jnp.sin(ang)], axis=0).astype(BF16)
    pq = _const_lhs_matmul(a, h.reshape(bsz, seq, d), cn=d)
    return pq.reshape(bsz, 2, seq, d)


def _fourier_out_kernel(p_ref, q_ref, x_ref, cm_ref, wo_ref, bo_ref, mod_ref, o_ref, *, bpb, n_lat):
    d = x_ref.shape[1]
    gw = d // FNET_GROUPS
    b = _batch_row(bpb, n_lat)
    p, q = p_ref[0, 0], q_ref[0, 0]
    f = [jnp.dot(p[:, g * gw:(g + 1) * gw], cm_ref[:gw], preferred_element_type=F32)
         + jnp.dot(q[:, g * gw:(g + 1) * gw], cm_ref[gw:], preferred_element_type=F32)
         for g in range(FNET_GROUPS)]
    f = jnp.concatenate(f, axis=1).astype(BF16)
    y = jnp.dot(f, wo_ref[...], preferred_element_type=F32) + bo_ref[...]
    o_ref[...] = x_ref[...] + _mod_row(mod_ref, b, 2, d) * y


def _fourier_out(pq, x, mod, w_out, b_out, *, seq, tm, bpb, n_lat):
    t, d = x.shape
    gw = d // FNET_GROUPS
    k = jnp.arange(gw, dtype=jnp.int32)
    ang = ((k[:, None] * k[None, :]) % gw).astype(F32) * (2.0 * math.pi / gw)
    scale = 1.0 / math.sqrt(seq * gw)
    cm = (jnp.concatenate([jnp.cos(ang), -jnp.sin(ang)], axis=0) * scale).astype(BF16)
    return pl.pallas_call(
        functools.partial(_fourier_out_kernel, bpb=bpb, n_lat=n_lat),
        grid=(t // tm,),
        in_specs=[
            pl.BlockSpec((1, 1, tm, d), lambda i: (i // bpb, 0, i % bpb, 0)),
            pl.BlockSpec((1, 1, tm, d), lambda i: (i // bpb, 1, i % bpb, 0)),
            pl.BlockSpec((tm, d), lambda i: (i, 0)),
            pl.BlockSpec((2 * gw, gw), lambda i: (0, 0)),
            pl.BlockSpec((d, d), lambda i: (0, 0)),
            pl.BlockSpec((1, d), lambda i: (0, 0)),
            pl.BlockSpec(mod.shape, lambda i: (0, 0)),
        ],
        out_specs=pl.BlockSpec((tm, d), lambda i: (i, 0)),
        out_shape=jax.ShapeDtypeStruct((t, d), F32),
        compiler_params=_cparams("parallel"),
        name="fourier_out",
    )(pq, pq, x, cm, w_out.astype(BF16), b_out.reshape(1, d), mod)


def _head_perm():
    return jnp.concatenate([jnp.arange(0, HEAD_DIM, 2), jnp.arange(1, HEAD_DIM, 2)])


def _rope_tables(seq):
    rows = seq // GRID_W
    row = jnp.repeat(jnp.arange(rows, dtype=F32), GRID_W)
    col = jnp.tile(jnp.arange(GRID_W, dtype=F32), rows)
    n_pairs_axis = HEAD_DIM // 4
    inv = ROPE_THETA ** (-jnp.arange(n_pairs_axis, dtype=F32) / n_pairs_axis)
    ang = jnp.concatenate([row[:, None] * inv, col[:, None] * inv], axis=-1)
    cos, sin = jnp.cos(ang), jnp.sin(ang)
    return jnp.concatenate([cos, cos], axis=-1), jnp.concatenate([-sin, sin], axis=-1)


def _qkv_kernel(x_ref, g_ref, mod_ref, w_ref, gq_ref, gk_ref, cos_ref, sin_ref, qt_ref, k_ref, v_ref, *, bpb, n_lat):
    d = x_ref.shape[1]
    b = _batch_row(bpb, n_lat)
    h = _norm_mod(x_ref[...], g_ref[...], mod_ref, b, 0, d)
    u = jnp.dot(h.astype(BF16), w_ref[...], preferred_element_type=F32)
    cos, sin = cos_ref[...], sin_ref[...]
    kv_dim = k_ref.shape[1]
    q_dim = Q_PER_KV * kv_dim
    tq = qt_ref.shape[4] // Q_PER_KV

    def norm_rope(xh, gain):
        xh = _rms(xh, gain)
        return xh * cos + pltpu.roll(xh, HEAD_DIM // 2, 1) * sin

    q_scale = HEAD_DIM ** -0.5 * math.log2(math.e)
    for hh in range(q_dim // HEAD_DIM):
        sl = slice(hh * HEAD_DIM, (hh + 1) * HEAD_DIM)
        kvh, g = divmod(hh, Q_PER_KV)
        q_t = (norm_rope(u[:, sl], gq_ref[...]) * q_scale).T.astype(BF16)
        for qb in range(qt_ref.shape[2]):
            qt_ref[0, kvh, qb, :, g * tq:(g + 1) * tq] = q_t[:, qb * tq:(qb + 1) * tq]
    for hh in range(kv_dim // HEAD_DIM):
        sl = slice(hh * HEAD_DIM, (hh + 1) * HEAD_DIM)
        k_ref[:, sl] = norm_rope(u[:, q_dim + hh * HEAD_DIM:q_dim + (hh + 1) * HEAD_DIM], gk_ref[...]).astype(BF16)
    v_ref[...] = u[:, q_dim + kv_dim:].astype(BF16)


def _qkv(x, g, mod, w_qkv, gq, gk, cos, sin, *, seq, tq, tm, bpb, n_lat):
    t, d = x.shape
    assert tm % tq == 0 and seq % tm == 0
    q_dim = N_KV_HEADS * Q_PER_KV * HEAD_DIM
    kv_dim = N_KV_HEADS * HEAD_DIM
    perm = _head_perm()
    n_heads_qk = (q_dim + kv_dim) // HEAD_DIM
    cols = (jnp.arange(n_heads_qk)[:, None] * HEAD_DIM + perm[None, :]).reshape(-1)
    cols = jnp.concatenate([cols, jnp.arange(q_dim + kv_dim, q_dim + 2 * kv_dim)])
    w = w_qkv[:, cols].astype(BF16)
    n_pos = cos.shape[0] // tm
    return pl.pallas_call(
        functools.partial(_qkv_kernel, bpb=bpb, n_lat=n_lat),
        grid=(t // tm,),
        in_specs=[
            pl.BlockSpec((tm, d), lambda i: (i, 0)),
            pl.BlockSpec((1, d), lambda i: (0, 0)),
            pl.BlockSpec(mod.shape, lambda i: (0, 0)),
            pl.BlockSpec((d, q_dim + 2 * kv_dim), lambda i: (0, 0)),
            pl.BlockSpec((1, HEAD_DIM), lambda i: (0, 0)),
            pl.BlockSpec((1, HEAD_DIM), lambda i: (0, 0)),
            pl.BlockSpec((tm, HEAD_DIM), lambda i: (i % n_pos, 0)),
            pl.BlockSpec((tm, HEAD_DIM), lambda i: (i % n_pos, 0)),
        ],
        out_specs=[
            pl.BlockSpec((1, N_KV_HEADS, tm // tq, HEAD_DIM, Q_PER_KV * tq),
                         lambda i: (i // (seq // tm), 0, i % (seq // tm), 0, 0)),
            pl.BlockSpec((tm, kv_dim), lambda i: (i, 0)),
            pl.BlockSpec((tm, kv_dim), lambda i: (i, 0)),
        ],
        out_shape=[
            jax.ShapeDtypeStruct((t // seq, N_KV_HEADS, seq // tq, HEAD_DIM, Q_PER_KV * tq), BF16),
            jax.ShapeDtypeStruct((t, kv_dim), BF16),
            jax.ShapeDtypeStruct((t, kv_dim), BF16),
        ],
        compiler_params=_cparams("parallel"),
        name="qkv_proj",
    )(x, g.reshape(1, d), mod, w, gq[perm].reshape(1, HEAD_DIM), gk[perm].reshape(1, HEAD_DIM), cos, sin)


ATTN_TQ = 256
ATTN_ROW_CHUNK = 16
ATTN_MAX_TK = 1408


def _attn_kernel(qt_ref, k_ref, vt_ref, o_ref, *scratch, tq, tk, nk):
    s_ref, p_ref, acc_ref = scratch
    nq = Q_PER_KV * tq
    acc_ref[...] = jnp.zeros(acc_ref.shape, F32)
    rc = ATTN_ROW_CHUNK

    def kv_chunk(j, carry):
        m_prev, l_prev = carry
        kk = k_ref[0, pl.ds(pl.multiple_of(j * tk, tk), tk), :]
        s_ref[...] = jnp.dot(kk, qt_ref[0, 0, 0], preferred_element_type=F32)
        mx = s_ref[0:SUBLANES, :]
        for r in range(1, tk // SUBLANES):
            mx = jnp.maximum(mx, s_ref[r * SUBLANES:(r + 1) * SUBLANES, :])
        m_new = jnp.maximum(m_prev, jnp.max(mx, axis=0, keepdims=True))
        psum = jnp.zeros((SUBLANES, nq), F32)
        for r in range(tk // rc):
            p = jnp.exp2(s_ref[r * rc:(r + 1) * rc, :] - m_new)
            p_ref[r * rc:(r + 1) * rc, :] = p.astype(BF16)
            for h in range(rc // SUBLANES):
                psum = psum + p[h * SUBLANES:(h + 1) * SUBLANES]
        alpha = jnp.exp2(m_prev - m_new)
        acc_ref[...] = alpha * acc_ref[...] + jnp.dot(vt_ref[0, 0, j], p_ref[...], preferred_element_type=F32)
        return m_new, alpha * l_prev + jnp.sum(psum, axis=0, keepdims=True)

    init = (jnp.full((1, nq), -jnp.inf, F32), jnp.zeros((1, nq), F32))
    _, l_all = lax.fori_loop(0, nk, kv_chunk, init)
    out = (acc_ref[...] / l_all).T
    for g in range(Q_PER_KV):
        o_ref[0, :, g * HEAD_DIM:(g + 1) * HEAD_DIM] = out[g * tq:(g + 1) * tq].astype(o_ref.dtype)


def _attention(qt, k_all, v_all, *, tk):
    bsz, _, n_qblk, _, nq = qt.shape
    tq = nq // Q_PER_KV
    seq = n_qblk * tq
    lk = k_all.shape[1]
    gq = Q_PER_KV * HEAD_DIM
    q_dim = N_KV_HEADS * gq
    nk = lk // tk
    assert tk % ATTN_ROW_CHUNK == 0 and nq % LANES == 0
    vt = v_all.reshape(bsz, nk, tk, N_KV_HEADS, HEAD_DIM).transpose(0, 3, 1, 4, 2)
    return pl.pallas_call(
        functools.partial(_attn_kernel, tq=tq, tk=tk, nk=nk),
        grid=(bsz, N_KV_HEADS, seq // tq),
        in_specs=[
            pl.BlockSpec((1, 1, 1, HEAD_DIM, nq), lambda b, h, i: (b, h, i, 0, 0)),
            pl.BlockSpec((1, lk, HEAD_DIM), lambda b, h, i: (b, 0, h)),
            pl.BlockSpec((1, 1, nk, HEAD_DIM, tk), lambda b, h, i: (b, h, 0, 0, 0)),
        ],
        out_specs=pl.BlockSpec((1, tq, gq), lambda b, h, i: (b, i, h)),
        out_shape=jax.ShapeDtypeStruct((bsz, seq, q_dim), BF16),
        scratch_shapes=[pltpu.VMEM((tk, nq), F32), pltpu.VMEM((tk, nq), BF16), pltpu.VMEM((HEAD_DIM, nq), F32)],
        compiler_params=_cparams("parallel", "parallel", "parallel"),
        name="flash_attention",
    )(qt, k_all, vt)


def _proj_residual_kernel(a_ref, x_ref, w_ref, mod_ref, o_ref, *, bpb, n_lat):
    d = x_ref.shape[1]
    b = _batch_row(bpb, n_lat)
    y = jnp.dot(a_ref[...], w_ref[...], preferred_element_type=F32)
    o_ref[...] = x_ref[...] + _mod_row(mod_ref, b, 2, d) * y


def _proj_residual(a, x, mod, w, *, tm, bpb, n_lat):
    t, d = x.shape
    ka = a.shape[1]
    return pl.pallas_call(
        functools.partial(_proj_residual_kernel, bpb=bpb, n_lat=n_lat),
        grid=(t // tm,),
        in_specs=[
            pl.BlockSpec((tm, ka), lambda i: (i, 0)),
            pl.BlockSpec((tm, d), lambda i: (i, 0)),
            pl.BlockSpec((ka, d), lambda i: (0, 0)),
            pl.BlockSpec(mod.shape, lambda i: (0, 0)),
        ],
        out_specs=pl.BlockSpec((tm, d), lambda i: (i, 0)),
        out_shape=jax.ShapeDtypeStruct((t, d), F32),
        compiler_params=_cparams("parallel"),
        name="proj_residual",
    )(a, x, w.astype(BF16), mod)


ROUTE_E, ROUTE_GATE, ROUTE_RANK = 0, 2, 4
ROUTER_LOGIT0 = N_EXPERT_GROUPS


U32 = jnp.uint32
BF16_HIGH_BITS = 0xFFFF0000


def _pack_bf16_pairs(x):
    half = x.shape[1] // 2
    hi = lax.bitcast_convert_type(x[:, :half].astype(BF16).astype(F32), U32)
    lo = lax.bitcast_convert_type(x[:, half:].astype(BF16).astype(F32), U32)
    return (hi & U32(BF16_HIGH_BITS)) | (lo >> 16)


def _unpack_bf16_pairs(w):
    return (lax.bitcast_convert_type(w & U32(BF16_HIGH_BITS), F32), lax.bitcast_convert_type(w << 16, F32))


def _router_kernel(x_ref, g_ref, mod_ref, wr_ref, br_ref, tri_ref, h_ref, r_ref, cnt_ref, carry, *, bpb, n_lat):
    d = x_ref.shape[1]
    tm = x_ref.shape[0]
    i = pl.program_id(0)
    b = _batch_row(bpb, n_lat)

    @pl.when(i == 0)
    def _():
        carry[...] = jnp.zeros(carry.shape, F32)

    h = _norm_mod(x_ref[...], g_ref[...], mod_ref, b, 3, d)
    h_ref[...] = _pack_bf16_pairs(h)
    h_hi = h.astype(BF16)
    h_lo = (h - h_hi.astype(F32)).astype(BF16)
    logits = (jnp.dot(h_hi, wr_ref[0], preferred_element_type=F32) + jnp.dot(h_lo, wr_ref[0], preferred_element_type=F32)
              + jnp.dot(h_hi, wr_ref[1], preferred_element_type=F32) + br_ref[...])
    lane = lax.broadcasted_iota(jnp.int32, (tm, LANES), 1).astype(F32)
    big = float(LANES)
    neg = -jnp.inf
    gmask = lane < N_EXPERT_GROUPS
    gl = jnp.where(gmask, logits, neg)
    ge = jnp.exp(gl - jnp.max(gl, axis=-1, keepdims=True))
    gp = ge / jnp.sum(ge, axis=-1, keepdims=True)
    g_top = jnp.max(gp, axis=-1, keepdims=True)
    g_idx = jnp.min(jnp.where(gmask & (gp == g_top), lane, big), axis=-1, keepdims=True)
    lo = ROUTER_LOGIT0 + g_idx * EXPERTS_PER_GROUP
    emask = (lane >= lo) & (lane < lo + EXPERTS_PER_GROUP)
    el = jnp.where(emask, logits, neg)
    ee = jnp.exp(el - jnp.max(el, axis=-1, keepdims=True))
    ep = ee / jnp.sum(ee, axis=-1, keepdims=True)
    p1 = jnp.max(ep, axis=-1, keepdims=True)
    i1 = jnp.min(jnp.where(emask & (ep == p1), lane, big), axis=-1, keepdims=True)
    rest = emask & (lane != i1)
    p2 = jnp.max(jnp.where(rest, ep, -1.0), axis=-1, keepdims=True)
    i2 = jnp.min(jnp.where(rest & (ep == p2), lane, big), axis=-1, keepdims=True)
    denom = p1 + p2
    gate1 = g_top * p1 / denom
    gate2 = g_top * p2 / denom
    hit1 = lane == i1
    hit2 = lane == i2
    onehot = jnp.where(hit1 | hit2, 1.0, 0.0)
    cum = jnp.dot(tri_ref[...], onehot.astype(BF16), preferred_element_type=F32) + carry[...]
    rank1 = jnp.sum(jnp.where(hit1, cum, 0.0), axis=-1, keepdims=True)
    rank2 = jnp.sum(jnp.where(hit2, cum, 0.0), axis=-1, keepdims=True)
    carry[...] = carry[...] + jnp.sum(onehot, axis=0, keepdims=True)
    cnt_ref[...] = carry[...]
    rec = jnp.zeros((tm, LANES), F32)
    for ln, val in ((ROUTE_E, i1 - ROUTER_LOGIT0), (ROUTE_E + 1, i2 - ROUTER_LOGIT0), (ROUTE_GATE, gate1),
                    (ROUTE_GATE + 1, gate2), (ROUTE_RANK, rank1), (ROUTE_RANK + 1, rank2)):
        rec = jnp.where(lane == float(ln), val, rec)
    r_ref[...] = rec


def _router(x, g, mod, w_group, b_group, w_expert, b_expert, *, tm, bpb, n_lat):
    t, d = x.shape
    n_log = N_EXPERT_GROUPS + N_EXPERTS
    wr = jnp.zeros((d, LANES), F32).at[:, :N_EXPERT_GROUPS].set(w_group).at[:, N_EXPERT_GROUPS:n_log].set(w_expert)
    br = jnp.zeros((1, LANES), F32).at[0, :N_EXPERT_GROUPS].set(b_group).at[0, N_EXPERT_GROUPS:n_log].set(b_expert)
    wr_hi = wr.astype(BF16)
    wr = jnp.stack([wr_hi, (wr - wr_hi.astype(F32)).astype(BF16)])
    tri = (jnp.arange(tm)[:, None] > jnp.arange(tm)[None, :]).astype(BF16)
    return pl.pallas_call(
        functools.partial(_router_kernel, bpb=bpb, n_lat=n_lat),
        grid=(t // tm,),
        in_specs=[
            pl.BlockSpec((tm, d), lambda i: (i, 0)),
            pl.BlockSpec((1, d), lambda i: (0, 0)),
            pl.BlockSpec(mod.shape, lambda i: (0, 0)),
            pl.BlockSpec((2, d, LANES), lambda i: (0, 0, 0)),
            pl.BlockSpec((1, LANES), lambda i: (0, 0)),
            pl.BlockSpec((tm, tm), lambda i: (0, 0)),
        ],
        out_specs=[
            pl.BlockSpec((tm, d // 2), lambda i: (i, 0)),
            pl.BlockSpec((tm, LANES), lambda i: (i, 0)),
            pl.BlockSpec((1, LANES), lambda i: (0, 0)),
        ],
        out_shape=[
            jax.ShapeDtypeStruct((t, d // 2), U32),
            jax.ShapeDtypeStruct((t, LANES), F32),
            jax.ShapeDtypeStruct((1, LANES), F32),
        ],
        scratch_shapes=[pltpu.VMEM((1, LANES), F32)],
        compiler_params=_cparams("arbitrary"),
        name="moe_router",
    )(x, g.reshape(1, d), mod, wr, br, tri)


def _moe_ffn_kernel(be_ref, bv_ref, x_ref, wg_ref, wu_ref, wd_ref, o_ref, wgb, wub, wdb):
    i = pl.program_id(0)
    e = be_ref[i]
    e_prev = be_ref[jnp.maximum(i - 1, 0)]

    @pl.when((i == 0) | (e != e_prev))
    def _():
        wgb[...] = wg_ref[0, 0].astype(BF16)
        wub[...] = wu_ref[0, 0].astype(BF16)
        wdb[...] = wd_ref[0, 0].astype(BF16)

    n_valid = bv_ref[i]

    @pl.when(n_valid > 0)
    def _():
        half = x_ref.shape[1]
        x_a, x_b = (v.astype(BF16) for v in _unpack_bf16_pairs(x_ref[...]))

        def proj(w):
            return (jnp.dot(x_a, w[:half], preferred_element_type=F32)
                    + jnp.dot(x_b, w[half:], preferred_element_type=F32))

        gt, up = proj(wgb), proj(wub)
        a = gt * _sigmoid(gt) * up
        o_ref[...] = _pack_bf16_pairs(jnp.dot(a.astype(BF16), wdb[...], preferred_element_type=F32))

    @pl.when(n_valid <= 0)
    def _():
        o_ref[...] = jnp.zeros(o_ref.shape, U32)


def _moe_ffn(buf, blk_expert, blk_valid, w_gate, w_up, w_down, layer):
    p = buf.shape[0]
    d = w_gate.shape[-2]
    de = w_gate.shape[-1]
    n_blocks = p // MOE_BLOCK
    grid_spec = pltpu.PrefetchScalarGridSpec(
        num_scalar_prefetch=2,
        grid=(n_blocks,),
        in_specs=[
            pl.BlockSpec((MOE_BLOCK, d // 2), lambda i, be, bv: (i, 0)),
            pl.BlockSpec((1, 1, d, de), lambda i, be, bv: (layer, be[i], 0, 0)),
            pl.BlockSpec((1, 1, d, de), lambda i, be, bv: (layer, be[i], 0, 0)),
            pl.BlockSpec((1, 1, de, d), lambda i, be, bv: (layer, be[i], 0, 0)),
        ],
        out_specs=pl.BlockSpec((MOE_BLOCK, d // 2), lambda i, be, bv: (i, 0)),
        scratch_shapes=[pltpu.VMEM((d, de), BF16), pltpu.VMEM((d, de), BF16), pltpu.VMEM((de, d), BF16)],
    )
    return pl.pallas_call(
        _moe_ffn_kernel,
        grid_spec=grid_spec,
        out_shape=jax.ShapeDtypeStruct((p, d // 2), U32),
        compiler_params=_cparams("arbitrary"),
        name="moe_ffn",
    )(blk_expert, blk_valid, buf, w_gate, w_up, w_down)


DMA_ISSUE_UNROLL = 8


def _row_copy(src, src_row, dst, dst_row, sem):
    return pltpu.make_async_copy(src.at[pl.ds(src_row, 1)], dst.at[pl.ds(dst_row, 1)], sem)


def _moe_dispatch_kernel(pe_ref, dest_ref, h_ref, buf_ref, zero_ref, sems, zsem):
    tm = h_ref.shape[0] * SUBLANES
    n_blocks = buf_ref.shape[0] // MOE_BLOCK

    def zero_block(first_row):
        return pltpu.make_async_copy(zero_ref, buf_ref.at[pl.ds(pl.multiple_of(first_row, MOE_BLOCK), MOE_BLOCK)], zsem)

    @pl.when(pl.program_id(0) == 0)
    def _():
        zero_ref[...] = jnp.zeros(zero_ref.shape, zero_ref.dtype)

        def expert_has_rows(e):
            return pe_ref[e] > jnp.where(e > 0, pe_ref[jnp.maximum(e - 1, 0)], 0)

        def fill(start):
            def body(e, carry):
                @pl.when(expert_has_rows(e))
                def _():
                    cp = zero_block(pe_ref[e] - MOE_BLOCK)
                    cp.start() if start else cp.wait()
                return carry
            return body

        def tail(start):
            def body(b, carry):
                cp = zero_block(b * MOE_BLOCK)
                cp.start() if start else cp.wait()
                return carry
            return body

        first_unused = pe_ref[N_EXPERTS - 1] // MOE_BLOCK
        for start in (True, False):
            lax.fori_loop(0, N_EXPERTS, fill(start), 0)
            lax.fori_loop(first_unused, n_blocks, tail(start), 0)

    def issue(tile, carry):
        for k in range(SUBLANES):
            for j in range(2):
                slot = dest_ref[0, 0, 2 * SUBLANES * tile + 2 * k + j]
                pltpu.make_async_copy(h_ref.at[tile, pl.ds(k, 1)], buf_ref.at[pl.ds(slot, 1)], sems.at[j]).start(priority=j)
        return carry

    lax.fori_loop(0, tm // SUBLANES, issue, 0)
    for j in range(2):
        rows = buf_ref.at[pl.ds(0, tm)]
        pltpu.make_async_copy(rows, rows, sems.at[j]).wait()


def _moe_dispatch(h, dest, pad_end, n_slots, *, tm):
    t, d = h.shape
    grid_spec = pltpu.PrefetchScalarGridSpec(
        num_scalar_prefetch=1,
        grid=(t // tm,),
        in_specs=[
            pl.BlockSpec((1, 1, 2 * tm), lambda i, pe: (i, 0, 0), memory_space=pltpu.SMEM),
            pl.BlockSpec((tm // SUBLANES, SUBLANES, d), lambda i, pe: (i, 0, 0)),
        ],
        out_specs=pl.BlockSpec(memory_space=pl.ANY),
        scratch_shapes=[pltpu.VMEM((MOE_BLOCK, d), h.dtype), pltpu.SemaphoreType.DMA((2,)), pltpu.SemaphoreType.DMA(())],
    )
    return pl.pallas_call(
        _moe_dispatch_kernel,
        grid_spec=grid_spec,
        out_shape=jax.ShapeDtypeStruct((n_slots, d), h.dtype),
        compiler_params=_cparams("arbitrary"),
        name="moe_dispatch",
    )(pad_end.astype(jnp.int32), dest.reshape(t // tm, 1, 2 * tm), h.reshape(t // SUBLANES, SUBLANES, d))


def _moe_combine_kernel(dest_ref, x_ref, r_ref, mod_ref, yb_ref, o_ref, y0, y1, sems, *, bpb, n_lat):
    tm, d = x_ref.shape
    b = _batch_row(bpb, n_lat)
    ys = (y0, y1)

    def issue(tile, carry):
        for k in range(SUBLANES):
            for j in range(2):
                slot = dest_ref[0, 0, 2 * SUBLANES * tile + 2 * k + j]
                pltpu.make_async_copy(yb_ref.at[pl.ds(slot, 1)], ys[j].at[tile, pl.ds(k, 1)], sems.at[j]).start(priority=j)
        return carry

    lax.fori_loop(0, tm // SUBLANES, issue, 0)
    for j in range(2):
        rows = yb_ref.at[pl.ds(0, tm)]
        pltpu.make_async_copy(rows, rows, sems.at[j]).wait()
    r = r_ref[...]
    half = d // 2
    gated = [tuple(v * r[:, ROUTE_GATE + j:ROUTE_GATE + j + 1] for v in _unpack_bf16_pairs(ys[j][...].reshape(tm, half)))
             for j in range(2)]
    gate = _mod_row(mod_ref, b, 5, d)
    for c in range(2):
        cols = slice(c * half, (c + 1) * half)
        o_ref[:, cols] = x_ref[:, cols] + gate[:, cols] * (gated[0][c] + gated[1][c])


def _moe_combine(x, yb, dest, route, mod, *, tm, bpb, n_lat):
    t, d = x.shape
    row = pl.BlockSpec((tm, d), lambda i: (i, 0))
    return pl.pallas_call(
        functools.partial(_moe_combine_kernel, bpb=bpb, n_lat=n_lat),
        grid=(t // tm,),
        in_specs=[
            pl.BlockSpec((1, 1, 2 * tm), lambda i: (i, 0, 0), memory_space=pltpu.SMEM),
            row,
            pl.BlockSpec((tm, LANES), lambda i: (i, 0)),
            pl.BlockSpec(mod.shape, lambda i: (0, 0)),
            pl.BlockSpec(memory_space=pl.ANY),
        ],
        out_specs=row,
        out_shape=jax.ShapeDtypeStruct((t, d), F32),
        scratch_shapes=[pltpu.VMEM((tm // SUBLANES, SUBLANES, d // 2), U32) for _ in range(2)] + [pltpu.SemaphoreType.DMA((2,))],
        compiler_params=_cparams("arbitrary"),
        name="moe_combine",
    )(dest.reshape(t // tm, 1, 2 * tm), x, route, mod, yb)


def _hier_moe_residual(x, g, mod, w_group, b_group, w_expert, b_expert, w_gate, w_up, w_down, layer, *, tm, bpb, n_lat):
    t, d = x.shape
    h, route, counts = _router(x, g, mod, w_group, b_group, w_expert, b_expert, tm=tm, bpb=bpb, n_lat=n_lat)
    counts = counts[0, ROUTER_LOGIT0:ROUTER_LOGIT0 + N_EXPERTS].astype(jnp.int32)
    expert = route[:, ROUTE_E:ROUTE_E + 2].astype(jnp.int32)
    rank = route[:, ROUTE_RANK:ROUTE_RANK + 2].astype(jnp.int32)
    padded = (counts + MOE_BLOCK - 1) // MOE_BLOCK * MOE_BLOCK
    pad_end = jnp.cumsum(padded)
    pad_start = pad_end - padded
    dest = jnp.sum(jnp.where(expert[:, :, None] == jnp.arange(N_EXPERTS), pad_start, 0), axis=-1) + rank
    n_blocks = -(-(2 * t) // MOE_BLOCK) + N_EXPERTS
    blk_first = jnp.arange(n_blocks, dtype=jnp.int32) * MOE_BLOCK
    blk_expert = jnp.minimum(jnp.sum(blk_first[:, None] >= pad_end[None, :], axis=1), N_EXPERTS - 1).astype(jnp.int32)
    blk_rows = jnp.clip((pad_start + counts)[blk_expert] - blk_first, 0, MOE_BLOCK).astype(jnp.int32)
    buf = _moe_dispatch(h, dest, pad_end, n_blocks * MOE_BLOCK, tm=tm)
    yb = _moe_ffn(buf, blk_expert, blk_rows, w_gate, w_up, w_down, layer)
    return _moe_combine(x, yb, dest, route, mod, tm=tm, bpb=bpb, n_lat=n_lat)


def kernel(x, c, ctx, c_ctx, norm_g, w_mod, b_mod, conv_w_in, conv_b_in, conv_w_dw, conv_b_dw, conv_norm_g, conv_w_out, conv_b_out, fnet_w_out, fnet_b_out, attn_w_qkv, attn_q_norm_g, attn_k_norm_g, attn_w_out, moe_w_group, moe_b_group, moe_w_expert, moe_b_expert, moe_w_gate, moe_w_up, moe_w_down):
    bsz, seq, d = x.shape
    n_ctx = ctx.shape[1]
    depth = norm_g.shape[0]
    assert bsz <= MOD_ROWS // 2 and seq % DFT_NA == 0 and seq % GRID_W == 0
    t_lat, t_ctx = bsz * seq, bsz * n_ctx
    tm = min(512, seq)
    tmc = min(tm, n_ctx)
    assert seq % tm == 0 and n_ctx % tmc == 0 and tm % tmc == 0
    lat = dict(tm=tm, bpb=seq // tm, n_lat=t_lat // tm)
    cx = dict(tm=tmc, bpb=1, n_lat=0)
    tmb = tm if t_ctx % tm == 0 else tmc
    both = dict(tm=tmb, bpb=seq // tmb, n_lat=t_lat // tmb)
    last_reader = max([i for i in range(depth) if i % N_MIXERS == 2], default=-1)

    cvec = jnp.zeros((MOD_ROWS, d), F32).at[:bsz].set(c).at[MOD_ROWS // 2].set(c_ctx)
    mod_all = _modulation(cvec, w_mod, b_mod)
    cos, sin = _rope_tables(seq)

    xl = x.reshape(t_lat, d)
    xc = ctx.reshape(t_ctx, d)
    for i in range(depth):
        m, j = i % N_MIXERS, i // N_MIXERS
        ctx_on = i <= last_reader
        ctx_full = i < last_reader
        mod = mod_all[i]
        g1, g2 = norm_g[i, 0], norm_g[i, 1]
        if m == 0:
            cp = (conv_w_dw[j], conv_b_dw[j], conv_norm_g[j], conv_w_out[j], conv_b_out[j])
            v = _conv_in(xl, g1, mod, conv_w_in[j], conv_b_in[j], **lat)
            xl = _conv_out(v, xl, mod, *cp, **lat)
            if ctx_full:
                vc = _conv_in(xc, g1, mod, conv_w_in[j], conv_b_in[j], **cx)
                xc = _conv_out(vc, xc, mod, *cp, tm=n_ctx, bpb=1, n_lat=0)
        elif m == 1:
            hl = _prenorm(xl, g1, mod, dtype=F32, **lat)
            pq = _seq_dft_two_stage(hl, bsz, seq)
            xl = _fourier_out(pq, xl, mod, fnet_w_out[j], fnet_b_out[j], seq=seq, **lat)
            if ctx_full:
                hc = _prenorm(xc, g1, mod, dtype=BF16, **cx)
                pqc = _seq_dft_dense(hc, bsz, n_ctx)
                xc = _fourier_out(pqc, xc, mod, fnet_w_out[j], fnet_b_out[j], seq=n_ctx, tm=n_ctx, bpb=1, n_lat=0)
        else:
            gq, gk = attn_q_norm_g[j], attn_k_norm_g[j]
            qt, k, vv = _qkv(xl, g1, mod, attn_w_qkv[j], gq, gk, cos, sin, seq=seq, tq=min(ATTN_TQ, seq), **lat)
            ones = jnp.ones((tmc, HEAD_DIM), F32)
            _, kc, vc = _qkv(xc, g1, mod, attn_w_qkv[j], gq, gk, ones, jnp.zeros_like(ones),
                             seq=n_ctx, tq=min(ATTN_TQ, tmc), **cx)
            kv_dim = k.shape[1]
            k_all = jnp.concatenate([kc.reshape(bsz, n_ctx, kv_dim), k.reshape(bsz, seq, kv_dim)], axis=1)
            v_all = jnp.concatenate([vc.reshape(bsz, n_ctx, kv_dim), vv.reshape(bsz, seq, kv_dim)], axis=1)
            lk = seq + n_ctx
            tk = max(t for t in range(LANES, ATTN_MAX_TK + 1, LANES) if lk % t == 0)
            o = _attention(qt, k_all, v_all, tk=tk)
            xl = _proj_residual(o.reshape(t_lat, -1), xl, mod, attn_w_out[j], **lat)
            if ctx_full:
                raise NotImplementedError("context-stream attention output is only needed before the last reader layer")
        mp = (moe_w_group[i], moe_b_group[i], moe_w_expert[i], moe_b_expert[i], moe_w_gate, moe_w_up, moe_w_down, i)
        if ctx_full:
            xa = _hier_moe_residual(jnp.concatenate([xl, xc], axis=0), g2, mod, *mp, **both)
            xl, xc = xa[:t_lat], xa[t_lat:]
        else:
            xl = _hier_moe_residual(xl, g2, mod, *mp, **lat)
    return xl.reshape(bsz, seq, d)
```

```python
import functools
import math

import jax
import jax.numpy as jnp
from jax import lax
from jax.experimental import pallas as pl
from jax.experimental.pallas import tpu as pltpu

F32 = jnp.float32
BF16 = jnp.bfloat16
HIGHEST = lax.Precision.HIGHEST

EPS = 1e-6
GRID_W = 64
N_MIXERS = 3
CONV_WIDTH = 31
CONV_HALO = 16
CONV_ROW_CHUNK = 32
CONV_COL_CHUNK = 256
SUBLANES = 8
FNET_GROUPS = 4
HEAD_DIM = 128
N_KV_HEADS = 2
Q_PER_KV = 4
ROPE_THETA = 10000.0
N_EXPERT_GROUPS = 4
EXPERTS_PER_GROUP = 8
N_EXPERTS = N_EXPERT_GROUPS * EXPERTS_PER_GROUP
MOE_BLOCK = 256
LANES = 128
DFT_NA = 128
MOD_ROWS = 8
VMEM_LIMIT = 56 * 1024 * 1024


def _cparams(*sem):
    return pltpu.CompilerParams(dimension_semantics=sem, vmem_limit_bytes=VMEM_LIMIT)


def _rms(x, g):
    return x * lax.rsqrt(jnp.mean(x * x, axis=-1, keepdims=True) + EPS) * g


def _sigmoid(x):
    return 1.0 / (1.0 + jnp.exp(-x))


def _mod_row(mod_ref, b, k, d):
    return mod_ref[pl.ds(b, 1), pl.ds(k * d, d)]


def _norm_mod(x, g, mod_ref, b, k_shift, d):
    return _rms(x, g) * (1.0 + _mod_row(mod_ref, b, k_shift + 1, d)) + _mod_row(mod_ref, b, k_shift, d)


def _batch_row(bpb, n_lat_blocks):
    i = pl.program_id(0)
    return jnp.where(i < n_lat_blocks, i // bpb, MOD_ROWS // 2)


def _mod_kernel(c_ref, w_ref, b_ref, o_ref):
    c = c_ref[...]
    s = c * _sigmoid(c)
    o_ref[0] = jnp.dot(s, w_ref[0], precision=HIGHEST, preferred_element_type=F32) + b_ref[0]


def _modulation(cvec, w_mod, b_mod):
    depth, d, n = w_mod.shape
    tn = 1024
    return pl.pallas_call(
        _mod_kernel,
        grid=(depth, n // tn),
        in_specs=[
            pl.BlockSpec((MOD_ROWS, d), lambda l, j: (0, 0)),
            pl.BlockSpec((1, d, tn), lambda l, j: (l, 0, j)),
            pl.BlockSpec((1, 1, tn), lambda l, j: (l, 0, j)),
        ],
        out_specs=pl.BlockSpec((1, MOD_ROWS, tn), lambda l, j: (l, 0, j)),
        out_shape=jax.ShapeDtypeStruct((depth, MOD_ROWS, n), F32),
        compiler_params=_cparams("parallel", "parallel"),
        name="adaln_mod",
    )(cvec, w_mod, b_mod.reshape(depth, 1, n))


def _conv_in_kernel(x_ref, g_ref, mod_ref, w_ref, b_ref, v_ref, *, bpb, n_lat):
    d = x_ref.shape[1]
    b = _batch_row(bpb, n_lat)
    h = _norm_mod(x_ref[...], g_ref[...], mod_ref, b, 0, d)
    u = jnp.dot(h.astype(BF16), w_ref[...], preferred_element_type=F32) + b_ref[...]
    v_ref[...] = u[:, :d] * _sigmoid(u[:, d:])


def _conv_in(x, g, mod, w_in, b_in, *, tm, bpb, n_lat):
    t, d = x.shape
    return pl.pallas_call(
        functools.partial(_conv_in_kernel, bpb=bpb, n_lat=n_lat),
        grid=(t // tm,),
        in_specs=[
            pl.BlockSpec((tm, d), lambda i: (i, 0)),
            pl.BlockSpec((1, d), lambda i: (0, 0)),
            pl.BlockSpec(mod.shape, lambda i: (0, 0)),
            pl.BlockSpec((d, 2 * d), lambda i: (0, 0)),
            pl.BlockSpec((1, 2 * d), lambda i: (0, 0)),
        ],
        out_specs=pl.BlockSpec((tm, d), lambda i: (i, 0)),
        out_shape=jax.ShapeDtypeStruct((t, d), F32),
        compiler_params=_cparams("parallel"),
        name="conv_in",
    )(x, g.reshape(1, d), mod, w_in.astype(BF16), b_in.reshape(1, 2 * d))


def _conv_out_kernel(vp_ref, vc_ref, vn_ref, x_ref, wdw_ref, bdw_ref, gn_ref, wo_ref, bo_ref, mod_ref,
                     o_ref, buf, z, sh, *, bpb, n_lat, tm):
    d = x_ref.shape[1]
    i = pl.program_id(0)
    j = i % bpb
    b = _batch_row(bpb, n_lat)
    h = CONV_HALO
    buf[0:h, :] = jnp.where(j == 0, 0.0, vp_ref[...])
    buf[h:h + tm, :] = vc_ref[...]
    buf[h + tm:, :] = jnp.where(j == bpb - 1, 0.0, vn_ref[...])
    rc, cw = CONV_ROW_CHUNK, CONV_COL_CHUNK
    off = h - CONV_WIDTH // 2
    n_sh = sh.shape[1]

    def col_chunk(ci, carry):
        c0 = pl.multiple_of(ci * cw, cw)
        for s in range(SUBLANES):
            sh[s] = buf[s:s + n_sh, pl.ds(c0, cw)]
        for r in range(tm // rc):
            acc = jnp.zeros((rc, cw), F32)
            for t in range(CONV_WIDTH):
                q, s = divmod(off + t, SUBLANES)
                row0 = r * rc + q * SUBLANES
                acc = acc + sh[s, row0:row0 + rc, :] * wdw_ref[t:t + 1, pl.ds(c0, cw)]
            z[r * rc:(r + 1) * rc, pl.ds(c0, cw)] = acc
        return carry

    lax.fori_loop(0, d // cw, col_chunk, 0)
    zz = _rms(z[...] + bdw_ref[...], gn_ref[...])
    zz = zz * _sigmoid(zz)
    y = jnp.dot(zz.astype(BF16), wo_ref[...], preferred_element_type=F32) + bo_ref[...]
    o_ref[...] = x_ref[...] + _mod_row(mod_ref, b, 2, d) * y


def _conv_out(v, x, mod, w_dw, b_dw, g_norm, w_out, b_out, *, tm, bpb, n_lat):
    t, d = x.shape
    hb = tm // CONV_HALO
    n_halo = t // CONV_HALO
    wdw = jnp.zeros((32, d), F32).at[:CONV_WIDTH].set(w_dw)
    n_tap_tiles = (CONV_HALO - CONV_WIDTH // 2 + CONV_WIDTH - 1) // SUBLANES
    return pl.pallas_call(
        functools.partial(_conv_out_kernel, bpb=bpb, n_lat=n_lat, tm=tm),
        grid=(t // tm,),
        in_specs=[
            pl.BlockSpec((CONV_HALO, d), lambda i: (jnp.maximum(i * hb - 1, 0), 0)),
            pl.BlockSpec((tm, d), lambda i: (i, 0)),
            pl.BlockSpec((CONV_HALO, d), lambda i: (jnp.minimum((i + 1) * hb, n_halo - 1), 0)),
            pl.BlockSpec((tm, d), lambda i: (i, 0)),
            pl.BlockSpec((32, d), lambda i: (0, 0)),
            pl.BlockSpec((1, d), lambda i: (0, 0)),
            pl.BlockSpec((1, d), lambda i: (0, 0)),
            pl.BlockSpec((d, d), lambda i: (0, 0)),
            pl.BlockSpec((1, d), lambda i: (0, 0)),
            pl.BlockSpec(mod.shape, lambda i: (0, 0)),
        ],
        out_specs=pl.BlockSpec((tm, d), lambda i: (i, 0)),
        out_shape=jax.ShapeDtypeStruct((t, d), F32),
        scratch_shapes=[
            pltpu.VMEM((tm + 2 * CONV_HALO, d), F32),
            pltpu.VMEM((tm, d), F32),
            pltpu.VMEM((SUBLANES, tm + n_tap_tiles * SUBLANES, CONV_COL_CHUNK), F32),
        ],
        compiler_params=_cparams("parallel"),
        name="conv_out",
    )(v, v, v, x, wdw, b_dw.reshape(1, d), g_norm.reshape(1, d), w_out.astype(BF16), b_out.reshape(1, d), mod)


def _prenorm_kernel(x_ref, g_ref, mod_ref, h_ref, *, bpb, n_lat):
    d = x_ref.shape[1]
    b = _batch_row(bpb, n_lat)
    h_ref[...] = _norm_mod(x_ref[...], g_ref[...], mod_ref, b, 0, d).astype(h_ref.dtype)


def _prenorm(x, g, mod, *, tm, bpb, n_lat, dtype):
    t, d = x.shape
    return pl.pallas_call(
        functools.partial(_prenorm_kernel, bpb=bpb, n_lat=n_lat),
        grid=(t // tm,),
        in_specs=[
            pl.BlockSpec((tm, d), lambda i: (i, 0)),
            pl.BlockSpec((1, d), lambda i: (0, 0)),
            pl.BlockSpec(mod.shape, lambda i: (0, 0)),
        ],
        out_specs=pl.BlockSpec((tm, d), lambda i: (i, 0)),
        out_shape=jax.ShapeDtypeStruct((t, d), dtype),
        compiler_params=_cparams("parallel"),
        name="prenorm",
    )(x, g.reshape(1, d), mod)


def _dft1_kernel(h_ref, m_ref, o_ref, *, nb_chunk, nb, na):
    bc = pl.program_id(2)
    for bi in range(nb_chunk):
        b = bc * nb_chunk + bi
        xs = h_ref[0, pl.ds(b, na, stride=nb), :]
        t = jnp.dot(m_ref[bi], xs.astype(BF16), preferred_element_type=F32)
        o_ref[0, 0, bi] = t[:na].astype(BF16)
        o_ref[0, 1, bi] = t[na:].astype(BF16)


def _dft_tables(seq):
    na, nb = DFT_NA, seq // DFT_NA
    ka = jnp.arange(na, dtype=jnp.int32)
    a = jnp.arange(na, dtype=jnp.int32)
    b = jnp.arange(nb, dtype=jnp.int32)
    n = a[None, None, :] * nb + b[:, None, None]
    ang = ((ka[None, :, None] * n) % seq).astype(F32) * (2.0 * math.pi / seq)
    m1 = jnp.concatenate([jnp.cos(ang), -jnp.sin(ang)], axis=1).astype(BF16)
    kb = jnp.arange(nb, dtype=jnp.int32)
    ang2 = ((kb[:, None] * b[None, :]) % nb).astype(F32) * (2.0 * math.pi / nb)
    c2, s2 = jnp.cos(ang2), jnp.sin(ang2)
    m2 = jnp.concatenate([jnp.concatenate([c2, s2], axis=1),
                          jnp.concatenate([s2, -c2], axis=1)], axis=0).astype(BF16)
    return m1, m2


def _const_lhs_matmul_kernel(a_ref, x_ref, o_ref):
    o_ref[0] = jnp.dot(a_ref[...], x_ref[0], preferred_element_type=F32).astype(o_ref.dtype)


def _const_lhs_matmul(a, x, *, cn):
    m, k = a.shape
    bsz, _, n = x.shape
    return pl.pallas_call(
        _const_lhs_matmul_kernel,
        grid=(bsz, n // cn),
        in_specs=[
            pl.BlockSpec((m, k), lambda b, j: (0, 0)),
            pl.BlockSpec((1, k, cn), lambda b, j: (b, 0, j)),
        ],
        out_specs=pl.BlockSpec((1, m, cn), lambda b, j: (b, 0, j)),
        out_shape=jax.ShapeDtypeStruct((bsz, m, n), BF16),
        compiler_params=_cparams("parallel", "parallel"),
        name="const_lhs_matmul",
    )(a, x)


def _seq_dft_two_stage(h, bsz, seq):
    d = h.shape[1]
    na, nb = DFT_NA, seq // DFT_NA
    gw = LANES
    nb_chunk = min(nb, 8)
    m1, m2 = _dft_tables(seq)
    t1 = pl.pallas_call(
        functools.partial(_dft1_kernel, nb_chunk=nb_chunk, nb=nb, na=na),
        grid=(bsz, d // gw, nb // nb_chunk),
        in_specs=[
            pl.BlockSpec((1, seq, gw), lambda b, g, c: (b, 0, g)),
            pl.BlockSpec((nb_chunk, 2 * na, na), lambda b, g, c: (c, 0, 0)),
        ],
        out_specs=pl.BlockSpec((1, 2, nb_chunk, na, gw), lambda b, g, c: (b, 0, c, 0, g)),
        out_shape=jax.ShapeDtypeStruct((bsz, 2, nb, na, d), BF16),
        compiler_params=_cparams("parallel", "parallel", "arbitrary"),
        name="dft_stage1",
    )(h.reshape(bsz, seq, d), m1)
    pq = _const_lhs_matmul(m2, t1.reshape(bsz, 2 * nb, na * d), cn=min(na * d, 8192))
    return pq.reshape(bsz, 2, seq, d)


def _seq_dft_dense(h, bsz, seq):
    d = h.shape[1]
    k = jnp.arange(seq, dtype=jnp.int32)
    ang = ((k[:, None] * k[None, :]) % seq).astype(F32) * (2.0 * math.pi / seq)
    a = jnp.concatenate([jnp.cos(ang), jnp.sin(ang)], axis=0).astype(BF16)
    pq = _const_lhs_matmul(a, h.reshape(bsz, seq, d), cn=d)
    return pq.reshape(bsz, 2, seq, d)


def _fourier_out_kernel(p_ref, q_ref, x_ref, cm_ref, wo_ref, bo_ref, mod_ref, o_ref, *, bpb, n_lat):
    d = x_ref.shape[1]
    gw = d // FNET_GROUPS
    b = _batch_row(bpb, n_lat)
    p, q = p_ref[0, 0], q_ref[0, 0]
    f = [jnp.dot(p[:, g * gw:(g + 1) * gw], cm_ref[:gw], preferred_element_type=F32)
         + jnp.dot(q[:, g * gw:(g + 1) * gw], cm_ref[gw:], preferred_element_type=F32)
         for g in range(FNET_GROUPS)]
    f = jnp.concatenate(f, axis=1).astype(BF16)
    y = jnp.dot(f, wo_ref[...], preferred_element_type=F32) + bo_ref[...]
    o_ref[...] = x_ref[...] + _mod_row(mod_ref, b, 2, d) * y


def _fourier_out(pq, x, mod, w_out, b_out, *, seq, tm, bpb, n_lat):
    t, d = x.shape
    gw = d // FNET_GROUPS
    k = jnp.arange(gw, dtype=jnp.int32)
    ang = ((k[:, None] * k[None, :]) % gw).astype(F32) * (2.0 * math.pi / gw)
    scale = 1.0 / math.sqrt(seq * gw)
    cm = (jnp.concatenate([jnp.cos(ang), -jnp.sin(ang)], axis=0) * scale).astype(BF16)
    return pl.pallas_call(
        functools.partial(_fourier_out_kernel, bpb=bpb, n_lat=n_lat),
        grid=(t // tm,),
        in_specs=[
            pl.BlockSpec((1, 1, tm, d), lambda i: (i // bpb, 0, i % bpb, 0)),
            pl.BlockSpec((1, 1, tm, d), lambda i: (i // bpb, 1, i % bpb, 0)),
            pl.BlockSpec((tm, d), lambda i: (i, 0)),
            pl.BlockSpec((2 * gw, gw), lambda i: (0, 0)),
            pl.BlockSpec((d, d), lambda i: (0, 0)),
            pl.BlockSpec((1, d), lambda i: (0, 0)),
            pl.BlockSpec(mod.shape, lambda i: (0, 0)),
        ],
        out_specs=pl.BlockSpec((tm, d), lambda i: (i, 0)),
        out_shape=jax.ShapeDtypeStruct((t, d), F32),
        compiler_params=_cparams("parallel"),
        name="fourier_out",
    )(pq, pq, x, cm, w_out.astype(BF16), b_out.reshape(1, d), mod)


def _head_perm():
    return jnp.concatenate([jnp.arange(0, HEAD_DIM, 2), jnp.arange(1, HEAD_DIM, 2)])


def _rope_tables(seq):
    rows = seq // GRID_W
    row = jnp.repeat(jnp.arange(rows, dtype=F32), GRID_W)
    col = jnp.tile(jnp.arange(GRID_W, dtype=F32), rows)
    n_pairs_axis = HEAD_DIM // 4
    inv = ROPE_THETA ** (-jnp.arange(n_pairs_axis, dtype=F32) / n_pairs_axis)
    ang = jnp.concatenate([row[:, None] * inv, col[:, None] * inv], axis=-1)
    cos, sin = jnp.cos(ang), jnp.sin(ang)
    return jnp.concatenate([cos, cos], axis=-1), jnp.concatenate([-sin, sin], axis=-1)


def _qkv_kernel(x_ref, g_ref, mod_ref, w_ref, gq_ref, gk_ref, cos_ref, sin_ref, qt_ref, k_ref, v_ref, *, bpb, n_lat):
    d = x_ref.shape[1]
    b = _batch_row(bpb, n_lat)
    h = _norm_mod(x_ref[...], g_ref[...], mod_ref, b, 0, d)
    u = jnp.dot(h.astype(BF16), w_ref[...], preferred_element_type=F32)
    cos, sin = cos_ref[...], sin_ref[...]
    kv_dim = k_ref.shape[1]
    q_dim = Q_PER_KV * kv_dim
    tq = qt_ref.shape[4] // Q_PER_KV

    def norm_rope(xh, gain):
        xh = _rms(xh, gain)
        return xh * cos + pltpu.roll(xh, HEAD_DIM // 2, 1) * sin

    q_scale = HEAD_DIM ** -0.5 * math.log2(math.e)
    for hh in range(q_dim // HEAD_DIM):
        sl = slice(hh * HEAD_DIM, (hh + 1) * HEAD_DIM)
        kvh, g = divmod(hh, Q_PER_KV)
        q_t = (norm_rope(u[:, sl], gq_ref[...]) * q_scale).T.astype(BF16)
        for qb in range(qt_ref.shape[2]):
            qt_ref[0, kvh, qb, :, g * tq:(g + 1) * tq] = q_t[:, qb * tq:(qb + 1) * tq]
    for hh in range(kv_dim // HEAD_DIM):
        sl = slice(hh * HEAD_DIM, (hh + 1) * HEAD_DIM)
        k_ref[:, sl] = norm_rope(u[:, q_dim + hh * HEAD_DIM:q_dim + (hh + 1) * HEAD_DIM], gk_ref[...]).astype(BF16)
    v_ref[...] = u[:, q_dim + kv_dim:].astype(BF16)


def _qkv(x, g, mod, w_qkv, gq, gk, cos, sin, *, seq, tq, tm, bpb, n_lat):
    t, d = x.shape
    assert tm % tq == 0 and seq % tm == 0
    q_dim = N_KV_HEADS * Q_PER_KV * HEAD_DIM
    kv_dim = N_KV_HEADS * HEAD_DIM
    perm = _head_perm()
    n_heads_qk = (q_dim + kv_dim) // HEAD_DIM
    cols = (jnp.arange(n_heads_qk)[:, None] * HEAD_DIM + perm[None, :]).reshape(-1)
    cols = jnp.concatenate([cols, jnp.arange(q_dim + kv_dim, q_dim + 2 * kv_dim)])
    w = w_qkv[:, cols].astype(BF16)
    n_pos = cos.shape[0] // tm
    return pl.pallas_call(
        functools.partial(_qkv_kernel, bpb=bpb, n_lat=n_lat),
        grid=(t // tm,),
        in_specs=[
            pl.BlockSpec((tm, d), lambda i: (i, 0)),
            pl.BlockSpec((1, d), lambda i: (0, 0)),
            pl.BlockSpec(mod.shape, lambda i: (0, 0)),
            pl.BlockSpec((d, q_dim + 2 * kv_dim), lambda i: (0, 0)),
            pl.BlockSpec((1, HEAD_DIM), lambda i: (0, 0)),
            pl.BlockSpec((1, HEAD_DIM), lambda i: (0, 0)),
            pl.BlockSpec((tm, HEAD_DIM), lambda i: (i % n_pos, 0)),
            pl.BlockSpec((tm, HEAD_DIM), lambda i: (i % n_pos, 0)),
        ],
        out_specs=[
            pl.BlockSpec((1, N_KV_HEADS, tm // tq, HEAD_DIM, Q_PER_KV * tq),
                         lambda i: (i // (seq // tm), 0, i % (seq // tm), 0, 0)),
            pl.BlockSpec((tm, kv_dim), lambda i: (i, 0)),
            pl.BlockSpec((tm, kv_dim), lambda i: (i, 0)),
        ],
        out_shape=[
            jax.ShapeDtypeStruct((t // seq, N_KV_HEADS, seq // tq, HEAD_DIM, Q_PER_KV * tq), BF16),
            jax.ShapeDtypeStruct((t, kv_dim), BF16),
            jax.ShapeDtypeStruct((t, kv_dim), BF16),
        ],
        compiler_params=_cparams("parallel"),
        name="qkv_proj",
    )(x, g.reshape(1, d), mod, w, gq[perm].reshape(1, HEAD_DIM), gk[perm].reshape(1, HEAD_DIM), cos, sin)


ATTN_TQ = 512
ATTN_ROW_CHUNK = 16
ATTN_MAX_TK = 1408


def _attn_kernel(qt_ref, k_ref, vt_ref, o_ref, *scratch, tq, tk, nk):
    s_ref, p_ref, acc_ref = scratch
    nq = Q_PER_KV * tq
    acc_ref[...] = jnp.zeros(acc_ref.shape, F32)
    rc = ATTN_ROW_CHUNK

    def kv_chunk(j, carry):
        m_prev, l_prev = carry
        kk = k_ref[0, pl.ds(pl.multiple_of(j * tk, tk), tk), :]
        s_ref[...] = jnp.dot(kk, qt_ref[0, 0, 0], preferred_element_type=F32)
        mx = s_ref[0:SUBLANES, :]
        for r in range(1, tk // SUBLANES):
            mx = jnp.maximum(mx, s_ref[r * SUBLANES:(r + 1) * SUBLANES, :])
        m_new = jnp.maximum(m_prev, jnp.max(mx, axis=0, keepdims=True))
        psum = jnp.zeros((SUBLANES, nq), F32)
        for r in range(tk // rc):
            p = jnp.exp2(s_ref[r * rc:(r + 1) * rc, :] - m_new)
            p_ref[r * rc:(r + 1) * rc, :] = p.astype(BF16)
            for h in range(rc // SUBLANES):
                psum = psum + p[h * SUBLANES:(h + 1) * SUBLANES]
        alpha = jnp.exp2(m_prev - m_new)
        acc_ref[...] = alpha * acc_ref[...] + jnp.dot(vt_ref[0, 0, j], p_ref[...], preferred_element_type=F32)
        return m_new, alpha * l_prev + jnp.sum(psum, axis=0, keepdims=True)

    init = (jnp.full((1, nq), -jnp.inf, F32), jnp.zeros((1, nq), F32))
    _, l_all = lax.fori_loop(0, nk, kv_chunk, init)
    out = (acc_ref[...] / l_all).T
    for g in range(Q_PER_KV):
        o_ref[0, :, g * HEAD_DIM:(g + 1) * HEAD_DIM] = out[g * tq:(g + 1) * tq].astype(o_ref.dtype)


def _attention(qt, k_all, v_all, *, tk):
    bsz, _, n_qblk, _, nq = qt.shape
    tq = nq // Q_PER_KV
    seq = n_qblk * tq
    lk = k_all.shape[1]
    gq = Q_PER_KV * HEAD_DIM
    q_dim = N_KV_HEADS * gq
    nk = lk // tk
    assert tk % ATTN_ROW_CHUNK == 0 and nq % LANES == 0
    vt = v_all.reshape(bsz, nk, tk, N_KV_HEADS, HEAD_DIM).transpose(0, 3, 1, 4, 2)
    return pl.pallas_call(
        functools.partial(_attn_kernel, tq=tq, tk=tk, nk=nk),
        grid=(bsz, N_KV_HEADS, seq // tq),
        in_specs=[
            pl.BlockSpec((1, 1, 1, HEAD_DIM, nq), lambda b, h, i: (b, h, i, 0, 0)),
            pl.BlockSpec((1, lk, HEAD_DIM), lambda b, h, i: (b, 0, h)),
            pl.BlockSpec((1, 1, nk, HEAD_DIM, tk), lambda b, h, i: (b, h, 0, 0, 0)),
        ],
        out_specs=pl.BlockSpec((1, tq, gq), lambda b, h, i: (b, i, h)),
        out_shape=jax.ShapeDtypeStruct((bsz, seq, q_dim), BF16),
        scratch_shapes=[pltpu.VMEM((tk, nq), F32), pltpu.VMEM((tk, nq), BF16), pltpu.VMEM((HEAD_DIM, nq), F32)],
        compiler_params=_cparams("parallel", "parallel", "parallel"),
        name="flash_attention",
    )(qt, k_all, vt)


def _proj_residual_kernel(a_ref, x_ref, w_ref, mod_ref, o_ref, *, bpb, n_lat):
    d = x_ref.shape[1]
    b = _batch_row(bpb, n_lat)
    y = jnp.dot(a_ref[...], w_ref[...], preferred_element_type=F32)
    o_ref[...] = x_ref[...] + _mod_row(mod_ref, b, 2, d) * y


def _proj_residual(a, x, mod, w, *, tm, bpb, n_lat):
    t, d = x.shape
    ka = a.shape[1]
    return pl.pallas_call(
        functools.partial(_proj_residual_kernel, bpb=bpb, n_lat=n_lat),
        grid=(t // tm,),
        in_specs=[
            pl.BlockSpec((tm, ka), lambda i: (i, 0)),
            pl.BlockSpec((tm, d), lambda i: (i, 0)),
            pl.BlockSpec((ka, d), lambda i: (0, 0)),
            pl.BlockSpec(mod.shape, lambda i: (0, 0)),
        ],
        out_specs=pl.BlockSpec((tm, d), lambda i: (i, 0)),
        out_shape=jax.ShapeDtypeStruct((t, d), F32),
        compiler_params=_cparams("parallel"),
        name="proj_residual",
    )(a, x, w.astype(BF16), mod)


ROUTE_E, ROUTE_GATE, ROUTE_RANK = 0, 2, 4
ROUTER_LOGIT0 = N_EXPERT_GROUPS


U32 = jnp.uint32
BF16_HIGH_BITS = 0xFFFF0000


def _pack_bf16_pairs(x):
    half = x.shape[1] // 2
    hi = lax.bitcast_convert_type(x[:, :half].astype(BF16).astype(F32), U32)
    lo = lax.bitcast_convert_type(x[:, half:].astype(BF16).astype(F32), U32)
    return (hi & U32(BF16_HIGH_BITS)) | (lo >> 16)


def _unpack_bf16_pairs(w):
    return (lax.bitcast_convert_type(w & U32(BF16_HIGH_BITS), F32), lax.bitcast_convert_type(w << 16, F32))


def _router_kernel(x_ref, g_ref, mod_ref, wr_ref, br_ref, tri_ref, h_ref, r_ref, cnt_ref, carry, *, bpb, n_lat):
    d = x_ref.shape[1]
    tm = x_ref.shape[0]
    i = pl.program_id(0)
    b = _batch_row(bpb, n_lat)

    @pl.when(i == 0)
    def _():
        carry[...] = jnp.zeros(carry.shape, F32)

    h = _norm_mod(x_ref[...], g_ref[...], mod_ref, b, 3, d)
    h_ref[...] = _pack_bf16_pairs(h)
    h_hi = h.astype(BF16)
    h_lo = (h - h_hi.astype(F32)).astype(BF16)
    logits = (jnp.dot(h_hi, wr_ref[0], preferred_element_type=F32) + jnp.dot(h_lo, wr_ref[0], preferred_element_type=F32)
              + jnp.dot(h_hi, wr_ref[1], preferred_element_type=F32) + br_ref[...])
    lane = lax.broadcasted_iota(jnp.int32, (tm, LANES), 1).astype(F32)
    big = float(LANES)
    neg = -jnp.inf
    gmask = lane < N_EXPERT_GROUPS
    gl = jnp.where(gmask, logits, neg)
    ge = jnp.exp(gl - jnp.max(gl, axis=-1, keepdims=True))
    gp = ge / jnp.sum(ge, axis=-1, keepdims=True)
    g_top = jnp.max(gp, axis=-1, keepdims=True)
    g_idx = jnp.min(jnp.where(gmask & (gp == g_top), lane, big), axis=-1, keepdims=True)
    lo = ROUTER_LOGIT0 + g_idx * EXPERTS_PER_GROUP
    emask = (lane >= lo) & (lane < lo + EXPERTS_PER_GROUP)
    el = jnp.where(emask, logits, neg)
    ee = jnp.exp(el - jnp.max(el, axis=-1, keepdims=True))
    ep = ee / jnp.sum(ee, axis=-1, keepdims=True)
    p1 = jnp.max(ep, axis=-1, keepdims=True)
    i1 = jnp.min(jnp.where(emask & (ep == p1), lane, big), axis=-1, keepdims=True)
    rest = emask & (lane != i1)
    p2 = jnp.max(jnp.where(rest, ep, -1.0), axis=-1, keepdims=True)
    i2 = jnp.min(jnp.where(rest & (ep == p2), lane, big), axis=-1, keepdims=True)
    denom = p1 + p2
    gate1 = g_top * p1 / denom
    gate2 = g_top * p2 / denom
    hit1 = lane == i1
    hit2 = lane == i2
    onehot = jnp.where(hit1 | hit2, 1.0, 0.0)
    cum = jnp.dot(tri_ref[...], onehot.astype(BF16), preferred_element_type=F32) + carry[...]
    rank1 = jnp.sum(jnp.where(hit1, cum, 0.0), axis=-1, keepdims=True)
    rank2 = jnp.sum(jnp.where(hit2, cum, 0.0), axis=-1, keepdims=True)
    carry[...] = carry[...] + jnp.sum(onehot, axis=0, keepdims=True)
    cnt_ref[...] = carry[...]
    rec = jnp.zeros((tm, LANES), F32)
    for ln, val in ((ROUTE_E, i1 - ROUTER_LOGIT0), (ROUTE_E + 1, i2 - ROUTER_LOGIT0), (ROUTE_GATE, gate1),
                    (ROUTE_GATE + 1, gate2), (ROUTE_RANK, rank1), (ROUTE_RANK + 1, rank2)):
        rec = jnp.where(lane == float(ln), val, rec)
    r_ref[...] = rec


def _router(x, g, mod, w_group, b_group, w_expert, b_expert, *, tm, bpb, n_lat):
    t, d = x.shape
    n_log = N_EXPERT_GROUPS + N_EXPERTS
    wr = jnp.zeros((d, LANES), F32).at[:, :N_EXPERT_GROUPS].set(w_group).at[:, N_EXPERT_GROUPS:n_log].set(w_expert)
    br = jnp.zeros((1, LANES), F32).at[0, :N_EXPERT_GROUPS].set(b_group).at[0, N_EXPERT_GROUPS:n_log].set(b_expert)
    wr_hi = wr.astype(BF16)
    wr = jnp.stack([wr_hi, (wr - wr_hi.astype(F32)).astype(BF16)])
    tri = (jnp.arange(tm)[:, None] > jnp.arange(tm)[None, :]).astype(BF16)
    return pl.pallas_call(
        functools.partial(_router_kernel, bpb=bpb, n_lat=n_lat),
        grid=(t // tm,),
        in_specs=[
            pl.BlockSpec((tm, d), lambda i: (i, 0)),
            pl.BlockSpec((1, d), lambda i: (0, 0)),
            pl.BlockSpec(mod.shape, lambda i: (0, 0)),
            pl.BlockSpec((2, d, LANES), lambda i: (0, 0, 0)),
            pl.BlockSpec((1, LANES), lambda i: (0, 0)),
            pl.BlockSpec((tm, tm), lambda i: (0, 0)),
        ],
        out_specs=[
            pl.BlockSpec((tm, d // 2), lambda i: (i, 0)),
            pl.BlockSpec((tm, LANES), lambda i: (i, 0)),
            pl.BlockSpec((1, LANES), lambda i: (0, 0)),
        ],
        out_shape=[
            jax.ShapeDtypeStruct((t, d // 2), U32),
            jax.ShapeDtypeStruct((t, LANES), F32),
            jax.ShapeDtypeStruct((1, LANES), F32),
        ],
        scratch_shapes=[pltpu.VMEM((1, LANES), F32)],
        compiler_params=_cparams("arbitrary"),
        name="moe_router",
    )(x, g.reshape(1, d), mod, wr, br, tri)


def _moe_ffn_kernel(be_ref, bv_ref, x_ref, wg_ref, wu_ref, wd_ref, o_ref, wgb, wub, wdb):
    i = pl.program_id(0)
    e = be_ref[i]
    e_prev = be_ref[jnp.maximum(i - 1, 0)]

    @pl.when((i == 0) | (e != e_prev))
    def _():
        wgb[...] = wg_ref[0, 0].astype(BF16)
        wub[...] = wu_ref[0, 0].astype(BF16)
        wdb[...] = wd_ref[0, 0].astype(BF16)

    n_valid = bv_ref[i]

    @pl.when(n_valid > 0)
    def _():
        half = x_ref.shape[1]
        x_a, x_b = (v.astype(BF16) for v in _unpack_bf16_pairs(x_ref[...]))

        def proj(w):
            return (jnp.dot(x_a, w[:half], preferred_element_type=F32)
                    + jnp.dot(x_b, w[half:], preferred_element_type=F32))

        gt, up = proj(wgb), proj(wub)
        a = gt * _sigmoid(gt) * up
        o_ref[...] = _pack_bf16_pairs(jnp.dot(a.astype(BF16), wdb[...], preferred_element_type=F32))

    @pl.when(n_valid <= 0)
    def _():
        o_ref[...] = jnp.zeros(o_ref.shape, U32)


def _moe_ffn(buf, blk_expert, blk_valid, w_gate, w_up, w_down, layer):
    p = buf.shape[0]
    d = w_gate.shape[-2]
    de = w_gate.shape[-1]
    n_blocks = p // MOE_BLOCK
    grid_spec = pltpu.PrefetchScalarGridSpec(
        num_scalar_prefetch=2,
        grid=(n_blocks,),
        in_specs=[
            pl.BlockSpec((MOE_BLOCK, d // 2), lambda i, be, bv: (i, 0)),
            pl.BlockSpec((1, 1, d, de), lambda i, be, bv: (layer, be[i], 0, 0)),
            pl.BlockSpec((1, 1, d, de), lambda i, be, bv: (layer, be[i], 0, 0)),
            pl.BlockSpec((1, 1, de, d), lambda i, be, bv: (layer, be[i], 0, 0)),
        ],
        out_specs=pl.BlockSpec((MOE_BLOCK, d // 2), lambda i, be, bv: (i, 0)),
        scratch_shapes=[pltpu.VMEM((d, de), BF16), pltpu.VMEM((d, de), BF16), pltpu.VMEM((de, d), BF16)],
    )
    return pl.pallas_call(
        _moe_ffn_kernel,
        grid_spec=grid_spec,
        out_shape=jax.ShapeDtypeStruct((p, d // 2), U32),
        compiler_params=_cparams("arbitrary"),
        name="moe_ffn",
    )(blk_expert, blk_valid, buf, w_gate, w_up, w_down)


DMA_ISSUE_UNROLL = 8


def _row_copy(src, src_row, dst, dst_row, sem):
    return pltpu.make_async_copy(src.at[pl.ds(src_row, 1)], dst.at[pl.ds(dst_row, 1)], sem)


def _moe_dispatch_kernel(pe_ref, dest_ref, h_ref, buf_ref, zero_ref, sems, zsem):
    tm = h_ref.shape[0] * SUBLANES
    n_blocks = buf_ref.shape[0] // MOE_BLOCK

    def zero_block(first_row):
        return pltpu.make_async_copy(zero_ref, buf_ref.at[pl.ds(pl.multiple_of(first_row, MOE_BLOCK), MOE_BLOCK)], zsem)

    @pl.when(pl.program_id(0) == 0)
    def _():
        zero_ref[...] = jnp.zeros(zero_ref.shape, zero_ref.dtype)

        def expert_has_rows(e):
            return pe_ref[e] > jnp.where(e > 0, pe_ref[jnp.maximum(e - 1, 0)], 0)

        def fill(start):
            def body(e, carry):
                @pl.when(expert_has_rows(e))
                def _():
                    cp = zero_block(pe_ref[e] - MOE_BLOCK)
                    cp.start() if start else cp.wait()
                return carry
            return body

        def tail(start):
            def body(b, carry):
                cp = zero_block(b * MOE_BLOCK)
                cp.start() if start else cp.wait()
                return carry
            return body

        first_unused = pe_ref[N_EXPERTS - 1] // MOE_BLOCK
        for start in (True, False):
            lax.fori_loop(0, N_EXPERTS, fill(start), 0)
            lax.fori_loop(first_unused, n_blocks, tail(start), 0)

    def issue(tile, carry):
        for k in range(SUBLANES):
            for j in range(2):
                slot = dest_ref[0, 0, 2 * SUBLANES * tile + 2 * k + j]
                pltpu.make_async_copy(h_ref.at[tile, pl.ds(k, 1)], buf_ref.at[pl.ds(slot, 1)], sems.at[j]).start(priority=j)
        return carry

    lax.fori_loop(0, tm // SUBLANES, issue, 0)
    for j in range(2):
        rows = buf_ref.at[pl.ds(0, tm)]
        pltpu.make_async_copy(rows, rows, sems.at[j]).wait()


def _moe_dispatch(h, dest, pad_end, n_slots, *, tm):
    t, d = h.shape
    grid_spec = pltpu.PrefetchScalarGridSpec(
        num_scalar_prefetch=1,
        grid=(t // tm,),
        in_specs=[
            pl.BlockSpec((1, 1, 2 * tm), lambda i, pe: (i, 0, 0), memory_space=pltpu.SMEM),
            pl.BlockSpec((tm // SUBLANES, SUBLANES, d), lambda i, pe: (i, 0, 0)),
        ],
        out_specs=pl.BlockSpec(memory_space=pl.ANY),
        scratch_shapes=[pltpu.VMEM((MOE_BLOCK, d), h.dtype), pltpu.SemaphoreType.DMA((2,)), pltpu.SemaphoreType.DMA(())],
    )
    return pl.pallas_call(
        _moe_dispatch_kernel,
        grid_spec=grid_spec,
        out_shape=jax.ShapeDtypeStruct((n_slots, d), h.dtype),
        compiler_params=_cparams("arbitrary"),
        name="moe_dispatch",
    )(pad_end.astype(jnp.int32), dest.reshape(t // tm, 1, 2 * tm), h.reshape(t // SUBLANES, SUBLANES, d))


def _moe_combine_kernel(dest_ref, dest_next_ref, x_ref, r_ref, mod_ref, yb_ref, o_ref, *scratch, bpb, n_lat):
    tm, d = x_ref.shape
    ys, sems = (scratch[0:2], scratch[2:4]), scratch[4]
    i = pl.program_id(0)
    n_steps = pl.num_programs(0)
    b = _batch_row(bpb, n_lat)

    def issue_block(idx_ref, buf):
        def issue(tile, carry):
            for k in range(SUBLANES):
                for j in range(2):
                    slot = idx_ref[0, 0, 2 * SUBLANES * tile + 2 * k + j]
                    pltpu.make_async_copy(yb_ref.at[pl.ds(slot, 1)], ys[buf][j].at[tile, pl.ds(k, 1)],
                                          sems.at[buf, j]).start(priority=j)
            return carry

        lax.fori_loop(0, tm // SUBLANES, issue, 0)

    def step(buf):
        @pl.when(i == 0)
        def _():
            issue_block(dest_ref, buf)

        @pl.when(i + 1 < n_steps)
        def _():
            issue_block(dest_next_ref, 1 - buf)

        for j in range(2):
            rows = yb_ref.at[pl.ds(0, tm)]
            pltpu.make_async_copy(rows, rows, sems.at[buf, j]).wait()
        r = r_ref[...]
        half = d // 2
        gated = [tuple(v * r[:, ROUTE_GATE + j:ROUTE_GATE + j + 1]
                       for v in _unpack_bf16_pairs(ys[buf][j][...].reshape(tm, half))) for j in range(2)]
        gate = _mod_row(mod_ref, b, 5, d)
        for c in range(2):
            cols = slice(c * half, (c + 1) * half)
            o_ref[:, cols] = x_ref[:, cols] + gate[:, cols] * (gated[0][c] + gated[1][c])

    for buf in range(2):
        pl.when(i % 2 == buf)(functools.partial(step, buf))


def _moe_combine(x, yb, dest, route, mod, *, tm, bpb, n_lat):
    t, d = x.shape
    n_steps = t // tm
    dest = dest.reshape(n_steps, 1, 2 * tm)
    row = pl.BlockSpec((tm, d), lambda i: (i, 0))
    return pl.pallas_call(
        functools.partial(_moe_combine_kernel, bpb=bpb, n_lat=n_lat),
        grid=(n_steps,),
        in_specs=[
            pl.BlockSpec((1, 1, 2 * tm), lambda i: (i, 0, 0), memory_space=pltpu.SMEM),
            pl.BlockSpec((1, 1, 2 * tm), lambda i: (jnp.minimum(i + 1, n_steps - 1), 0, 0), memory_space=pltpu.SMEM),
            row,
            pl.BlockSpec((tm, LANES), lambda i: (i, 0)),
            pl.BlockSpec(mod.shape, lambda i: (0, 0)),
            pl.BlockSpec(memory_space=pl.ANY),
        ],
        out_specs=row,
        out_shape=jax.ShapeDtypeStruct((t, d), F32),
        scratch_shapes=[pltpu.VMEM((tm // SUBLANES, SUBLANES, d // 2), U32) for _ in range(4)] + [pltpu.SemaphoreType.DMA((2, 2))],
        compiler_params=_cparams("arbitrary"),
        name="moe_combine",
    )(dest, dest, x, route, mod, yb)


def _hier_moe_residual(x, g, mod, w_group, b_group, w_expert, b_expert, w_gate, w_up, w_down, layer, *, tm, bpb, n_lat):
    t, d = x.shape
    h, route, counts = _router(x, g, mod, w_group, b_group, w_expert, b_expert, tm=tm, bpb=bpb, n_lat=n_lat)
    counts = counts[0, ROUTER_LOGIT0:ROUTER_LOGIT0 + N_EXPERTS].astype(jnp.int32)
    expert = route[:, ROUTE_E:ROUTE_E + 2].astype(jnp.int32)
    rank = route[:, ROUTE_RANK:ROUTE_RANK + 2].astype(jnp.int32)
    padded = (counts + MOE_BLOCK - 1) // MOE_BLOCK * MOE_BLOCK
    pad_end = jnp.cumsum(padded)
    pad_start = pad_end - padded
    dest = jnp.sum(jnp.where(expert[:, :, None] == jnp.arange(N_EXPERTS), pad_start, 0), axis=-1) + rank
    n_blocks = -(-(2 * t) // MOE_BLOCK) + N_EXPERTS
    blk_first = jnp.arange(n_blocks, dtype=jnp.int32) * MOE_BLOCK
    blk_expert = jnp.minimum(jnp.sum(blk_first[:, None] >= pad_end[None, :], axis=1), N_EXPERTS - 1).astype(jnp.int32)
    blk_rows = jnp.clip((pad_start + counts)[blk_expert] - blk_first, 0, MOE_BLOCK).astype(jnp.int32)
    buf = _moe_dispatch(h, dest, pad_end, n_blocks * MOE_BLOCK, tm=tm)
    yb = _moe_ffn(buf, blk_expert, blk_rows, w_gate, w_up, w_down, layer)
    return _moe_combine(x, yb, dest, route, mod, tm=tm, bpb=bpb, n_lat=n_lat)


def kernel(x, c, ctx, c_ctx, norm_g, w_mod, b_mod, conv_w_in, conv_b_in, conv_w_dw, conv_b_dw, conv_norm_g, conv_w_out, conv_b_out, fnet_w_out, fnet_b_out, attn_w_qkv, attn_q_norm_g, attn_k_norm_g, attn_w_out, moe_w_group, moe_b_group, moe_w_expert, moe_b_expert, moe_w_gate, moe_w_up, moe_w_down):
    bsz, seq, d = x.shape
    n_ctx = ctx.shape[1]
    depth = norm_g.shape[0]
    assert bsz <= MOD_ROWS // 2 and seq % DFT_NA == 0 and seq % GRID_W == 0
    t_lat, t_ctx = bsz * seq, bsz * n_ctx
    tm = min(512, seq)
    tmc = min(tm, n_ctx)
    assert seq % tm == 0 and n_ctx % tmc == 0 and tm % tmc == 0
    lat = dict(tm=tm, bpb=seq // tm, n_lat=t_lat // tm)
    cx = dict(tm=tmc, bpb=1, n_lat=0)
    tmb = tm if t_ctx % tm == 0 else tmc
    both = dict(tm=tmb, bpb=seq // tmb, n_lat=t_lat // tmb)
    last_reader = max([i for i in range(depth) if i % N_MIXERS == 2], default=-1)

    cvec = jnp.zeros((MOD_ROWS, d), F32).at[:bsz].set(c).at[MOD_ROWS // 2].set(c_ctx)
    mod_all = _modulation(cvec, w_mod, b_mod)
    cos, sin = _rope_tables(seq)

    xl = x.reshape(t_lat, d)
    xc = ctx.reshape(t_ctx, d)
    for i in range(depth):
        m, j = i % N_MIXERS, i // N_MIXERS
        ctx_on = i <= last_reader
        ctx_full = i < last_reader
        mod = mod_all[i]
        g1, g2 = norm_g[i, 0], norm_g[i, 1]
        if m == 0:
            cp = (conv_w_dw[j], conv_b_dw[j], conv_norm_g[j], conv_w_out[j], conv_b_out[j])
            v = _conv_in(xl, g1, mod, conv_w_in[j], conv_b_in[j], **lat)
            xl = _conv_out(v, xl, mod, *cp, **lat)
            if ctx_full:
                vc = _conv_in(xc, g1, mod, conv_w_in[j], conv_b_in[j], **cx)
                xc = _conv_out(vc, xc, mod, *cp, tm=n_ctx, bpb=1, n_lat=0)
        elif m == 1:
            hl = _prenorm(xl, g1, mod, dtype=F32, **lat)
            pq = _seq_dft_two_stage(hl, bsz, seq)
            xl = _fourier_out(pq, xl, mod, fnet_w_out[j], fnet_b_out[j], seq=seq, **lat)
            if ctx_full:
                hc = _prenorm(xc, g1, mod, dtype=BF16, **cx)
                pqc = _seq_dft_dense(hc, bsz, n_ctx)
                xc = _fourier_out(pqc, xc, mod, fnet_w_out[j], fnet_b_out[j], seq=n_ctx, tm=n_ctx, bpb=1, n_lat=0)
        else:
            gq, gk = attn_q_norm_g[j], attn_k_norm_g[j]
            qt, k, vv = _qkv(xl, g1, mod, attn_w_qkv[j], gq, gk, cos, sin, seq=seq, tq=min(ATTN_TQ, seq), **lat)
            ones = jnp.ones((tmc, HEAD_DIM), F32)
            _, kc, vc = _qkv(xc, g1, mod, attn_w_qkv[j], gq, gk, ones, jnp.zeros_like(ones),
                             seq=n_ctx, tq=min(ATTN_TQ, tmc), **cx)
            kv_dim = k.shape[1]
            k_all = jnp.concatenate([kc.reshape(bsz, n_ctx, kv_dim), k.reshape(bsz, seq, kv_dim)], axis=1)
            v_all = jnp.concatenate([vc.reshape(bsz, n_ctx, kv_dim), vv.reshape(bsz, seq, kv_dim)], axis=1)
            lk = seq + n_ctx
            tk = max(t for t in range(LANES, ATTN_MAX_TK + 1, LANES) if lk % t == 0)
            o = _attention(qt, k_all, v_all, tk=tk)
            xl = _proj_residual(o.reshape(t_lat, -1), xl, mod, attn_w_out[j], **lat)
            if ctx_full:
                raise NotImplementedError("context-stream attention output is only needed before the last reader layer")
        mp = (moe_w_group[i], moe_b_group[i], moe_w_expert[i], moe_b_expert[i], moe_w_gate, moe_w_up, moe_w_down, i)
        if ctx_full:
            xa = _hier_moe_residual(jnp.concatenate([xl, xc], axis=0), g2, mod, *mp, **both)
            xl, xc = xa[:t_lat], xa[t_lat:]
        else:
            xl = _hier_moe_residual(xl, g2, mod, *mp, **lat)
    return xl.reshape(bsz, seq, d)
```

```python
import functools
import math

import jax
import jax.numpy as jnp
from jax import lax
from jax.experimental import pallas as pl
from jax.experimental.pallas import tpu as pltpu

F32 = jnp.float32
BF16 = jnp.bfloat16
HIGHEST = lax.Precision.HIGHEST

EPS = 1e-6
GRID_W = 64
N_MIXERS = 3
CONV_WIDTH = 31
CONV_HALO = 16
CONV_ROW_CHUNK = 32
CONV_COL_CHUNK = 256
SUBLANES = 8
FNET_GROUPS = 4
HEAD_DIM = 128
N_KV_HEADS = 2
Q_PER_KV = 4
ROPE_THETA = 10000.0
N_EXPERT_GROUPS = 4
EXPERTS_PER_GROUP = 8
N_EXPERTS = N_EXPERT_GROUPS * EXPERTS_PER_GROUP
MOE_BLOCK = 256
LANES = 128
DFT_NA = 128
MOD_ROWS = 8
VMEM_LIMIT = 56 * 1024 * 1024


def _cparams(*sem):
    return pltpu.CompilerParams(dimension_semantics=sem, vmem_limit_bytes=VMEM_LIMIT)


def _rms(x, g):
    return x * lax.rsqrt(jnp.mean(x * x, axis=-1, keepdims=True) + EPS) * g


def _sigmoid(x):
    return 1.0 / (1.0 + jnp.exp(-x))


def _mod_row(mod_ref, b, k, d):
    return mod_ref[pl.ds(b, 1), pl.ds(k * d, d)]


def _norm_mod(x, g, mod_ref, b, k_shift, d):
    return _rms(x, g) * (1.0 + _mod_row(mod_ref, b, k_shift + 1, d)) + _mod_row(mod_ref, b, k_shift, d)


def _batch_row(bpb, n_lat_blocks):
    i = pl.program_id(0)
    return jnp.where(i < n_lat_blocks, i // bpb, MOD_ROWS // 2)


def _mod_kernel(c_ref, w_ref, b_ref, o_ref):
    c = c_ref[...]
    s = c * _sigmoid(c)
    o_ref[0] = jnp.dot(s, w_ref[0], precision=HIGHEST, preferred_element_type=F32) + b_ref[0]


def _modulation(cvec, w_mod, b_mod):
    depth, d, n = w_mod.shape
    tn = 1024
    return pl.pallas_call(
        _mod_kernel,
        grid=(depth, n // tn),
        in_specs=[
            pl.BlockSpec((MOD_ROWS, d), lambda l, j: (0, 0)),
            pl.BlockSpec((1, d, tn), lambda l, j: (l, 0, j)),
            pl.BlockSpec((1, 1, tn), lambda l, j: (l, 0, j)),
        ],
        out_specs=pl.BlockSpec((1, MOD_ROWS, tn), lambda l, j: (l, 0, j)),
        out_shape=jax.ShapeDtypeStruct((depth, MOD_ROWS, n), F32),
        compiler_params=_cparams("parallel", "parallel"),
        name="adaln_mod",
    )(cvec, w_mod, b_mod.reshape(depth, 1, n))


def _conv_in_kernel(x_ref, g_ref, mod_ref, w_ref, b_ref, v_ref, *, bpb, n_lat):
    d = x_ref.shape[1]
    b = _batch_row(bpb, n_lat)
    h = _norm_mod(x_ref[...], g_ref[...], mod_ref, b, 0, d)
    u = jnp.dot(h.astype(BF16), w_ref[...], preferred_element_type=F32) + b_ref[...]
    v_ref[...] = u[:, :d] * _sigmoid(u[:, d:])


def _conv_in(x, g, mod, w_in, b_in, *, tm, bpb, n_lat):
    t, d = x.shape
    return pl.pallas_call(
        functools.partial(_conv_in_kernel, bpb=bpb, n_lat=n_lat),
        grid=(t // tm,),
        in_specs=[
            pl.BlockSpec((tm, d), lambda i: (i, 0)),
            pl.BlockSpec((1, d), lambda i: (0, 0)),
            pl.BlockSpec(mod.shape, lambda i: (0, 0)),
            pl.BlockSpec((d, 2 * d), lambda i: (0, 0)),
            pl.BlockSpec((1, 2 * d), lambda i: (0, 0)),
        ],
        out_specs=pl.BlockSpec((tm, d), lambda i: (i, 0)),
        out_shape=jax.ShapeDtypeStruct((t, d), F32),
        compiler_params=_cparams("parallel"),
        name="conv_in",
    )(x, g.reshape(1, d), mod, w_in.astype(BF16), b_in.reshape(1, 2 * d))


def _conv_out_kernel(vp_ref, vc_ref, vn_ref, x_ref, wdw_ref, bdw_ref, gn_ref, wo_ref, bo_ref, mod_ref,
                     o_ref, buf, z, sh, *, bpb, n_lat, tm):
    d = x_ref.shape[1]
    i = pl.program_id(0)
    j = i % bpb
    b = _batch_row(bpb, n_lat)
    h = CONV_HALO
    buf[0:h, :] = jnp.where(j == 0, 0.0, vp_ref[...])
    buf[h:h + tm, :] = vc_ref[...]
    buf[h + tm:, :] = jnp.where(j == bpb - 1, 0.0, vn_ref[...])
    rc, cw = CONV_ROW_CHUNK, CONV_COL_CHUNK
    off = h - CONV_WIDTH // 2
    n_sh = sh.shape[1]

    def col_chunk(ci, carry):
        c0 = pl.multiple_of(ci * cw, cw)
        for s in range(SUBLANES):
            sh[s] = buf[s:s + n_sh, pl.ds(c0, cw)]
        for r in range(tm // rc):
            acc = jnp.zeros((rc, cw), F32)
            for t in range(CONV_WIDTH):
                q, s = divmod(off + t, SUBLANES)
                row0 = r * rc + q * SUBLANES
                acc = acc + sh[s, row0:row0 + rc, :] * wdw_ref[t:t + 1, pl.ds(c0, cw)]
            z[r * rc:(r + 1) * rc, pl.ds(c0, cw)] = acc
        return carry

    lax.fori_loop(0, d // cw, col_chunk, 0)
    zz = _rms(z[...] + bdw_ref[...], gn_ref[...])
    zz = zz * _sigmoid(zz)
    y = jnp.dot(zz.astype(BF16), wo_ref[...], preferred_element_type=F32) + bo_ref[...]
    o_ref[...] = x_ref[...] + _mod_row(mod_ref, b, 2, d) * y


def _conv_out(v, x, mod, w_dw, b_dw, g_norm, w_out, b_out, *, tm, bpb, n_lat):
    t, d = x.shape
    hb = tm // CONV_HALO
    n_halo = t // CONV_HALO
    wdw = jnp.zeros((32, d), F32).at[:CONV_WIDTH].set(w_dw)
    n_tap_tiles = (CONV_HALO - CONV_WIDTH // 2 + CONV_WIDTH - 1) // SUBLANES
    return pl.pallas_call(
        functools.partial(_conv_out_kernel, bpb=bpb, n_lat=n_lat, tm=tm),
        grid=(t // tm,),
        in_specs=[
            pl.BlockSpec((CONV_HALO, d), lambda i: (jnp.maximum(i * hb - 1, 0), 0)),
            pl.BlockSpec((tm, d), lambda i: (i, 0)),
            pl.BlockSpec((CONV_HALO, d), lambda i: (jnp.minimum((i + 1) * hb, n_halo - 1), 0)),
            pl.BlockSpec((tm, d), lambda i: (i, 0)),
            pl.BlockSpec((32, d), lambda i: (0, 0)),
            pl.BlockSpec((1, d), lambda i: (0, 0)),
            pl.BlockSpec((1, d), lambda i: (0, 0)),
            pl.BlockSpec((d, d), lambda i: (0, 0)),
            pl.BlockSpec((1, d), lambda i: (0, 0)),
            pl.BlockSpec(mod.shape, lambda i: (0, 0)),
        ],
        out_specs=pl.BlockSpec((tm, d), lambda i: (i, 0)),
        out_shape=jax.ShapeDtypeStruct((t, d), F32),
        scratch_shapes=[
            pltpu.VMEM((tm + 2 * CONV_HALO, d), F32),
            pltpu.VMEM((tm, d), F32),
            pltpu.VMEM((SUBLANES, tm + n_tap_tiles * SUBLANES, CONV_COL_CHUNK), F32),
        ],
        compiler_params=_cparams("parallel"),
        name="conv_out",
    )(v, v, v, x, wdw, b_dw.reshape(1, d), g_norm.reshape(1, d), w_out.astype(BF16), b_out.reshape(1, d), mod)


def _prenorm_kernel(x_ref, g_ref, mod_ref, h_ref, *, bpb, n_lat):
    d = x_ref.shape[1]
    b = _batch_row(bpb, n_lat)
    h_ref[...] = _norm_mod(x_ref[...], g_ref[...], mod_ref, b, 0, d).astype(h_ref.dtype)


def _prenorm(x, g, mod, *, tm, bpb, n_lat, dtype):
    t, d = x.shape
    return pl.pallas_call(
        functools.partial(_prenorm_kernel, bpb=bpb, n_lat=n_lat),
        grid=(t // tm,),
        in_specs=[
            pl.BlockSpec((tm, d), lambda i: (i, 0)),
            pl.BlockSpec((1, d), lambda i: (0, 0)),
            pl.BlockSpec(mod.shape, lambda i: (0, 0)),
        ],
        out_specs=pl.BlockSpec((tm, d), lambda i: (i, 0)),
        out_shape=jax.ShapeDtypeStruct((t, d), dtype),
        compiler_params=_cparams("parallel"),
        name="prenorm",
    )(x, g.reshape(1, d), mod)


def _dft1_kernel(h_ref, m_ref, o_ref, *, nb_chunk, nb, na):
    bc = pl.program_id(2)
    for bi in range(nb_chunk):
        b = bc * nb_chunk + bi
        xs = h_ref[0, pl.ds(b, na, stride=nb), :]
        t = jnp.dot(m_ref[bi], xs.astype(BF16), preferred_element_type=F32)
        o_ref[0, 0, bi] = t[:na].astype(BF16)
        o_ref[0, 1, bi] = t[na:].astype(BF16)


def _dft_tables(seq):
    na, nb = DFT_NA, seq // DFT_NA
    ka = jnp.arange(na, dtype=jnp.int32)
    a = jnp.arange(na, dtype=jnp.int32)
    b = jnp.arange(nb, dtype=jnp.int32)
    n = a[None, None, :] * nb + b[:, None, None]
    ang = ((ka[None, :, None] * n) % seq).astype(F32) * (2.0 * math.pi / seq)
    m1 = jnp.concatenate([jnp.cos(ang), -jnp.sin(ang)], axis=1).astype(BF16)
    kb = jnp.arange(nb, dtype=jnp.int32)
    ang2 = ((kb[:, None] * b[None, :]) % nb).astype(F32) * (2.0 * math.pi / nb)
    c2, s2 = jnp.cos(ang2), jnp.sin(ang2)
    m2 = jnp.concatenate([jnp.concatenate([c2, s2], axis=1),
                          jnp.concatenate([s2, -c2], axis=1)], axis=0).astype(BF16)
    return m1, m2


def _const_lhs_matmul_kernel(a_ref, x_ref, o_ref):
    o_ref[0] = jnp.dot(a_ref[...], x_ref[0], preferred_element_type=F32).astype(o_ref.dtype)


def _const_lhs_matmul(a, x, *, cn):
    m, k = a.shape
    bsz, _, n = x.shape
    return pl.pallas_call(
        _const_lhs_matmul_kernel,
        grid=(bsz, n // cn),
        in_specs=[
            pl.BlockSpec((m, k), lambda b, j: (0, 0)),
            pl.BlockSpec((1, k, cn), lambda b, j: (b, 0, j)),
        ],
        out_specs=pl.BlockSpec((1, m, cn), lambda b, j: (b, 0, j)),
        out_shape=jax.ShapeDtypeStruct((bsz, m, n), BF16),
        compiler_params=_cparams("parallel", "parallel"),
        name="const_lhs_matmul",
    )(a, x)


def _seq_dft_two_stage(h, bsz, seq):
    d = h.shape[1]
    na, nb = DFT_NA, seq // DFT_NA
    gw = LANES
    nb_chunk = min(nb, 8)
    m1, m2 = _dft_tables(seq)
    t1 = pl.pallas_call(
        functools.partial(_dft1_kernel, nb_chunk=nb_chunk, nb=nb, na=na),
        grid=(bsz, d // gw, nb // nb_chunk),
        in_specs=[
            pl.BlockSpec((1, seq, gw), lambda b, g, c: (b, 0, g)),
            pl.BlockSpec((nb_chunk, 2 * na, na), lambda b, g, c: (c, 0, 0)),
        ],
        out_specs=pl.BlockSpec((1, 2, nb_chunk, na, gw), lambda b, g, c: (b, 0, c, 0, g)),
        out_shape=jax.ShapeDtypeStruct((bsz, 2, nb, na, d), BF16),
        compiler_params=_cparams("parallel", "parallel", "arbitrary"),
        name="dft_stage1",
    )(h.reshape(bsz, seq, d), m1)
    pq = _const_lhs_matmul(m2, t1.reshape(bsz, 2 * nb, na * d), cn=min(na * d, 8192))
    return pq.reshape(bsz, 2, seq, d)


def _seq_dft_dense(h, bsz, seq):
    d = h.shape[1]
    k = jnp.arange(seq, dtype=jnp.int32)
    ang = ((k[:, None] * k[None, :]) % seq).astype(F32) * (2.0 * math.pi / seq)
    a = jnp.concatenate([jnp.cos(ang), jnp.sin(ang)], axis=0).astype(BF16)
    pq = _const_lhs_matmul(a, h.reshape(bsz, seq, d), cn=d)
    return pq.reshape(bsz, 2, seq, d)


def _fourier_out_kernel(p_ref, q_ref, x_ref, cm_ref, wo_ref, bo_ref, mod_ref, o_ref, *, bpb, n_lat):
    d = x_ref.shape[1]
    gw = d // FNET_GROUPS
    b = _batch_row(bpb, n_lat)
    p, q = p_ref[0, 0], q_ref[0, 0]
    f = [jnp.dot(p[:, g * gw:(g + 1) * gw], cm_ref[:gw], preferred_element_type=F32)
         + jnp.dot(q[:, g * gw:(g + 1) * gw], cm_ref[gw:], preferred_element_type=F32)
         for g in range(FNET_GROUPS)]
    f = jnp.concatenate(f, axis=1).astype(BF16)
    y = jnp.dot(f, wo_ref[...], preferred_element_type=F32) + bo_ref[...]
    o_ref[...] = x_ref[...] + _mod_row(mod_ref, b, 2, d) * y


def _fourier_out(pq, x, mod, w_out, b_out, *, seq, tm, bpb, n_lat):
    t, d = x.shape
    gw = d // FNET_GROUPS
    k = jnp.arange(gw, dtype=jnp.int32)
    ang = ((k[:, None] * k[None, :]) % gw).astype(F32) * (2.0 * math.pi / gw)
    scale = 1.0 / math.sqrt(seq * gw)
    cm = (jnp.concatenate([jnp.cos(ang), -jnp.sin(ang)], axis=0) * scale).astype(BF16)
    return pl.pallas_call(
        functools.partial(_fourier_out_kernel, bpb=bpb, n_lat=n_lat),
        grid=(t // tm,),
        in_specs=[
            pl.BlockSpec((1, 1, tm, d), lambda i: (i // bpb, 0, i % bpb, 0)),
            pl.BlockSpec((1, 1, tm, d), lambda i: (i // bpb, 1, i % bpb, 0)),
            pl.BlockSpec((tm, d), lambda i: (i, 0)),
            pl.BlockSpec((2 * gw, gw), lambda i: (0, 0)),
            pl.BlockSpec((d, d), lambda i: (0, 0)),
            pl.BlockSpec((1, d), lambda i: (0, 0)),
            pl.BlockSpec(mod.shape, lambda i: (0, 0)),
        ],
        out_specs=pl.BlockSpec((tm, d), lambda i: (i, 0)),
        out_shape=jax.ShapeDtypeStruct((t, d), F32),
        compiler_params=_cparams("parallel"),
        name="fourier_out",
    )(pq, pq, x, cm, w_out.astype(BF16), b_out.reshape(1, d), mod)


def _head_perm():
    return jnp.concatenate([jnp.arange(0, HEAD_DIM, 2), jnp.arange(1, HEAD_DIM, 2)])


def _rope_tables(seq):
    rows = seq // GRID_W
    row = jnp.repeat(jnp.arange(rows, dtype=F32), GRID_W)
    col = jnp.tile(jnp.arange(GRID_W, dtype=F32), rows)
    n_pairs_axis = HEAD_DIM // 4
    inv = ROPE_THETA ** (-jnp.arange(n_pairs_axis, dtype=F32) / n_pairs_axis)
    ang = jnp.concatenate([row[:, None] * inv, col[:, None] * inv], axis=-1)
    cos, sin = jnp.cos(ang), jnp.sin(ang)
    return jnp.concatenate([cos, cos], axis=-1), jnp.concatenate([-sin, sin], axis=-1)


def _qkv_kernel(x_ref, g_ref, mod_ref, w_ref, gq_ref, gk_ref, cos_ref, sin_ref, qt_ref, k_ref, v_ref, *, bpb, n_lat):
    d = x_ref.shape[1]
    b = _batch_row(bpb, n_lat)
    h = _norm_mod(x_ref[...], g_ref[...], mod_ref, b, 0, d)
    u = jnp.dot(h.astype(BF16), w_ref[...], preferred_element_type=F32)
    cos, sin = cos_ref[...], sin_ref[...]
    kv_dim = k_ref.shape[1]
    q_dim = Q_PER_KV * kv_dim
    tq = qt_ref.shape[4] // Q_PER_KV

    def norm_rope(xh, gain):
        xh = _rms(xh, gain)
        return xh * cos + pltpu.roll(xh, HEAD_DIM // 2, 1) * sin

    q_scale = HEAD_DIM ** -0.5 * math.log2(math.e)
    for hh in range(q_dim // HEAD_DIM):
        sl = slice(hh * HEAD_DIM, (hh + 1) * HEAD_DIM)
        kvh, g = divmod(hh, Q_PER_KV)
        q_t = (norm_rope(u[:, sl], gq_ref[...]) * q_scale).T.astype(BF16)
        for qb in range(qt_ref.shape[2]):
            qt_ref[0, kvh, qb, :, g * tq:(g + 1) * tq] = q_t[:, qb * tq:(qb + 1) * tq]
    for hh in range(kv_dim // HEAD_DIM):
        sl = slice(hh * HEAD_DIM, (hh + 1) * HEAD_DIM)
        k_ref[:, sl] = norm_rope(u[:, q_dim + hh * HEAD_DIM:q_dim + (hh + 1) * HEAD_DIM], gk_ref[...]).astype(BF16)
    v_ref[...] = u[:, q_dim + kv_dim:].astype(BF16)


def _qkv(x, g, mod, w_qkv, gq, gk, cos, sin, *, seq, tq, tm, bpb, n_lat):
    t, d = x.shape
    assert tm % tq == 0 and seq % tm == 0
    q_dim = N_KV_HEADS * Q_PER_KV * HEAD_DIM
    kv_dim = N_KV_HEADS * HEAD_DIM
    perm = _head_perm()
    n_heads_qk = (q_dim + kv_dim) // HEAD_DIM
    cols = (jnp.arange(n_heads_qk)[:, None] * HEAD_DIM + perm[None, :]).reshape(-1)
    cols = jnp.concatenate([cols, jnp.arange(q_dim + kv_dim, q_dim + 2 * kv_dim)])
    w = w_qkv[:, cols].astype(BF16)
    n_pos = cos.shape[0] // tm
    return pl.pallas_call(
        functools.partial(_qkv_kernel, bpb=bpb, n_lat=n_lat),
        grid=(t // tm,),
        in_specs=[
            pl.BlockSpec((tm, d), lambda i: (i, 0)),
            pl.BlockSpec((1, d), lambda i: (0, 0)),
            pl.BlockSpec(mod.shape, lambda i: (0, 0)),
            pl.BlockSpec((d, q_dim + 2 * kv_dim), lambda i: (0, 0)),
            pl.BlockSpec((1, HEAD_DIM), lambda i: (0, 0)),
            pl.BlockSpec((1, HEAD_DIM), lambda i: (0, 0)),
            pl.BlockSpec((tm, HEAD_DIM), lambda i: (i % n_pos, 0)),
            pl.BlockSpec((tm, HEAD_DIM), lambda i: (i % n_pos, 0)),
        ],
        out_specs=[
            pl.BlockSpec((1, N_KV_HEADS, tm // tq, HEAD_DIM, Q_PER_KV * tq),
                         lambda i: (i // (seq // tm), 0, i % (seq // tm), 0, 0)),
            pl.BlockSpec((tm, kv_dim), lambda i: (i, 0)),
            pl.BlockSpec((tm, kv_dim), lambda i: (i, 0)),
        ],
        out_shape=[
            jax.ShapeDtypeStruct((t // seq, N_KV_HEADS, seq // tq, HEAD_DIM, Q_PER_KV * tq), BF16),
            jax.ShapeDtypeStruct((t, kv_dim), BF16),
            jax.ShapeDtypeStruct((t, kv_dim), BF16),
        ],
        compiler_params=_cparams("parallel"),
        name="qkv_proj",
    )(x, g.reshape(1, d), mod, w, gq[perm].reshape(1, HEAD_DIM), gk[perm].reshape(1, HEAD_DIM), cos, sin)


ATTN_TQ = 1024
ATTN_ROW_CHUNK = 16
ATTN_MAX_TK = 1408


def _attn_kernel(qt_ref, k_ref, vt_ref, o_ref, *scratch, tq, tk, nk):
    s_ref, p_ref, acc_ref = scratch
    nq = Q_PER_KV * tq
    acc_ref[...] = jnp.zeros(acc_ref.shape, F32)
    rc = ATTN_ROW_CHUNK

    def kv_chunk(j, carry):
        m_prev, l_prev = carry
        kk = k_ref[0, pl.ds(pl.multiple_of(j * tk, tk), tk), :]
        s_ref[...] = jnp.dot(kk, qt_ref[0, 0, 0], preferred_element_type=F32)
        mx = s_ref[0:SUBLANES, :]
        for r in range(1, tk // SUBLANES):
            mx = jnp.maximum(mx, s_ref[r * SUBLANES:(r + 1) * SUBLANES, :])
        m_new = jnp.maximum(m_prev, jnp.max(mx, axis=0, keepdims=True))
        psum = jnp.zeros((SUBLANES, nq), F32)
        for r in range(tk // rc):
            p = jnp.exp2(s_ref[r * rc:(r + 1) * rc, :] - m_new)
            p_ref[r * rc:(r + 1) * rc, :] = p.astype(BF16)
            for h in range(rc // SUBLANES):
                psum = psum + p[h * SUBLANES:(h + 1) * SUBLANES]
        alpha = jnp.exp2(m_prev - m_new)
        acc_ref[...] = alpha * acc_ref[...] + jnp.dot(vt_ref[0, 0, j], p_ref[...], preferred_element_type=F32)
        return m_new, alpha * l_prev + jnp.sum(psum, axis=0, keepdims=True)

    init = (jnp.full((1, nq), -jnp.inf, F32), jnp.zeros((1, nq), F32))
    _, l_all = lax.fori_loop(0, nk, kv_chunk, init)
    out = (acc_ref[...] / l_all).T
    for g in range(Q_PER_KV):
        o_ref[0, :, g * HEAD_DIM:(g + 1) * HEAD_DIM] = out[g * tq:(g + 1) * tq].astype(o_ref.dtype)


def _attention(qt, k_all, v_all, *, tk):
    bsz, _, n_qblk, _, nq = qt.shape
    tq = nq // Q_PER_KV
    seq = n_qblk * tq
    lk = k_all.shape[1]
    gq = Q_PER_KV * HEAD_DIM
    q_dim = N_KV_HEADS * gq
    nk = lk // tk
    assert tk % ATTN_ROW_CHUNK == 0 and nq % LANES == 0
    vt = v_all.reshape(bsz, nk, tk, N_KV_HEADS, HEAD_DIM).transpose(0, 3, 1, 4, 2)
    return pl.pallas_call(
        functools.partial(_attn_kernel, tq=tq, tk=tk, nk=nk),
        grid=(bsz, N_KV_HEADS, seq // tq),
        in_specs=[
            pl.BlockSpec((1, 1, 1, HEAD_DIM, nq), lambda b, h, i: (b, h, i, 0, 0)),
            pl.BlockSpec((1, lk, HEAD_DIM), lambda b, h, i: (b, 0, h)),
            pl.BlockSpec((1, 1, nk, HEAD_DIM, tk), lambda b, h, i: (b, h, 0, 0, 0)),
        ],
        out_specs=pl.BlockSpec((1, tq, gq), lambda b, h, i: (b, i, h)),
        out_shape=jax.ShapeDtypeStruct((bsz, seq, q_dim), BF16),
        scratch_shapes=[pltpu.VMEM((tk, nq), F32), pltpu.VMEM((tk, nq), BF16), pltpu.VMEM((HEAD_DIM, nq), F32)],
        compiler_params=_cparams("parallel", "parallel", "parallel"),
        name="flash_attention",
    )(qt, k_all, vt)


def _proj_residual_kernel(a_ref, x_ref, w_ref, mod_ref, o_ref, *, bpb, n_lat):
    d = x_ref.shape[1]
    b = _batch_row(bpb, n_lat)
    y = jnp.dot(a_ref[...], w_ref[...], preferred_element_type=F32)
    o_ref[...] = x_ref[...] + _mod_row(mod_ref, b, 2, d) * y


def _proj_residual(a, x, mod, w, *, tm, bpb, n_lat):
    t, d = x.shape
    ka = a.shape[1]
    return pl.pallas_call(
        functools.partial(_proj_residual_kernel, bpb=bpb, n_lat=n_lat),
        grid=(t // tm,),
        in_specs=[
            pl.BlockSpec((tm, ka), lambda i: (i, 0)),
            pl.BlockSpec((tm, d), lambda i: (i, 0)),
            pl.BlockSpec((ka, d), lambda i: (0, 0)),
            pl.BlockSpec(mod.shape, lambda i: (0, 0)),
        ],
        out_specs=pl.BlockSpec((tm, d), lambda i: (i, 0)),
        out_shape=jax.ShapeDtypeStruct((t, d), F32),
        compiler_params=_cparams("parallel"),
        name="proj_residual",
    )(a, x, w.astype(BF16), mod)


ROUTE_E, ROUTE_GATE, ROUTE_RANK = 0, 2, 4
ROUTER_LOGIT0 = N_EXPERT_GROUPS


U32 = jnp.uint32
BF16_HIGH_BITS = 0xFFFF0000


def _pack_bf16_pairs(x):
    half = x.shape[1] // 2
    hi = lax.bitcast_convert_type(x[:, :half].astype(BF16).astype(F32), U32)
    lo = lax.bitcast_convert_type(x[:, half:].astype(BF16).astype(F32), U32)
    return (hi & U32(BF16_HIGH_BITS)) | (lo >> 16)


def _unpack_bf16_pairs(w):
    return (lax.bitcast_convert_type(w & U32(BF16_HIGH_BITS), F32), lax.bitcast_convert_type(w << 16, F32))


def _router_kernel(x_ref, g_ref, mod_ref, wr_ref, br_ref, tri_ref, h_ref, r_ref, cnt_ref, carry, *, bpb, n_lat):
    d = x_ref.shape[1]
    tm = x_ref.shape[0]
    i = pl.program_id(0)
    b = _batch_row(bpb, n_lat)

    @pl.when(i == 0)
    def _():
        carry[...] = jnp.zeros(carry.shape, F32)

    h = _norm_mod(x_ref[...], g_ref[...], mod_ref, b, 3, d)
    h_ref[...] = _pack_bf16_pairs(h)
    h_hi = h.astype(BF16)
    h_lo = (h - h_hi.astype(F32)).astype(BF16)
    logits = (jnp.dot(h_hi, wr_ref[0], preferred_element_type=F32) + jnp.dot(h_lo, wr_ref[0], preferred_element_type=F32)
              + jnp.dot(h_hi, wr_ref[1], preferred_element_type=F32) + br_ref[...])
    lane = lax.broadcasted_iota(jnp.int32, (tm, LANES), 1).astype(F32)
    big = float(LANES)
    neg = -jnp.inf
    gmask = lane < N_EXPERT_GROUPS
    gl = jnp.where(gmask, logits, neg)
    ge = jnp.exp(gl - jnp.max(gl, axis=-1, keepdims=True))
    gp = ge / jnp.sum(ge, axis=-1, keepdims=True)
    g_top = jnp.max(gp, axis=-1, keepdims=True)
    g_idx = jnp.min(jnp.where(gmask & (gp == g_top), lane, big), axis=-1, keepdims=True)
    lo = ROUTER_LOGIT0 + g_idx * EXPERTS_PER_GROUP
    emask = (lane >= lo) & (lane < lo + EXPERTS_PER_GROUP)
    el = jnp.where(emask, logits, neg)
    ee = jnp.exp(el - jnp.max(el, axis=-1, keepdims=True))
    ep = ee / jnp.sum(ee, axis=-1, keepdims=True)
    p1 = jnp.max(ep, axis=-1, keepdims=True)
    i1 = jnp.min(jnp.where(emask & (ep == p1), lane, big), axis=-1, keepdims=True)
    rest = emask & (lane != i1)
    p2 = jnp.max(jnp.where(rest, ep, -1.0), axis=-1, keepdims=True)
    i2 = jnp.min(jnp.where(rest & (ep == p2), lane, big), axis=-1, keepdims=True)
    denom = p1 + p2
    gate1 = g_top * p1 / denom
    gate2 = g_top * p2 / denom
    hit1 = lane == i1
    hit2 = lane == i2
    onehot = jnp.where(hit1 | hit2, 1.0, 0.0)
    cum = jnp.dot(tri_ref[...], onehot.astype(BF16), preferred_element_type=F32) + carry[...]
    rank1 = jnp.sum(jnp.where(hit1, cum, 0.0), axis=-1, keepdims=True)
    rank2 = jnp.sum(jnp.where(hit2, cum, 0.0), axis=-1, keepdims=True)
    carry[...] = carry[...] + jnp.sum(onehot, axis=0, keepdims=True)
    cnt_ref[...] = carry[...]
    rec = jnp.zeros((tm, LANES), F32)
    for ln, val in ((ROUTE_E, i1 - ROUTER_LOGIT0), (ROUTE_E + 1, i2 - ROUTER_LOGIT0), (ROUTE_GATE, gate1),
                    (ROUTE_GATE + 1, gate2), (ROUTE_RANK, rank1), (ROUTE_RANK + 1, rank2)):
        rec = jnp.where(lane == float(ln), val, rec)
    r_ref[...] = rec


def _router(x, g, mod, w_group, b_group, w_expert, b_expert, *, tm, bpb, n_lat):
    t, d = x.shape
    n_log = N_EXPERT_GROUPS + N_EXPERTS
    wr = jnp.zeros((d, LANES), F32).at[:, :N_EXPERT_GROUPS].set(w_group).at[:, N_EXPERT_GROUPS:n_log].set(w_expert)
    br = jnp.zeros((1, LANES), F32).at[0, :N_EXPERT_GROUPS].set(b_group).at[0, N_EXPERT_GROUPS:n_log].set(b_expert)
    wr_hi = wr.astype(BF16)
    wr = jnp.stack([wr_hi, (wr - wr_hi.astype(F32)).astype(BF16)])
    tri = (jnp.arange(tm)[:, None] > jnp.arange(tm)[None, :]).astype(BF16)
    return pl.pallas_call(
        functools.partial(_router_kernel, bpb=bpb, n_lat=n_lat),
        grid=(t // tm,),
        in_specs=[
            pl.BlockSpec((tm, d), lambda i: (i, 0)),
            pl.BlockSpec((1, d), lambda i: (0, 0)),
            pl.BlockSpec(mod.shape, lambda i: (0, 0)),
            pl.BlockSpec((2, d, LANES), lambda i: (0, 0, 0)),
            pl.BlockSpec((1, LANES), lambda i: (0, 0)),
            pl.BlockSpec((tm, tm), lambda i: (0, 0)),
        ],
        out_specs=[
            pl.BlockSpec((tm, d // 2), lambda i: (i, 0)),
            pl.BlockSpec((tm, LANES), lambda i: (i, 0)),
            pl.BlockSpec((1, LANES), lambda i: (0, 0)),
        ],
        out_shape=[
            jax.ShapeDtypeStruct((t, d // 2), U32),
            jax.ShapeDtypeStruct((t, LANES), F32),
            jax.ShapeDtypeStruct((1, LANES), F32),
        ],
        scratch_shapes=[pltpu.VMEM((1, LANES), F32)],
        compiler_params=_cparams("arbitrary"),
        name="moe_router",
    )(x, g.reshape(1, d), mod, wr, br, tri)


def _moe_ffn_kernel(be_ref, bv_ref, x_ref, wg_ref, wu_ref, wd_ref, o_ref, wgb, wub, wdb):
    i = pl.program_id(0)
    e = be_ref[i]
    e_prev = be_ref[jnp.maximum(i - 1, 0)]

    @pl.when((i == 0) | (e != e_prev))
    def _():
        wgb[...] = wg_ref[0, 0].astype(BF16)
        wub[...] = wu_ref[0, 0].astype(BF16)
        wdb[...] = wd_ref[0, 0].astype(BF16)

    n_valid = bv_ref[i]

    @pl.when(n_valid > 0)
    def _():
        half = x_ref.shape[1]
        x_a, x_b = (v.astype(BF16) for v in _unpack_bf16_pairs(x_ref[...]))

        def proj(w):
            return (jnp.dot(x_a, w[:half], preferred_element_type=F32)
                    + jnp.dot(x_b, w[half:], preferred_element_type=F32))

        gt, up = proj(wgb), proj(wub)
        a = gt * _sigmoid(gt) * up
        o_ref[...] = _pack_bf16_pairs(jnp.dot(a.astype(BF16), wdb[...], preferred_element_type=F32))

    @pl.when(n_valid <= 0)
    def _():
        o_ref[...] = jnp.zeros(o_ref.shape, U32)


def _moe_ffn(buf, blk_expert, blk_valid, w_gate, w_up, w_down, layer):
    p = buf.shape[0]
    d = w_gate.shape[-2]
    de = w_gate.shape[-1]
    n_blocks = p // MOE_BLOCK
    grid_spec = pltpu.PrefetchScalarGridSpec(
        num_scalar_prefetch=2,
        grid=(n_blocks,),
        in_specs=[
            pl.BlockSpec((MOE_BLOCK, d // 2), lambda i, be, bv: (i, 0)),
            pl.BlockSpec((1, 1, d, de), lambda i, be, bv: (layer, be[i], 0, 0)),
            pl.BlockSpec((1, 1, d, de), lambda i, be, bv: (layer, be[i], 0, 0)),
            pl.BlockSpec((1, 1, de, d), lambda i, be, bv: (layer, be[i], 0, 0)),
        ],
        out_specs=pl.BlockSpec((MOE_BLOCK, d // 2), lambda i, be, bv: (i, 0)),
        scratch_shapes=[pltpu.VMEM((d, de), BF16), pltpu.VMEM((d, de), BF16), pltpu.VMEM((de, d), BF16)],
    )
    return pl.pallas_call(
        _moe_ffn_kernel,
        grid_spec=grid_spec,
        out_shape=jax.ShapeDtypeStruct((p, d // 2), U32),
        compiler_params=_cparams("arbitrary"),
        name="moe_ffn",
    )(blk_expert, blk_valid, buf, w_gate, w_up, w_down)


DMA_ISSUE_UNROLL = 8


def _row_copy(src, src_row, dst, dst_row, sem):
    return pltpu.make_async_copy(src.at[pl.ds(src_row, 1)], dst.at[pl.ds(dst_row, 1)], sem)


def _moe_dispatch_kernel(pe_ref, dest_ref, h_ref, buf_ref, zero_ref, sems, zsem):
    tm = h_ref.shape[0] * SUBLANES
    n_blocks = buf_ref.shape[0] // MOE_BLOCK

    def zero_block(first_row):
        return pltpu.make_async_copy(zero_ref, buf_ref.at[pl.ds(pl.multiple_of(first_row, MOE_BLOCK), MOE_BLOCK)], zsem)

    @pl.when(pl.program_id(0) == 0)
    def _():
        zero_ref[...] = jnp.zeros(zero_ref.shape, zero_ref.dtype)

        def expert_has_rows(e):
            return pe_ref[e] > jnp.where(e > 0, pe_ref[jnp.maximum(e - 1, 0)], 0)

        def fill(start):
            def body(e, carry):
                @pl.when(expert_has_rows(e))
                def _():
                    cp = zero_block(pe_ref[e] - MOE_BLOCK)
                    cp.start() if start else cp.wait()
                return carry
            return body

        def tail(start):
            def body(b, carry):
                cp = zero_block(b * MOE_BLOCK)
                cp.start() if start else cp.wait()
                return carry
            return body

        first_unused = pe_ref[N_EXPERTS - 1] // MOE_BLOCK
        for start in (True, False):
            lax.fori_loop(0, N_EXPERTS, fill(start), 0)
            lax.fori_loop(first_unused, n_blocks, tail(start), 0)

    def issue(tile, carry):
        for k in range(SUBLANES):
            for j in range(2):
                slot = dest_ref[0, 0, 2 * SUBLANES * tile + 2 * k + j]
                pltpu.make_async_copy(h_ref.at[tile, pl.ds(k, 1)], buf_ref.at[pl.ds(slot, 1)], sems.at[j]).start(priority=j)
        return carry

    lax.fori_loop(0, tm // SUBLANES, issue, 0)
    for j in range(2):
        rows = buf_ref.at[pl.ds(0, tm)]
        pltpu.make_async_copy(rows, rows, sems.at[j]).wait()


def _moe_dispatch(h, dest, pad_end, n_slots, *, tm):
    t, d = h.shape
    grid_spec = pltpu.PrefetchScalarGridSpec(
        num_scalar_prefetch=1,
        grid=(t // tm,),
        in_specs=[
            pl.BlockSpec((1, 1, 2 * tm), lambda i, pe: (i, 0, 0), memory_space=pltpu.SMEM),
            pl.BlockSpec((tm // SUBLANES, SUBLANES, d), lambda i, pe: (i, 0, 0)),
        ],
        out_specs=pl.BlockSpec(memory_space=pl.ANY),
        scratch_shapes=[pltpu.VMEM((MOE_BLOCK, d), h.dtype), pltpu.SemaphoreType.DMA((2,)), pltpu.SemaphoreType.DMA(())],
    )
    return pl.pallas_call(
        _moe_dispatch_kernel,
        grid_spec=grid_spec,
        out_shape=jax.ShapeDtypeStruct((n_slots, d), h.dtype),
        compiler_params=_cparams("arbitrary"),
        name="moe_dispatch",
    )(pad_end.astype(jnp.int32), dest.reshape(t // tm, 1, 2 * tm), h.reshape(t // SUBLANES, SUBLANES, d))


def _moe_combine_kernel(dest_ref, dest_next_ref, x_ref, r_ref, mod_ref, yb_ref, o_ref, *scratch, bpb, n_lat):
    tm, d = x_ref.shape
    ys, sems = (scratch[0:2], scratch[2:4]), scratch[4]
    i = pl.program_id(0)
    n_steps = pl.num_programs(0)
    b = _batch_row(bpb, n_lat)

    def issue_block(idx_ref, buf):
        def issue(tile, carry):
            for k in range(SUBLANES):
                for j in range(2):
                    slot = idx_ref[0, 0, 2 * SUBLANES * tile + 2 * k + j]
                    pltpu.make_async_copy(yb_ref.at[pl.ds(slot, 1)], ys[buf][j].at[tile, pl.ds(k, 1)],
                                          sems.at[buf, j]).start(priority=j)
            return carry

        lax.fori_loop(0, tm // SUBLANES, issue, 0)

    def step(buf):
        @pl.when(i == 0)
        def _():
            issue_block(dest_ref, buf)

        @pl.when(i + 1 < n_steps)
        def _():
            issue_block(dest_next_ref, 1 - buf)

        for j in range(2):
            rows = yb_ref.at[pl.ds(0, tm)]
            pltpu.make_async_copy(rows, rows, sems.at[buf, j]).wait()
        r = r_ref[...]
        half = d // 2
        gated = [tuple(v * r[:, ROUTE_GATE + j:ROUTE_GATE + j + 1]
                       for v in _unpack_bf16_pairs(ys[buf][j][...].reshape(tm, half))) for j in range(2)]
        gate = _mod_row(mod_ref, b, 5, d)
        for c in range(2):
            cols = slice(c * half, (c + 1) * half)
            o_ref[:, cols] = x_ref[:, cols] + gate[:, cols] * (gated[0][c] + gated[1][c])

    for buf in range(2):
        pl.when(i % 2 == buf)(functools.partial(step, buf))


def _moe_combine(x, yb, dest, route, mod, *, tm, bpb, n_lat):
    t, d = x.shape
    n_steps = t // tm
    dest = dest.reshape(n_steps, 1, 2 * tm)
    row = pl.BlockSpec((tm, d), lambda i: (i, 0))
    return pl.pallas_call(
        functools.partial(_moe_combine_kernel, bpb=bpb, n_lat=n_lat),
        grid=(n_steps,),
        in_specs=[
            pl.BlockSpec((1, 1, 2 * tm), lambda i: (i, 0, 0), memory_space=pltpu.SMEM),
            pl.BlockSpec((1, 1, 2 * tm), lambda i: (jnp.minimum(i + 1, n_steps - 1), 0, 0), memory_space=pltpu.SMEM),
            row,
            pl.BlockSpec((tm, LANES), lambda i: (i, 0)),
            pl.BlockSpec(mod.shape, lambda i: (0, 0)),
            pl.BlockSpec(memory_space=pl.ANY),
        ],
        out_specs=row,
        out_shape=jax.ShapeDtypeStruct((t, d), F32),
        scratch_shapes=[pltpu.VMEM((tm // SUBLANES, SUBLANES, d // 2), U32) for _ in range(4)] + [pltpu.SemaphoreType.DMA((2, 2))],
        compiler_params=_cparams("arbitrary"),
        name="moe_combine",
    )(dest, dest, x, route, mod, yb)


def _hier_moe_residual(x, g, mod, w_group, b_group, w_expert, b_expert, w_gate, w_up, w_down, layer, *, tm, bpb, n_lat):
    t, d = x.shape
    h, route, counts = _router(x, g, mod, w_group, b_group, w_expert, b_expert, tm=tm, bpb=bpb, n_lat=n_lat)
    counts = counts[0, ROUTER_LOGIT0:ROUTER_LOGIT0 + N_EXPERTS].astype(jnp.int32)
    expert = route[:, ROUTE_E:ROUTE_E + 2].astype(jnp.int32)
    rank = route[:, ROUTE_RANK:ROUTE_RANK + 2].astype(jnp.int32)
    padded = (counts + MOE_BLOCK - 1) // MOE_BLOCK * MOE_BLOCK
    pad_end = jnp.cumsum(padded)
    pad_start = pad_end - padded
    dest = jnp.sum(jnp.where(expert[:, :, None] == jnp.arange(N_EXPERTS), pad_start, 0), axis=-1) + rank
    n_blocks = -(-(2 * t) // MOE_BLOCK) + N_EXPERTS
    blk_first = jnp.arange(n_blocks, dtype=jnp.int32) * MOE_BLOCK
    blk_expert = jnp.minimum(jnp.sum(blk_first[:, None] >= pad_end[None, :], axis=1), N_EXPERTS - 1).astype(jnp.int32)
    blk_rows = jnp.clip((pad_start + counts)[blk_expert] - blk_first, 0, MOE_BLOCK).astype(jnp.int32)
    buf = _moe_dispatch(h, dest, pad_end, n_blocks * MOE_BLOCK, tm=tm)
    yb = _moe_ffn(buf, blk_expert, blk_rows, w_gate, w_up, w_down, layer)
    return _moe_combine(x, yb, dest, route, mod, tm=tm, bpb=bpb, n_lat=n_lat)


def kernel(x, c, ctx, c_ctx, norm_g, w_mod, b_mod, conv_w_in, conv_b_in, conv_w_dw, conv_b_dw, conv_norm_g, conv_w_out, conv_b_out, fnet_w_out, fnet_b_out, attn_w_qkv, attn_q_norm_g, attn_k_norm_g, attn_w_out, moe_w_group, moe_b_group, moe_w_expert, moe_b_expert, moe_w_gate, moe_w_up, moe_w_down):
    bsz, seq, d = x.shape
    n_ctx = ctx.shape[1]
    depth = norm_g.shape[0]
    assert bsz <= MOD_ROWS // 2 and seq % DFT_NA == 0 and seq % GRID_W == 0
    t_lat, t_ctx = bsz * seq, bsz * n_ctx
    tm = min(512, seq)
    tmc = min(tm, n_ctx)
    assert seq % tm == 0 and n_ctx % tmc == 0 and tm % tmc == 0
    lat = dict(tm=tm, bpb=seq // tm, n_lat=t_lat // tm)
    cx = dict(tm=tmc, bpb=1, n_lat=0)
    tmb = tm if t_ctx % tm == 0 else tmc
    both = dict(tm=tmb, bpb=seq // tmb, n_lat=t_lat // tmb)
    last_reader = max([i for i in range(depth) if i % N_MIXERS == 2], default=-1)

    cvec = jnp.zeros((MOD_ROWS, d), F32).at[:bsz].set(c).at[MOD_ROWS // 2].set(c_ctx)
    mod_all = _modulation(cvec, w_mod, b_mod)
    cos, sin = _rope_tables(seq)

    xl = x.reshape(t_lat, d)
    xc = ctx.reshape(t_ctx, d)
    for i in range(depth):
        m, j = i % N_MIXERS, i // N_MIXERS
        ctx_on = i <= last_reader
        ctx_full = i < last_reader
        mod = mod_all[i]
        g1, g2 = norm_g[i, 0], norm_g[i, 1]
        if m == 0:
            cp = (conv_w_dw[j], conv_b_dw[j], conv_norm_g[j], conv_w_out[j], conv_b_out[j])
            v = _conv_in(xl, g1, mod, conv_w_in[j], conv_b_in[j], **lat)
            xl = _conv_out(v, xl, mod, *cp, **lat)
            if ctx_full:
                vc = _conv_in(xc, g1, mod, conv_w_in[j], conv_b_in[j], **cx)
                xc = _conv_out(vc, xc, mod, *cp, tm=n_ctx, bpb=1, n_lat=0)
        elif m == 1:
            hl = _prenorm(xl, g1, mod, dtype=F32, **lat)
            pq = _seq_dft_two_stage(hl, bsz, seq)
            xl = _fourier_out(pq, xl, mod, fnet_w_out[j], fnet_b_out[j], seq=seq, **lat)
            if ctx_full:
                hc = _prenorm(xc, g1, mod, dtype=BF16, **cx)
                pqc = _seq_dft_dense(hc, bsz, n_ctx)
                xc = _fourier_out(pqc, xc, mod, fnet_w_out[j], fnet_b_out[j], seq=n_ctx, tm=n_ctx, bpb=1, n_lat=0)
        else:
            gq, gk = attn_q_norm_g[j], attn_k_norm_g[j]
            tq = min(ATTN_TQ, seq)
            tmq = max(tm, tq)
            qt, k, vv = _qkv(xl, g1, mod, attn_w_qkv[j], gq, gk, cos, sin, seq=seq, tq=tq,
                             tm=tmq, bpb=seq // tmq, n_lat=t_lat // tmq)
            ones = jnp.ones((tmc, HEAD_DIM), F32)
            _, kc, vc = _qkv(xc, g1, mod, attn_w_qkv[j], gq, gk, ones, jnp.zeros_like(ones),
                             seq=n_ctx, tq=min(ATTN_TQ, tmc), **cx)
            kv_dim = k.shape[1]
            k_all = jnp.concatenate([kc.reshape(bsz, n_ctx, kv_dim), k.reshape(bsz, seq, kv_dim)], axis=1)
            v_all = jnp.concatenate([vc.reshape(bsz, n_ctx, kv_dim), vv.reshape(bsz, seq, kv_dim)], axis=1)
            lk = seq + n_ctx
            tk = max(t for t in range(LANES, ATTN_MAX_TK + 1, LANES) if lk % t == 0)
            o = _attention(qt, k_all, v_all, tk=tk)
            xl = _proj_residual(o.reshape(t_lat, -1), xl, mod, attn_w_out[j], **lat)
            if ctx_full:
                raise NotImplementedError("context-stream attention output is only needed before the last reader layer")
        mp = (moe_w_group[i], moe_b_group[i], moe_w_expert[i], moe_b_expert[i], moe_w_gate, moe_w_up, moe_w_down, i)
        if ctx_full:
            xa = _hier_moe_residual(jnp.concatenate([xl, xc], axis=0), g2, mod, *mp, **both)
            xl, xc = xa[:t_lat], xa[t_lat:]
        else:
            xl = _hier_moe_residual(xl, g2, mod, *mp, **lat)
    return xl.reshape(bsz, seq, d)
```

```python
import functools
import math

import jax
import jax.numpy as jnp
from jax import lax
from jax.experimental import pallas as pl
from jax.experimental.pallas import tpu as pltpu

F32 = jnp.float32
BF16 = jnp.bfloat16
HIGHEST = lax.Precision.HIGHEST

EPS = 1e-6
GRID_W = 64
N_MIXERS = 3
CONV_WIDTH = 31
CONV_HALO = 16
CONV_ROW_CHUNK = 32
CONV_COL_CHUNK = 256
SUBLANES = 8
FNET_GROUPS = 4
HEAD_DIM = 128
N_KV_HEADS = 2
Q_PER_KV = 4
ROPE_THETA = 10000.0
N_EXPERT_GROUPS = 4
EXPERTS_PER_GROUP = 8
N_EXPERTS = N_EXPERT_GROUPS * EXPERTS_PER_GROUP
MOE_BLOCK = 256
LANES = 128
DFT_NA = 128
MOD_ROWS = 8
VMEM_LIMIT = 56 * 1024 * 1024


def _cparams(*sem):
    return pltpu.CompilerParams(dimension_semantics=sem, vmem_limit_bytes=VMEM_LIMIT)


def _rms(x, g):
    return x * lax.rsqrt(jnp.mean(x * x, axis=-1, keepdims=True) + EPS) * g


def _sigmoid(x):
    return 1.0 / (1.0 + jnp.exp(-x))


def _mod_row(mod_ref, b, k, d):
    return mod_ref[pl.ds(b, 1), pl.ds(k * d, d)]


def _norm_mod(x, g, mod_ref, b, k_shift, d):
    return _rms(x, g) * (1.0 + _mod_row(mod_ref, b, k_shift + 1, d)) + _mod_row(mod_ref, b, k_shift, d)


def _batch_row(bpb, n_lat_blocks):
    i = pl.program_id(0)
    return jnp.where(i < n_lat_blocks, i // bpb, MOD_ROWS // 2)


def _mod_kernel(c_ref, w_ref, b_ref, o_ref):
    c = c_ref[...]
    s = c * _sigmoid(c)
    o_ref[0] = jnp.dot(s, w_ref[0], precision=HIGHEST, preferred_element_type=F32) + b_ref[0]


def _modulation(cvec, w_mod, b_mod):
    depth, d, n = w_mod.shape
    tn = 1024
    return pl.pallas_call(
        _mod_kernel,
        grid=(depth, n // tn),
        in_specs=[
            pl.BlockSpec((MOD_ROWS, d), lambda l, j: (0, 0)),
            pl.BlockSpec((1, d, tn), lambda l, j: (l, 0, j)),
            pl.BlockSpec((1, 1, tn), lambda l, j: (l, 0, j)),
        ],
        out_specs=pl.BlockSpec((1, MOD_ROWS, tn), lambda l, j: (l, 0, j)),
        out_shape=jax.ShapeDtypeStruct((depth, MOD_ROWS, n), F32),
        compiler_params=_cparams("parallel", "parallel"),
        name="adaln_mod",
    )(cvec, w_mod, b_mod.reshape(depth, 1, n))


def _conv_in_kernel(x_ref, g_ref, mod_ref, w_ref, b_ref, v_ref, *, bpb, n_lat):
    d = x_ref.shape[1]
    b = _batch_row(bpb, n_lat)
    h = _norm_mod(x_ref[...], g_ref[...], mod_ref, b, 0, d)
    u = jnp.dot(h.astype(BF16), w_ref[...], preferred_element_type=F32) + b_ref[...]
    v_ref[...] = u[:, :d] * _sigmoid(u[:, d:])


def _conv_in(x, g, mod, w_in, b_in, *, tm, bpb, n_lat):
    t, d = x.shape
    return pl.pallas_call(
        functools.partial(_conv_in_kernel, bpb=bpb, n_lat=n_lat),
        grid=(t // tm,),
        in_specs=[
            pl.BlockSpec((tm, d), lambda i: (i, 0)),
            pl.BlockSpec((1, d), lambda i: (0, 0)),
            pl.BlockSpec(mod.shape, lambda i: (0, 0)),
            pl.BlockSpec((d, 2 * d), lambda i: (0, 0)),
            pl.BlockSpec((1, 2 * d), lambda i: (0, 0)),
        ],
        out_specs=pl.BlockSpec((tm, d), lambda i: (i, 0)),
        out_shape=jax.ShapeDtypeStruct((t, d), F32),
        compiler_params=_cparams("parallel"),
        name="conv_in",
    )(x, g.reshape(1, d), mod, w_in.astype(BF16), b_in.reshape(1, 2 * d))


def _conv_out_kernel(vp_ref, vc_ref, vn_ref, x_ref, wdw_ref, bdw_ref, gn_ref, wo_ref, bo_ref, mod_ref,
                     o_ref, buf, z, sh, *, bpb, n_lat, tm):
    d = x_ref.shape[1]
    i = pl.program_id(0)
    j = i % bpb
    b = _batch_row(bpb, n_lat)
    h = CONV_HALO
    buf[0:h, :] = jnp.where(j == 0, 0.0, vp_ref[...])
    buf[h:h + tm, :] = vc_ref[...]
    buf[h + tm:, :] = jnp.where(j == bpb - 1, 0.0, vn_ref[...])
    rc, cw = CONV_ROW_CHUNK, CONV_COL_CHUNK
    off = h - CONV_WIDTH // 2
    n_sh = sh.shape[1]

    def col_chunk(ci, carry):
        c0 = pl.multiple_of(ci * cw, cw)
        for s in range(SUBLANES):
            sh[s] = buf[s:s + n_sh, pl.ds(c0, cw)]
        for r in range(tm // rc):
            acc = jnp.zeros((rc, cw), F32)
            for t in range(CONV_WIDTH):
                q, s = divmod(off + t, SUBLANES)
                row0 = r * rc + q * SUBLANES
                acc = acc + sh[s, row0:row0 + rc, :] * wdw_ref[t:t + 1, pl.ds(c0, cw)]
            z[r * rc:(r + 1) * rc, pl.ds(c0, cw)] = acc
        return carry

    lax.fori_loop(0, d // cw, col_chunk, 0)
    zz = _rms(z[...] + bdw_ref[...], gn_ref[...])
    zz = zz * _sigmoid(zz)
    y = jnp.dot(zz.astype(BF16), wo_ref[...], preferred_element_type=F32) + bo_ref[...]
    o_ref[...] = x_ref[...] + _mod_row(mod_ref, b, 2, d) * y


def _conv_out(v, x, mod, w_dw, b_dw, g_norm, w_out, b_out, *, tm, bpb, n_lat):
    t, d = x.shape
    hb = tm // CONV_HALO
    n_halo = t // CONV_HALO
    wdw = jnp.zeros((32, d), F32).at[:CONV_WIDTH].set(w_dw)
    n_tap_tiles = (CONV_HALO - CONV_WIDTH // 2 + CONV_WIDTH - 1) // SUBLANES
    return pl.pallas_call(
        functools.partial(_conv_out_kernel, bpb=bpb, n_lat=n_lat, tm=tm),
        grid=(t // tm,),
        in_specs=[
            pl.BlockSpec((CONV_HALO, d), lambda i: (jnp.maximum(i * hb - 1, 0), 0)),
            pl.BlockSpec((tm, d), lambda i: (i, 0)),
            pl.BlockSpec((CONV_HALO, d), lambda i: (jnp.minimum((i + 1) * hb, n_halo - 1), 0)),
            pl.BlockSpec((tm, d), lambda i: (i, 0)),
            pl.BlockSpec((32, d), lambda i: (0, 0)),
            pl.BlockSpec((1, d), lambda i: (0, 0)),
            pl.BlockSpec((1, d), lambda i: (0, 0)),
            pl.BlockSpec((d, d), lambda i: (0, 0)),
            pl.BlockSpec((1, d), lambda i: (0, 0)),
            pl.BlockSpec(mod.shape, lambda i: (0, 0)),
        ],
        out_specs=pl.BlockSpec((tm, d), lambda i: (i, 0)),
        out_shape=jax.ShapeDtypeStruct((t, d), F32),
        scratch_shapes=[
            pltpu.VMEM((tm + 2 * CONV_HALO, d), F32),
            pltpu.VMEM((tm, d), F32),
            pltpu.VMEM((SUBLANES, tm + n_tap_tiles * SUBLANES, CONV_COL_CHUNK), F32),
        ],
        compiler_params=_cparams("parallel"),
        name="conv_out",
    )(v, v, v, x, wdw, b_dw.reshape(1, d), g_norm.reshape(1, d), w_out.astype(BF16), b_out.reshape(1, d), mod)


def _prenorm_kernel(x_ref, g_ref, mod_ref, h_ref, *, bpb, n_lat):
    d = x_ref.shape[1]
    b = _batch_row(bpb, n_lat)
    h_ref[...] = _norm_mod(x_ref[...], g_ref[...], mod_ref, b, 0, d).astype(h_ref.dtype)


def _prenorm(x, g, mod, *, tm, bpb, n_lat, dtype):
    t, d = x.shape
    return pl.pallas_call(
        functools.partial(_prenorm_kernel, bpb=bpb, n_lat=n_lat),
        grid=(t // tm,),
        in_specs=[
            pl.BlockSpec((tm, d), lambda i: (i, 0)),
            pl.BlockSpec((1, d), lambda i: (0, 0)),
            pl.BlockSpec(mod.shape, lambda i: (0, 0)),
        ],
        out_specs=pl.BlockSpec((tm, d), lambda i: (i, 0)),
        out_shape=jax.ShapeDtypeStruct((t, d), dtype),
        compiler_params=_cparams("parallel"),
        name="prenorm",
    )(x, g.reshape(1, d), mod)


def _dft1_kernel(h_ref, m_ref, o_ref, *, nb_chunk, nb, na):
    bc = pl.program_id(2)
    for bi in range(nb_chunk):
        b = bc * nb_chunk + bi
        xs = h_ref[0, pl.ds(b, na, stride=nb), :]
        t = jnp.dot(m_ref[bi], xs.astype(BF16), preferred_element_type=F32)
        o_ref[0, 0, bi] = t[:na].astype(BF16)
        o_ref[0, 1, bi] = t[na:].astype(BF16)


def _dft_tables(seq):
    na, nb = DFT_NA, seq // DFT_NA
    ka = jnp.arange(na, dtype=jnp.int32)
    a = jnp.arange(na, dtype=jnp.int32)
    b = jnp.arange(nb, dtype=jnp.int32)
    n = a[None, None, :] * nb + b[:, None, None]
    ang = ((ka[None, :, None] * n) % seq).astype(F32) * (2.0 * math.pi / seq)
    m1 = jnp.concatenate([jnp.cos(ang), -jnp.sin(ang)], axis=1).astype(BF16)
    kb = jnp.arange(nb, dtype=jnp.int32)
    ang2 = ((kb[:, None] * b[None, :]) % nb).astype(F32) * (2.0 * math.pi / nb)
    c2, s2 = jnp.cos(ang2), jnp.sin(ang2)
    m2 = jnp.concatenate([jnp.concatenate([c2, s2], axis=1),
                          jnp.concatenate([s2, -c2], axis=1)], axis=0).astype(BF16)
    return m1, m2


def _const_lhs_matmul_kernel(a_ref, x_ref, o_ref):
    o_ref[0] = jnp.dot(a_ref[...], x_ref[0], preferred_element_type=F32).astype(o_ref.dtype)


def _const_lhs_matmul(a, x, *, cn):
    m, k = a.shape
    bsz, _, n = x.shape
    return pl.pallas_call(
        _const_lhs_matmul_kernel,
        grid=(bsz, n // cn),
        in_specs=[
            pl.BlockSpec((m, k), lambda b, j: (0, 0)),
            pl.BlockSpec((1, k, cn), lambda b, j: (b, 0, j)),
        ],
        out_specs=pl.BlockSpec((1, m, cn), lambda b, j: (b, 0, j)),
        out_shape=jax.ShapeDtypeStruct((bsz, m, n), BF16),
        compiler_params=_cparams("parallel", "parallel"),
        name="const_lhs_matmul",
    )(a, x)


def _seq_dft_two_stage(h, bsz, seq):
    d = h.shape[1]
    na, nb = DFT_NA, seq // DFT_NA
    gw = LANES
    nb_chunk = min(nb, 8)
    m1, m2 = _dft_tables(seq)
    t1 = pl.pallas_call(
        functools.partial(_dft1_kernel, nb_chunk=nb_chunk, nb=nb, na=na),
        grid=(bsz, d // gw, nb // nb_chunk),
        in_specs=[
            pl.BlockSpec((1, seq, gw), lambda b, g, c: (b, 0, g)),
            pl.BlockSpec((nb_chunk, 2 * na, na), lambda b, g, c: (c, 0, 0)),
        ],
        out_specs=pl.BlockSpec((1, 2, nb_chunk, na, gw), lambda b, g, c: (b, 0, c, 0, g)),
        out_shape=jax.ShapeDtypeStruct((bsz, 2, nb, na, d), BF16),
        compiler_params=_cparams("parallel", "parallel", "arbitrary"),
        name="dft_stage1",
    )(h.reshape(bsz, seq, d), m1)
    pq = _const_lhs_matmul(m2, t1.reshape(bsz, 2 * nb, na * d), cn=min(na * d, 8192))
    return pq.reshape(bsz, 2, seq, d)


def _seq_dft_dense(h, bsz, seq):
    d = h.shape[1]
    k = jnp.arange(seq, dtype=jnp.int32)
    ang = ((k[:, None] * k[None, :]) % seq).astype(F32) * (2.0 * math.pi / seq)
    a = jnp.concatenate([jnp.cos(ang), jnp.sin(ang)], axis=0).astype(BF16)
    pq = _const_lhs_matmul(a, h.reshape(bsz, seq, d), cn=d)
    return pq.reshape(bsz, 2, seq, d)


def _fourier_out_kernel(p_ref, q_ref, x_ref, cm_ref, wo_ref, bo_ref, mod_ref, o_ref, *, bpb, n_lat):
    d = x_ref.shape[1]
    gw = d // FNET_GROUPS
    b = _batch_row(bpb, n_lat)
    p, q = p_ref[0, 0], q_ref[0, 0]
    f = [jnp.dot(p[:, g * gw:(g + 1) * gw], cm_ref[:gw], preferred_element_type=F32)
         + jnp.dot(q[:, g * gw:(g + 1) * gw], cm_ref[gw:], preferred_element_type=F32)
         for g in range(FNET_GROUPS)]
    f = jnp.concatenate(f, axis=1).astype(BF16)
    y = jnp.dot(f, wo_ref[...], preferred_element_type=F32) + bo_ref[...]
    o_ref[...] = x_ref[...] + _mod_row(mod_ref, b, 2, d) * y


def _fourier_out(pq, x, mod, w_out, b_out, *, seq, tm, bpb, n_lat):
    t, d = x.shape
    gw = d // FNET_GROUPS
    k = jnp.arange(gw, dtype=jnp.int32)
    ang = ((k[:, None] * k[None, :]) % gw).astype(F32) * (2.0 * math.pi / gw)
    scale = 1.0 / math.sqrt(seq * gw)
    cm = (jnp.concatenate([jnp.cos(ang), -jnp.sin(ang)], axis=0) * scale).astype(BF16)
    return pl.pallas_call(
        functools.partial(_fourier_out_kernel, bpb=bpb, n_lat=n_lat),
        grid=(t // tm,),
        in_specs=[
            pl.BlockSpec((1, 1, tm, d), lambda i: (i // bpb, 0, i % bpb, 0)),
            pl.BlockSpec((1, 1, tm, d), lambda i: (i // bpb, 1, i % bpb, 0)),
            pl.BlockSpec((tm, d), lambda i: (i, 0)),
            pl.BlockSpec((2 * gw, gw), lambda i: (0, 0)),
            pl.BlockSpec((d, d), lambda i: (0, 0)),
            pl.BlockSpec((1, d), lambda i: (0, 0)),
            pl.BlockSpec(mod.shape, lambda i: (0, 0)),
        ],
        out_specs=pl.BlockSpec((tm, d), lambda i: (i, 0)),
        out_shape=jax.ShapeDtypeStruct((t, d), F32),
        compiler_params=_cparams("parallel"),
        name="fourier_out",
    )(pq, pq, x, cm, w_out.astype(BF16), b_out.reshape(1, d), mod)


def _head_perm():
    return jnp.concatenate([jnp.arange(0, HEAD_DIM, 2), jnp.arange(1, HEAD_DIM, 2)])


def _rope_tables(seq):
    rows = seq // GRID_W
    row = jnp.repeat(jnp.arange(rows, dtype=F32), GRID_W)
    col = jnp.tile(jnp.arange(GRID_W, dtype=F32), rows)
    n_pairs_axis = HEAD_DIM // 4
    inv = ROPE_THETA ** (-jnp.arange(n_pairs_axis, dtype=F32) / n_pairs_axis)
    ang = jnp.concatenate([row[:, None] * inv, col[:, None] * inv], axis=-1)
    cos, sin = jnp.cos(ang), jnp.sin(ang)
    return jnp.concatenate([cos, cos], axis=-1), jnp.concatenate([-sin, sin], axis=-1)


def _qkv_kernel(x_ref, g_ref, mod_ref, w_ref, gq_ref, gk_ref, cos_ref, sin_ref, qt_ref, k_ref, v_ref, *, bpb, n_lat, n_sub):
    tm, d = x_ref.shape
    b = _batch_row(bpb, n_lat)
    h = _norm_mod(x_ref[...], g_ref[...], mod_ref, b, 0, d)
    u = jnp.dot(h.astype(BF16), w_ref[...], preferred_element_type=F32)
    cos, sin = cos_ref[...], sin_ref[...]
    kv_dim = k_ref.shape[1]
    q_dim = Q_PER_KV * kv_dim
    tq = qt_ref.shape[4] // Q_PER_KV

    def norm_rope(xh, gain):
        xh = _rms(xh, gain)
        return xh * cos + pltpu.roll(xh, HEAD_DIM // 2, 1) * sin

    q_scale = HEAD_DIM ** -0.5 * math.log2(math.e)
    q_ts = [(norm_rope(u[:, hh * HEAD_DIM:(hh + 1) * HEAD_DIM], gq_ref[...]) * q_scale).T.astype(BF16)
            for hh in range(q_dim // HEAD_DIM)]

    def store_q(lane0, width, qb_of):
        for hh, q_t in enumerate(q_ts):
            kvh, g = divmod(hh, Q_PER_KV)
            for qb in range(qt_ref.shape[2]):
                qt_ref[0, kvh, qb, :, g * tq + lane0:g * tq + lane0 + width] = q_t[:, qb_of(qb):qb_of(qb) + width]

    if n_sub == 1:
        store_q(0, tq, lambda qb: qb * tq)
    else:
        for s in range(n_sub):
            pl.when(pl.program_id(0) % n_sub == s)(functools.partial(store_q, s * tm, tm, lambda qb: 0))
    for hh in range(kv_dim // HEAD_DIM):
        sl = slice(hh * HEAD_DIM, (hh + 1) * HEAD_DIM)
        k_ref[:, sl] = norm_rope(u[:, q_dim + hh * HEAD_DIM:q_dim + (hh + 1) * HEAD_DIM], gk_ref[...]).astype(BF16)
    v_ref[...] = u[:, q_dim + kv_dim:].astype(BF16)


def _qkv(x, g, mod, w_qkv, gq, gk, cos, sin, *, seq, tq, tm, bpb, n_lat):
    t, d = x.shape
    assert (tm % tq == 0 or tq % tm == 0) and seq % tm == 0 and seq % tq == 0
    n_sub = max(tq // tm, 1)
    n_qb = max(tm // tq, 1)
    row_blocks = seq // tm
    q_dim = N_KV_HEADS * Q_PER_KV * HEAD_DIM
    kv_dim = N_KV_HEADS * HEAD_DIM
    perm = _head_perm()
    n_heads_qk = (q_dim + kv_dim) // HEAD_DIM
    cols = (jnp.arange(n_heads_qk)[:, None] * HEAD_DIM + perm[None, :]).reshape(-1)
    cols = jnp.concatenate([cols, jnp.arange(q_dim + kv_dim, q_dim + 2 * kv_dim)])
    w = w_qkv[:, cols].astype(BF16)
    n_pos = cos.shape[0] // tm
    return pl.pallas_call(
        functools.partial(_qkv_kernel, bpb=bpb, n_lat=n_lat, n_sub=n_sub),
        grid=(t // tm,),
        in_specs=[
            pl.BlockSpec((tm, d), lambda i: (i, 0)),
            pl.BlockSpec((1, d), lambda i: (0, 0)),
            pl.BlockSpec(mod.shape, lambda i: (0, 0)),
            pl.BlockSpec((d, q_dim + 2 * kv_dim), lambda i: (0, 0)),
            pl.BlockSpec((1, HEAD_DIM), lambda i: (0, 0)),
            pl.BlockSpec((1, HEAD_DIM), lambda i: (0, 0)),
            pl.BlockSpec((tm, HEAD_DIM), lambda i: (i % n_pos, 0)),
            pl.BlockSpec((tm, HEAD_DIM), lambda i: (i % n_pos, 0)),
        ],
        out_specs=[
            pl.BlockSpec((1, N_KV_HEADS, n_qb, HEAD_DIM, Q_PER_KV * tq),
                         lambda i: (i // row_blocks, 0, (i % row_blocks) // n_sub, 0, 0)),
            pl.BlockSpec((tm, kv_dim), lambda i: (i, 0)),
            pl.BlockSpec((tm, kv_dim), lambda i: (i, 0)),
        ],
        out_shape=[
            jax.ShapeDtypeStruct((t // seq, N_KV_HEADS, seq // tq, HEAD_DIM, Q_PER_KV * tq), BF16),
            jax.ShapeDtypeStruct((t, kv_dim), BF16),
            jax.ShapeDtypeStruct((t, kv_dim), BF16),
        ],
        compiler_params=_cparams("arbitrary"),
        name="qkv_proj",
    )(x, g.reshape(1, d), mod, w, gq[perm].reshape(1, HEAD_DIM), gk[perm].reshape(1, HEAD_DIM), cos, sin)


ATTN_TQ = 1024
ATTN_ROW_CHUNK = 16
ATTN_MAX_TK = 1408


def _attn_kernel(qt_ref, k_ref, vt_ref, o_ref, *scratch, tq, tk, nk):
    s_ref, p_ref, acc_ref = scratch
    nq = Q_PER_KV * tq
    acc_ref[...] = jnp.zeros(acc_ref.shape, F32)
    rc = ATTN_ROW_CHUNK

    def kv_chunk(j, carry):
        m_prev, l_prev = carry
        kk = k_ref[0, pl.ds(pl.multiple_of(j * tk, tk), tk), :]
        s_ref[...] = jnp.dot(kk, qt_ref[0, 0, 0], preferred_element_type=F32)
        mx = s_ref[0:SUBLANES, :]
        for r in range(1, tk // SUBLANES):
            mx = jnp.maximum(mx, s_ref[r * SUBLANES:(r + 1) * SUBLANES, :])
        m_new = jnp.maximum(m_prev, jnp.max(mx, axis=0, keepdims=True))
        psum = jnp.zeros((SUBLANES, nq), F32)
        for r in range(tk // rc):
            p = jnp.exp2(s_ref[r * rc:(r + 1) * rc, :] - m_new)
            p_ref[r * rc:(r + 1) * rc, :] = p.astype(BF16)
            for h in range(rc // SUBLANES):
                psum = psum + p[h * SUBLANES:(h + 1) * SUBLANES]
        alpha = jnp.exp2(m_prev - m_new)
        acc_ref[...] = alpha * acc_ref[...] + jnp.dot(vt_ref[0, 0, j], p_ref[...], preferred_element_type=F32)
        return m_new, alpha * l_prev + jnp.sum(psum, axis=0, keepdims=True)

    init = (jnp.full((1, nq), -jnp.inf, F32), jnp.zeros((1, nq), F32))
    _, l_all = lax.fori_loop(0, nk, kv_chunk, init)
    out = (acc_ref[...] / l_all).T
    for g in range(Q_PER_KV):
        o_ref[0, :, g * HEAD_DIM:(g + 1) * HEAD_DIM] = out[g * tq:(g + 1) * tq].astype(o_ref.dtype)


def _attention(qt, k_all, v_all, *, tk):
    bsz, _, n_qblk, _, nq = qt.shape
    tq = nq // Q_PER_KV
    seq = n_qblk * tq
    lk = k_all.shape[1]
    gq = Q_PER_KV * HEAD_DIM
    q_dim = N_KV_HEADS * gq
    nk = lk // tk
    assert tk % ATTN_ROW_CHUNK == 0 and nq % LANES == 0
    vt = v_all.reshape(bsz, nk, tk, N_KV_HEADS, HEAD_DIM).transpose(0, 3, 1, 4, 2)
    return pl.pallas_call(
        functools.partial(_attn_kernel, tq=tq, tk=tk, nk=nk),
        grid=(bsz, N_KV_HEADS, seq // tq),
        in_specs=[
            pl.BlockSpec((1, 1, 1, HEAD_DIM, nq), lambda b, h, i: (b, h, i, 0, 0)),
            pl.BlockSpec((1, lk, HEAD_DIM), lambda b, h, i: (b, 0, h)),
            pl.BlockSpec((1, 1, nk, HEAD_DIM, tk), lambda b, h, i: (b, h, 0, 0, 0)),
        ],
        out_specs=pl.BlockSpec((1, tq, gq), lambda b, h, i: (b, i, h)),
        out_shape=jax.ShapeDtypeStruct((bsz, seq, q_dim), BF16),
        scratch_shapes=[pltpu.VMEM((tk, nq), F32), pltpu.VMEM((tk, nq), BF16), pltpu.VMEM((HEAD_DIM, nq), F32)],
        compiler_params=_cparams("parallel", "parallel", "parallel"),
        name="flash_attention",
    )(qt, k_all, vt)


def _proj_residual_kernel(a_ref, x_ref, w_ref, mod_ref, o_ref, *, bpb, n_lat):
    d = x_ref.shape[1]
    b = _batch_row(bpb, n_lat)
    y = jnp.dot(a_ref[...], w_ref[...], preferred_element_type=F32)
    o_ref[...] = x_ref[...] + _mod_row(mod_ref, b, 2, d) * y


def _proj_residual(a, x, mod, w, *, tm, bpb, n_lat):
    t, d = x.shape
    ka = a.shape[1]
    return pl.pallas_call(
        functools.partial(_proj_residual_kernel, bpb=bpb, n_lat=n_lat),
        grid=(t // tm,),
        in_specs=[
            pl.BlockSpec((tm, ka), lambda i: (i, 0)),
            pl.BlockSpec((tm, d), lambda i: (i, 0)),
            pl.BlockSpec((ka, d), lambda i: (0, 0)),
            pl.BlockSpec(mod.shape, lambda i: (0, 0)),
        ],
        out_specs=pl.BlockSpec((tm, d), lambda i: (i, 0)),
        out_shape=jax.ShapeDtypeStruct((t, d), F32),
        compiler_params=_cparams("parallel"),
        name="proj_residual",
    )(a, x, w.astype(BF16), mod)


ROUTE_E, ROUTE_GATE, ROUTE_RANK = 0, 2, 4
ROUTER_LOGIT0 = N_EXPERT_GROUPS


U32 = jnp.uint32
BF16_HIGH_BITS = 0xFFFF0000


def _pack_bf16_pairs(x):
    half = x.shape[1] // 2
    hi = lax.bitcast_convert_type(x[:, :half].astype(BF16).astype(F32), U32)
    lo = lax.bitcast_convert_type(x[:, half:].astype(BF16).astype(F32), U32)
    return (hi & U32(BF16_HIGH_BITS)) | (lo >> 16)


def _unpack_bf16_pairs(w):
    return (lax.bitcast_convert_type(w & U32(BF16_HIGH_BITS), F32), lax.bitcast_convert_type(w << 16, F32))


def _router_kernel(x_ref, g_ref, mod_ref, wr_ref, br_ref, tri_ref, h_ref, r_ref, cnt_ref, carry, *, bpb, n_lat):
    d = x_ref.shape[1]
    tm = x_ref.shape[0]
    i = pl.program_id(0)
    b = _batch_row(bpb, n_lat)

    @pl.when(i == 0)
    def _():
        carry[...] = jnp.zeros(carry.shape, F32)

    h = _norm_mod(x_ref[...], g_ref[...], mod_ref, b, 3, d)
    h_ref[...] = _pack_bf16_pairs(h)
    h_hi = h.astype(BF16)
    h_lo = (h - h_hi.astype(F32)).astype(BF16)
    logits = (jnp.dot(h_hi, wr_ref[0], preferred_element_type=F32) + jnp.dot(h_lo, wr_ref[0], preferred_element_type=F32)
              + jnp.dot(h_hi, wr_ref[1], preferred_element_type=F32) + br_ref[...])
    lane = lax.broadcasted_iota(jnp.int32, (tm, LANES), 1).astype(F32)
    big = float(LANES)
    neg = -jnp.inf
    gmask = lane < N_EXPERT_GROUPS
    gl = jnp.where(gmask, logits, neg)
    ge = jnp.exp(gl - jnp.max(gl, axis=-1, keepdims=True))
    gp = ge / jnp.sum(ge, axis=-1, keepdims=True)
    g_top = jnp.max(gp, axis=-1, keepdims=True)
    g_idx = jnp.min(jnp.where(gmask & (gp == g_top), lane, big), axis=-1, keepdims=True)
    lo = ROUTER_LOGIT0 + g_idx * EXPERTS_PER_GROUP
    emask = (lane >= lo) & (lane < lo + EXPERTS_PER_GROUP)
    el = jnp.where(emask, logits, neg)
    ee = jnp.exp(el - jnp.max(el, axis=-1, keepdims=True))
    ep = ee / jnp.sum(ee, axis=-1, keepdims=True)
    p1 = jnp.max(ep, axis=-1, keepdims=True)
    i1 = jnp.min(jnp.where(emask & (ep == p1), lane, big), axis=-1, keepdims=True)
    rest = emask & (lane != i1)
    p2 = jnp.max(jnp.where(rest, ep, -1.0), axis=-1, keepdims=True)
    i2 = jnp.min(jnp.where(rest & (ep == p2), lane, big), axis=-1, keepdims=True)
    denom = p1 + p2
    gate1 = g_top * p1 / denom
    gate2 = g_top * p2 / denom
    hit1 = lane == i1
    hit2 = lane == i2
    onehot = jnp.where(hit1 | hit2, 1.0, 0.0)
    cum = jnp.dot(tri_ref[...], onehot.astype(BF16), preferred_element_type=F32) + carry[...]
    rank1 = jnp.sum(jnp.where(hit1, cum, 0.0), axis=-1, keepdims=True)
    rank2 = jnp.sum(jnp.where(hit2, cum, 0.0), axis=-1, keepdims=True)
    carry[...] = carry[...] + jnp.sum(onehot, axis=0, keepdims=True)
    cnt_ref[...] = carry[...]
    rec = jnp.zeros((tm, LANES), F32)
    for ln, val in ((ROUTE_E, i1 - ROUTER_LOGIT0), (ROUTE_E + 1, i2 - ROUTER_LOGIT0), (ROUTE_GATE, gate1),
                    (ROUTE_GATE + 1, gate2), (ROUTE_RANK, rank1), (ROUTE_RANK + 1, rank2)):
        rec = jnp.where(lane == float(ln), val, rec)
    r_ref[...] = rec


def _router(x, g, mod, w_group, b_group, w_expert, b_expert, *, tm, bpb, n_lat):
    t, d = x.shape
    n_log = N_EXPERT_GROUPS + N_EXPERTS
    wr = jnp.zeros((d, LANES), F32).at[:, :N_EXPERT_GROUPS].set(w_group).at[:, N_EXPERT_GROUPS:n_log].set(w_expert)
    br = jnp.zeros((1, LANES), F32).at[0, :N_EXPERT_GROUPS].set(b_group).at[0, N_EXPERT_GROUPS:n_log].set(b_expert)
    wr_hi = wr.astype(BF16)
    wr = jnp.stack([wr_hi, (wr - wr_hi.astype(F32)).astype(BF16)])
    tri = (jnp.arange(tm)[:, None] > jnp.arange(tm)[None, :]).astype(BF16)
    return pl.pallas_call(
        functools.partial(_router_kernel, bpb=bpb, n_lat=n_lat),
        grid=(t // tm,),
        in_specs=[
            pl.BlockSpec((tm, d), lambda i: (i, 0)),
            pl.BlockSpec((1, d), lambda i: (0, 0)),
            pl.BlockSpec(mod.shape, lambda i: (0, 0)),
            pl.BlockSpec((2, d, LANES), lambda i: (0, 0, 0)),
            pl.BlockSpec((1, LANES), lambda i: (0, 0)),
            pl.BlockSpec((tm, tm), lambda i: (0, 0)),
        ],
        out_specs=[
            pl.BlockSpec((tm, d // 2), lambda i: (i, 0)),
            pl.BlockSpec((tm, LANES), lambda i: (i, 0)),
            pl.BlockSpec((1, LANES), lambda i: (0, 0)),
        ],
        out_shape=[
            jax.ShapeDtypeStruct((t, d // 2), U32),
            jax.ShapeDtypeStruct((t, LANES), F32),
            jax.ShapeDtypeStruct((1, LANES), F32),
        ],
        scratch_shapes=[pltpu.VMEM((1, LANES), F32)],
        compiler_params=_cparams("arbitrary"),
        name="moe_router",
    )(x, g.reshape(1, d), mod, wr, br, tri)


def _moe_ffn_kernel(be_ref, bv_ref, x_ref, wg_ref, wu_ref, wd_ref, o_ref, wgb, wub, wdb):
    i = pl.program_id(0)
    e = be_ref[i]
    e_prev = be_ref[jnp.maximum(i - 1, 0)]

    @pl.when((i == 0) | (e != e_prev))
    def _():
        wgb[...] = wg_ref[0, 0].astype(BF16)
        wub[...] = wu_ref[0, 0].astype(BF16)
        wdb[...] = wd_ref[0, 0].astype(BF16)

    n_valid = bv_ref[i]

    @pl.when(n_valid > 0)
    def _():
        half = x_ref.shape[1]
        x_a, x_b = (v.astype(BF16) for v in _unpack_bf16_pairs(x_ref[...]))

        def proj(w):
            return (jnp.dot(x_a, w[:half], preferred_element_type=F32)
                    + jnp.dot(x_b, w[half:], preferred_element_type=F32))

        gt, up = proj(wgb), proj(wub)
        a = gt * _sigmoid(gt) * up
        o_ref[...] = _pack_bf16_pairs(jnp.dot(a.astype(BF16), wdb[...], preferred_element_type=F32))

    @pl.when(n_valid <= 0)
    def _():
        o_ref[...] = jnp.zeros(o_ref.shape, U32)


def _moe_ffn(buf, blk_expert, blk_valid, w_gate, w_up, w_down, layer):
    p = buf.shape[0]
    d = w_gate.shape[-2]
    de = w_gate.shape[-1]
    n_blocks = p // MOE_BLOCK
    grid_spec = pltpu.PrefetchScalarGridSpec(
        num_scalar_prefetch=2,
        grid=(n_blocks,),
        in_specs=[
            pl.BlockSpec((MOE_BLOCK, d // 2), lambda i, be, bv: (i, 0)),
            pl.BlockSpec((1, 1, d, de), lambda i, be, bv: (layer, be[i], 0, 0)),
            pl.BlockSpec((1, 1, d, de), lambda i, be, bv: (layer, be[i], 0, 0)),
            pl.BlockSpec((1, 1, de, d), lambda i, be, bv: (layer, be[i], 0, 0)),
        ],
        out_specs=pl.BlockSpec((MOE_BLOCK, d // 2), lambda i, be, bv: (i, 0)),
        scratch_shapes=[pltpu.VMEM((d, de), BF16), pltpu.VMEM((d, de), BF16), pltpu.VMEM((de, d), BF16)],
    )
    return pl.pallas_call(
        _moe_ffn_kernel,
        grid_spec=grid_spec,
        out_shape=jax.ShapeDtypeStruct((p, d // 2), U32),
        compiler_params=_cparams("arbitrary"),
        name="moe_ffn",
    )(blk_expert, blk_valid, buf, w_gate, w_up, w_down)


DMA_ISSUE_UNROLL = 8


def _row_copy(src, src_row, dst, dst_row, sem):
    return pltpu.make_async_copy(src.at[pl.ds(src_row, 1)], dst.at[pl.ds(dst_row, 1)], sem)


def _moe_dispatch_kernel(pe_ref, dest_ref, h_ref, buf_ref, zero_ref, sems, zsem):
    tm = h_ref.shape[0] * SUBLANES
    n_blocks = buf_ref.shape[0] // MOE_BLOCK

    def zero_block(first_row):
        return pltpu.make_async_copy(zero_ref, buf_ref.at[pl.ds(pl.multiple_of(first_row, MOE_BLOCK), MOE_BLOCK)], zsem)

    @pl.when(pl.program_id(0) == 0)
    def _():
        zero_ref[...] = jnp.zeros(zero_ref.shape, zero_ref.dtype)

        def expert_has_rows(e):
            return pe_ref[e] > jnp.where(e > 0, pe_ref[jnp.maximum(e - 1, 0)], 0)

        def fill(start):
            def body(e, carry):
                @pl.when(expert_has_rows(e))
                def _():
                    cp = zero_block(pe_ref[e] - MOE_BLOCK)
                    cp.start() if start else cp.wait()
                return carry
            return body

        def tail(start):
            def body(b, carry):
                cp = zero_block(b * MOE_BLOCK)
                cp.start() if start else cp.wait()
                return carry
            return body

        first_unused = pe_ref[N_EXPERTS - 1] // MOE_BLOCK
        for start in (True, False):
            lax.fori_loop(0, N_EXPERTS, fill(start), 0)
            lax.fori_loop(first_unused, n_blocks, tail(start), 0)

    def issue(tile, carry):
        for k in range(SUBLANES):
            for j in range(2):
                slot = dest_ref[0, 0, 2 * SUBLANES * tile + 2 * k + j]
                pltpu.make_async_copy(h_ref.at[tile, pl.ds(k, 1)], buf_ref.at[pl.ds(slot, 1)], sems.at[j]).start(priority=j)
        return carry

    lax.fori_loop(0, tm // SUBLANES, issue, 0)
    for j in range(2):
        rows = buf_ref.at[pl.ds(0, tm)]
        pltpu.make_async_copy(rows, rows, sems.at[j]).wait()


def _moe_dispatch(h, dest, pad_end, n_slots, *, tm):
    t, d = h.shape
    grid_spec = pltpu.PrefetchScalarGridSpec(
        num_scalar_prefetch=1,
        grid=(t // tm,),
        in_specs=[
            pl.BlockSpec((1, 1, 2 * tm), lambda i, pe: (i, 0, 0), memory_space=pltpu.SMEM),
            pl.BlockSpec((tm // SUBLANES, SUBLANES, d), lambda i, pe: (i, 0, 0)),
        ],
        out_specs=pl.BlockSpec(memory_space=pl.ANY),
        scratch_shapes=[pltpu.VMEM((MOE_BLOCK, d), h.dtype), pltpu.SemaphoreType.DMA((2,)), pltpu.SemaphoreType.DMA(())],
    )
    return pl.pallas_call(
        _moe_dispatch_kernel,
        grid_spec=grid_spec,
        out_shape=jax.ShapeDtypeStruct((n_slots, d), h.dtype),
        compiler_params=_cparams("arbitrary"),
        name="moe_dispatch",
    )(pad_end.astype(jnp.int32), dest.reshape(t // tm, 1, 2 * tm), h.reshape(t // SUBLANES, SUBLANES, d))


def _moe_combine_kernel(dest_ref, dest_next_ref, x_ref, r_ref, mod_ref, yb_ref, o_ref, *scratch, bpb, n_lat):
    tm, d = x_ref.shape
    ys, sems = (scratch[0:2], scratch[2:4]), scratch[4]
    i = pl.program_id(0)
    n_steps = pl.num_programs(0)
    b = _batch_row(bpb, n_lat)

    def issue_block(idx_ref, buf):
        def issue(tile, carry):
            for k in range(SUBLANES):
                for j in range(2):
                    slot = idx_ref[0, 0, 2 * SUBLANES * tile + 2 * k + j]
                    pltpu.make_async_copy(yb_ref.at[pl.ds(slot, 1)], ys[buf][j].at[tile, pl.ds(k, 1)],
                                          sems.at[buf, j]).start(priority=j)
            return carry

        lax.fori_loop(0, tm // SUBLANES, issue, 0)

    def step(buf):
        @pl.when(i == 0)
        def _():
            issue_block(dest_ref, buf)

        @pl.when(i + 1 < n_steps)
        def _():
            issue_block(dest_next_ref, 1 - buf)

        for j in range(2):
            rows = yb_ref.at[pl.ds(0, tm)]
            pltpu.make_async_copy(rows, rows, sems.at[buf, j]).wait()
        r = r_ref[...]
        half = d // 2
        gated = [tuple(v * r[:, ROUTE_GATE + j:ROUTE_GATE + j + 1]
                       for v in _unpack_bf16_pairs(ys[buf][j][...].reshape(tm, half))) for j in range(2)]
        gate = _mod_row(mod_ref, b, 5, d)
        for c in range(2):
            cols = slice(c * half, (c + 1) * half)
            o_ref[:, cols] = x_ref[:, cols] + gate[:, cols] * (gated[0][c] + gated[1][c])

    for buf in range(2):
        pl.when(i % 2 == buf)(functools.partial(step, buf))


def _moe_combine(x, yb, dest, route, mod, *, tm, bpb, n_lat):
    t, d = x.shape
    n_steps = t // tm
    dest = dest.reshape(n_steps, 1, 2 * tm)
    row = pl.BlockSpec((tm, d), lambda i: (i, 0))
    return pl.pallas_call(
        functools.partial(_moe_combine_kernel, bpb=bpb, n_lat=n_lat),
        grid=(n_steps,),
        in_specs=[
            pl.BlockSpec((1, 1, 2 * tm), lambda i: (i, 0, 0), memory_space=pltpu.SMEM),
            pl.BlockSpec((1, 1, 2 * tm), lambda i: (jnp.minimum(i + 1, n_steps - 1), 0, 0), memory_space=pltpu.SMEM),
            row,
            pl.BlockSpec((tm, LANES), lambda i: (i, 0)),
            pl.BlockSpec(mod.shape, lambda i: (0, 0)),
            pl.BlockSpec(memory_space=pl.ANY),
        ],
        out_specs=row,
        out_shape=jax.ShapeDtypeStruct((t, d), F32),
        scratch_shapes=[pltpu.VMEM((tm // SUBLANES, SUBLANES, d // 2), U32) for _ in range(4)] + [pltpu.SemaphoreType.DMA((2, 2))],
        compiler_params=_cparams("arbitrary"),
        name="moe_combine",
    )(dest, dest, x, route, mod, yb)


def _hier_moe_residual(x, g, mod, w_group, b_group, w_expert, b_expert, w_gate, w_up, w_down, layer, *, tm, bpb, n_lat):
    t, d = x.shape
    h, route, counts = _router(x, g, mod, w_group, b_group, w_expert, b_expert, tm=tm, bpb=bpb, n_lat=n_lat)
    counts = counts[0, ROUTER_LOGIT0:ROUTER_LOGIT0 + N_EXPERTS].astype(jnp.int32)
    expert = route[:, ROUTE_E:ROUTE_E + 2].astype(jnp.int32)
    rank = route[:, ROUTE_RANK:ROUTE_RANK + 2].astype(jnp.int32)
    padded = (counts + MOE_BLOCK - 1) // MOE_BLOCK * MOE_BLOCK
    pad_end = jnp.cumsum(padded)
    pad_start = pad_end - padded
    dest = jnp.sum(jnp.where(expert[:, :, None] == jnp.arange(N_EXPERTS), pad_start, 0), axis=-1) + rank
    n_blocks = -(-(2 * t) // MOE_BLOCK) + N_EXPERTS
    blk_first = jnp.arange(n_blocks, dtype=jnp.int32) * MOE_BLOCK
    blk_expert = jnp.minimum(jnp.sum(blk_first[:, None] >= pad_end[None, :], axis=1), N_EXPERTS - 1).astype(jnp.int32)
    blk_rows = jnp.clip((pad_start + counts)[blk_expert] - blk_first, 0, MOE_BLOCK).astype(jnp.int32)
    buf = _moe_dispatch(h, dest, pad_end, n_blocks * MOE_BLOCK, tm=tm)
    yb = _moe_ffn(buf, blk_expert, blk_rows, w_gate, w_up, w_down, layer)
    return _moe_combine(x, yb, dest, route, mod, tm=tm, bpb=bpb, n_lat=n_lat)


def kernel(x, c, ctx, c_ctx, norm_g, w_mod, b_mod, conv_w_in, conv_b_in, conv_w_dw, conv_b_dw, conv_norm_g, conv_w_out, conv_b_out, fnet_w_out, fnet_b_out, attn_w_qkv, attn_q_norm_g, attn_k_norm_g, attn_w_out, moe_w_group, moe_b_group, moe_w_expert, moe_b_expert, moe_w_gate, moe_w_up, moe_w_down):
    bsz, seq, d = x.shape
    n_ctx = ctx.shape[1]
    depth = norm_g.shape[0]
    assert bsz <= MOD_ROWS // 2 and seq % DFT_NA == 0 and seq % GRID_W == 0
    t_lat, t_ctx = bsz * seq, bsz * n_ctx
    tm = min(512, seq)
    tmc = min(tm, n_ctx)
    assert seq % tm == 0 and n_ctx % tmc == 0 and tm % tmc == 0
    lat = dict(tm=tm, bpb=seq // tm, n_lat=t_lat // tm)
    cx = dict(tm=tmc, bpb=1, n_lat=0)
    tmb = tm if t_ctx % tm == 0 else tmc
    both = dict(tm=tmb, bpb=seq // tmb, n_lat=t_lat // tmb)
    last_reader = max([i for i in range(depth) if i % N_MIXERS == 2], default=-1)

    cvec = jnp.zeros((MOD_ROWS, d), F32).at[:bsz].set(c).at[MOD_ROWS // 2].set(c_ctx)
    mod_all = _modulation(cvec, w_mod, b_mod)
    cos, sin = _rope_tables(seq)

    xl = x.reshape(t_lat, d)
    xc = ctx.reshape(t_ctx, d)
    for i in range(depth):
        m, j = i % N_MIXERS, i // N_MIXERS
        ctx_on = i <= last_reader
        ctx_full = i < last_reader
        mod = mod_all[i]
        g1, g2 = norm_g[i, 0], norm_g[i, 1]
        if m == 0:
            cp = (conv_w_dw[j], conv_b_dw[j], conv_norm_g[j], conv_w_out[j], conv_b_out[j])
            v = _conv_in(xl, g1, mod, conv_w_in[j], conv_b_in[j], **lat)
            xl = _conv_out(v, xl, mod, *cp, **lat)
            if ctx_full:
                vc = _conv_in(xc, g1, mod, conv_w_in[j], conv_b_in[j], **cx)
                xc = _conv_out(vc, xc, mod, *cp, tm=n_ctx, bpb=1, n_lat=0)
        elif m == 1:
            hl = _prenorm(xl, g1, mod, dtype=F32, **lat)
            pq = _seq_dft_two_stage(hl, bsz, seq)
            xl = _fourier_out(pq, xl, mod, fnet_w_out[j], fnet_b_out[j], seq=seq, **lat)
            if ctx_full:
                hc = _prenorm(xc, g1, mod, dtype=BF16, **cx)
                pqc = _seq_dft_dense(hc, bsz, n_ctx)
                xc = _fourier_out(pqc, xc, mod, fnet_w_out[j], fnet_b_out[j], seq=n_ctx, tm=n_ctx, bpb=1, n_lat=0)
        else:
            gq, gk = attn_q_norm_g[j], attn_k_norm_g[j]
            qt, k, vv = _qkv(xl, g1, mod, attn_w_qkv[j], gq, gk, cos, sin, seq=seq, tq=min(ATTN_TQ, seq), **lat)
            ones = jnp.ones((tmc, HEAD_DIM), F32)
            _, kc, vc = _qkv(xc, g1, mod, attn_w_qkv[j], gq, gk, ones, jnp.zeros_like(ones),
                             seq=n_ctx, tq=min(ATTN_TQ, tmc), **cx)
            kv_dim = k.shape[1]
            k_all = jnp.concatenate([kc.reshape(bsz, n_ctx, kv_dim), k.reshape(bsz, seq, kv_dim)], axis=1)
            v_all = jnp.concatenate([vc.reshape(bsz, n_ctx, kv_dim), vv.reshape(bsz, seq, kv_dim)], axis=1)
            lk = seq + n_ctx
            tk = max(t for t in range(LANES, ATTN_MAX_TK + 1, LANES) if lk % t == 0)
            o = _attention(qt, k_all, v_all, tk=tk)
            xl = _proj_residual(o.reshape(t_lat, -1), xl, mod, attn_w_out[j], **lat)
            if ctx_full:
                raise NotImplementedError("context-stream attention output is only needed before the last reader layer")
        mp = (moe_w_group[i], moe_b_group[i], moe_w_expert[i], moe_b_expert[i], moe_w_gate, moe_w_up, moe_w_down, i)
        if ctx_full:
            xa = _hier_moe_residual(jnp.concatenate([xl, xc], axis=0), g2, mod, *mp, **both)
            xl, xc = xa[:t_lat], xa[t_lat:]
        else:
            xl = _hier_moe_residual(xl, g2, mod, *mp, **lat)
    return xl.reshape(bsz, seq, d)
```
